```python
import math
import jax, jax.numpy as jnp
from jax import lax
import numpy as np

D_MODEL = 1024
BATCH = 16
SEQ = 2048
DEPTH = 1

N_MEM = 256
FOX_WIDTH = D_MODEL // 2
HEAD_DIM = 64
N_FOX_HEADS = FOX_WIDTH // HEAD_DIM
Q_BLOCK = 128
S5_WIDTH = D_MODEL - FOX_WIDTH
S5_GROUP_CH = 16
S5_GROUPS = S5_WIDTH // S5_GROUP_CH
S5_STATE = 64
N_X_HEADS = 4
X_HEAD_DIM = D_MODEL // N_X_HEADS
D_FF = 128 * ((8 * D_MODEL // 3 + 127) // 128)
CONV_W = 3
IN_COLS = 3 * FOX_WIDTH + N_FOX_HEADS + S5_WIDTH
EPS = 1e-6

kernel_name = "fox_s5_parallel_hybrid_layer"


def _rms_norm(x, g):
    xf = x.astype(jnp.float32)
    y = xf * lax.rsqrt(jnp.mean(xf * xf, axis=-1, keepdims=True) + EPS)
    return (y * g.astype(jnp.float32)).astype(x.dtype)


def _fox_attention(q, k, v, log_f):
    L = q.shape[2]
    c = jnp.cumsum(log_f, axis=-1)
    scale = HEAD_DIM ** -0.5
    outs = []
    for blk in range(L // Q_BLOCK):
        qs = blk * Q_BLOCK
        ke = qs + Q_BLOCK
        s = jnp.einsum('bhqd,bhkd->bhqk', q[:, :, qs:ke], k[:, :, :ke]).astype(jnp.float32) * scale
        s = s + c[:, :, qs:ke, None] - c[:, :, None, :ke]
        mask = jnp.arange(ke)[None, :] <= (qs + jnp.arange(Q_BLOCK))[:, None]
        s = jnp.where(mask, s, -jnp.inf)
        p = jax.nn.softmax(s, axis=-1).astype(v.dtype)
        outs.append(jnp.einsum('bhqk,bhkd->bhqd', p, v[:, :, :ke]))
    return jnp.concatenate(outs, axis=2)


def _cdiag_combine(left, right):
    a1r, a1i, b1r, b1i = left
    a2r, a2i, b2r, b2i = right
    ar = a2r * a1r - a2i * a1i
    ai = a2r * a1i + a2i * a1r
    br = a2r * b1r - a2i * b1i + b2r
    bi = a2r * b1i + a2i * b1r + b2i
    return (ar, ai, br, bi)


def _s5(u, a_re, a_im, log_dt, b_re, b_im, c_re, c_im, d):
    Bsz, L, _ = u.shape
    uf = u.astype(jnp.float32).reshape(Bsz, L, S5_GROUPS, S5_GROUP_CH)
    ar = a_re.astype(jnp.float32)
    ai = a_im.astype(jnp.float32)
    dt = jnp.exp(log_dt.astype(jnp.float32))[:, None]
    mag = jnp.exp(ar * dt)
    lb_r = mag * jnp.cos(ai * dt)
    lb_i = mag * jnp.sin(ai * dt)
    den = ar * ar + ai * ai
    nr = lb_r - 1.0
    coef_r = (nr * ar + lb_i * ai) / den
    coef_i = (lb_i * ar - nr * ai) / den
    br = b_re.astype(jnp.float32)
    bi = b_im.astype(jnp.float32)
    bb_r = coef_r[:, :, None] * br - coef_i[:, :, None] * bi
    bb_i = coef_r[:, :, None] * bi + coef_i[:, :, None] * br
    bu_r = jnp.einsum('blgc,gpc->lbgp', uf, bb_r)
    bu_i = jnp.einsum('blgc,gpc->lbgp', uf, bb_i)
    a_r = jnp.broadcast_to(lb_r[None, None], (L, 1, S5_GROUPS, S5_STATE))
    a_i = jnp.broadcast_to(lb_i[None, None], (L, 1, S5_GROUPS, S5_STATE))
    _, _, xr, xi = lax.associative_scan(_cdiag_combine, (a_r, a_i, bu_r, bu_i), axis=0)
    y = (jnp.einsum('lbgp,gcp->blgc', xr, c_re.astype(jnp.float32))
         - jnp.einsum('lbgp,gcp->blgc', xi, c_im.astype(jnp.float32))
         + d.astype(jnp.float32) * uf)
    return y.reshape(Bsz, L, S5_WIDTH)


def _causal_dwconv(a, w, b):
    L = a.shape[1]
    ap = jnp.pad(a, ((0, 0), (CONV_W - 1, 0), (0, 0)))
    out = b
    for i in range(CONV_W):
        out = out + w[i] * ap[:, i:i + L]
    return out


def setup_inputs(seed: int = 0) -> dict:
    key = jax.random.key(seed)
    ks = jax.random.split(key, 40)
    f32 = jnp.float32

    def nrm(k, shape, scale):
        return jax.random.normal(k, shape, f32) * scale

    def gain(k, shape):
        return 1.0 + 0.02 * jax.random.normal(k, shape, f32)

    Ld = DEPTH
    n_idx = jnp.arange(S5_STATE, dtype=f32)
    inp = {
        "x": jax.random.normal(ks[0], (BATCH, SEQ, D_MODEL), f32),
        "mem": jax.random.normal(ks[1], (BATCH, N_MEM, D_MODEL), f32),
        "norm_mix": gain(ks[2], (Ld, D_MODEL)),
        "w_in": nrm(ks[3], (Ld, D_MODEL, IN_COLS), D_MODEL ** -0.5),
        "fox_q_norm": gain(ks[4], (Ld, HEAD_DIM)),
        "fox_k_norm": gain(ks[5], (Ld, HEAD_DIM)),
        "fox_f_bias": 3.0 + 0.5 * jax.random.normal(ks[6], (Ld, N_FOX_HEADS), f32),
        "s5_a_re": -0.5 + 0.01 * jax.random.normal(ks[7], (Ld, S5_GROUPS, S5_STATE), f32),
        "s5_a_im": math.pi * n_idx[None, None, :] + 0.01 * jax.random.normal(ks[8], (Ld, S5_GROUPS, S5_STATE), f32),
        "s5_log_dt": jax.random.uniform(ks[9], (Ld, S5_GROUPS), f32, math.log(1e-3), math.log(1e-1)),
        "s5_b_re": nrm(ks[10], (Ld, S5_GROUPS, S5_STATE, S5_GROUP_CH), (2 * S5_GROUP_CH) ** -0.5),
        "s5_b_im": nrm(ks[11], (Ld, S5_GROUPS, S5_STATE, S5_GROUP_CH), (2 * S5_GROUP_CH) ** -0.5),
        "s5_c_re": nrm(ks[12], (Ld, S5_GROUPS, S5_GROUP_CH, S5_STATE), (2 * S5_STATE) ** -0.5),
        "s5_c_im": nrm(ks[13], (Ld, S5_GROUPS, S5_GROUP_CH, S5_STATE), (2 * S5_STATE) ** -0.5),
        "s5_d": nrm(ks[14], (Ld, S5_GROUPS, S5_GROUP_CH), 1.0),
        "s5_w_glu": nrm(ks[15], (Ld, S5_WIDTH, S5_WIDTH), S5_WIDTH ** -0.5),
        "s5_b_glu": nrm(ks[16], (Ld, S5_WIDTH), 0.02),
        "out_norm_fox": gain(ks[17], (Ld, FOX_WIDTH)),
        "out_norm_s5": gain(ks[18], (Ld, S5_WIDTH)),
        "w_out": nrm(ks[19], (Ld, D_MODEL, D_MODEL), D_MODEL ** -0.5),
        "norm_cross": gain(ks[20], (Ld, D_MODEL)),
        "norm_mem": gain(ks[21], (Ld, D_MODEL)),
        "w_xq": nrm(ks[22], (Ld, D_MODEL, D_MODEL), D_MODEL ** -0.5),
        "w_xkv": nrm(ks[23], (Ld, D_MODEL, 2 * D_MODEL), D_MODEL ** -0.5),
        "xq_norm": gain(ks[24], (Ld, X_HEAD_DIM)),
        "xk_norm": gain(ks[25], (Ld, X_HEAD_DIM)),
        "w_xo": nrm(ks[26], (Ld, D_MODEL, D_MODEL), D_MODEL ** -0.5),
        "norm_ffn": gain(ks[27], (Ld, D_MODEL)),
        "w_ffn_up": nrm(ks[28], (Ld, D_MODEL, 2 * D_FF), D_MODEL ** -0.5),
        "ffn_conv_w": nrm(ks[29], (Ld, CONV_W, D_FF), CONV_W ** -0.5),
        "ffn_conv_b": nrm(ks[30], (Ld, D_FF), 0.02),
        "w_ffn_down": nrm(ks[31], (Ld, D_FF, D_MODEL), D_FF ** -0.5),
    }
    return inp


def reference(x, mem, norm_mix, w_in, fox_q_norm, fox_k_norm, fox_f_bias,
              s5_a_re, s5_a_im, s5_log_dt, s5_b_re, s5_b_im, s5_c_re, s5_c_im,
              s5_d, s5_w_glu, s5_b_glu, out_norm_fox, out_norm_s5, w_out,
              norm_cross, norm_mem, w_xq, w_xkv, xq_norm, xk_norm, w_xo,
              norm_ffn, w_ffn_up, ffn_conv_w, ffn_conv_b, w_ffn_down):
    Bsz, L, _ = x.shape
    h = x
    for l in range(DEPTH):
        hn = _rms_norm(h, norm_mix[l])
        proj = hn @ w_in[l]
        q, k, v, f_logit, u = jnp.split(
            proj, [FOX_WIDTH, 2 * FOX_WIDTH, 3 * FOX_WIDTH, 3 * FOX_WIDTH + N_FOX_HEADS], axis=-1)
        q = _rms_norm(q.reshape(Bsz, L, N_FOX_HEADS, HEAD_DIM), fox_q_norm[l]).transpose(0, 2, 1, 3)
        k = _rms_norm(k.reshape(Bsz, L, N_FOX_HEADS, HEAD_DIM), fox_k_norm[l]).transpose(0, 2, 1, 3)
        v = v.reshape(Bsz, L, N_FOX_HEADS, HEAD_DIM).transpose(0, 2, 1, 3)
        log_f = jax.nn.log_sigmoid(f_logit.astype(jnp.float32) + fox_f_bias[l].astype(jnp.float32))
        fox = _fox_attention(q, k, v, log_f.transpose(0, 2, 1))
        fox = fox.transpose(0, 2, 1, 3).reshape(Bsz, L, FOX_WIDTH)

        y = _s5(u, s5_a_re[l], s5_a_im[l], s5_log_dt[l], s5_b_re[l], s5_b_im[l],
                s5_c_re[l], s5_c_im[l], s5_d[l])
        y = jax.nn.gelu(y)
        y = y * jax.nn.sigmoid(y @ s5_w_glu[l].astype(jnp.float32) + s5_b_glu[l].astype(jnp.float32))
        y = y.astype(h.dtype)

        mixed = jnp.concatenate([_rms_norm(fox, out_norm_fox[l]), _rms_norm(y, out_norm_s5[l])], axis=-1)
        h = h + mixed @ w_out[l]

        hn = _rms_norm(h, norm_cross[l])
        mn = _rms_norm(mem, norm_mem[l])
        xq = _rms_norm((hn @ w_xq[l]).reshape(Bsz, L, N_X_HEADS, X_HEAD_DIM), xq_norm[l])
        xk, xv = jnp.split(mn @ w_xkv[l], 2, axis=-1)
        xk = _rms_norm(xk.reshape(Bsz, N_MEM, N_X_HEADS, X_HEAD_DIM), xk_norm[l])
        xv = xv.reshape(Bsz, N_MEM, N_X_HEADS, X_HEAD_DIM)
        s = jnp.einsum('bqhd,bmhd->bhqm', xq, xk).astype(jnp.float32) * (X_HEAD_DIM ** -0.5)
        p = jax.nn.softmax(s, axis=-1).astype(xv.dtype)
        xo = jnp.einsum('bhqm,bmhd->bqhd', p, xv).reshape(Bsz, L, D_MODEL)
        h = h + xo @ w_xo[l]

        hn = _rms_norm(h, norm_ffn[l])
        gate, up = jnp.split(hn @ w_ffn_up[l], 2, axis=-1)
        gate = _causal_dwconv(gate, ffn_conv_w[l], ffn_conv_b[l])
        h = h + (jax.nn.silu(gate) * up) @ w_ffn_down[l]
    return h
```

```python
import functools
import math

import jax
import jax.numpy as jnp
from jax import lax
from jax.experimental import pallas as pl
from jax.experimental.pallas import tpu as pltpu

F32 = jnp.float32
BF16 = jnp.bfloat16

D_MODEL = 1024
N_MEM = 256
FOX_WIDTH = 512
HEAD_DIM = 64
N_FOX_HEADS = 8
S5_WIDTH = 512
S5_GROUP_CH = 16
S5_GROUPS = 32
S5_STATE = 64
N_X_HEADS = 4
X_HEAD_DIM = 256
D_FF = 2816
CONV_W = 3
EPS = 1e-6

LANES = 128
CHUNK = 16
CHUNK_W = CHUNK * S5_GROUP_CH
TOK_TILE = 512
ATT_BLOCK = 256
FF_CHUNK = 256
NEG_BIG = -1e30
HI = lax.Precision.HIGHEST
VMEM_LIMIT = 56 * 1024 * 1024

NT_DIMS = (((1,), (1,)), ((), ()))


def _cparams(n_axes):
    return pltpu.CompilerParams(dimension_semantics=("arbitrary",) * n_axes,
                                vmem_limit_bytes=VMEM_LIMIT)


def _rms(xf, g):
    return xf * lax.rsqrt(jnp.mean(xf * xf, axis=-1, keepdims=True) + EPS) * g


def _dot(a, b):
    return jnp.dot(a, b, preferred_element_type=F32)


def _dot_nt(a, b):
    return lax.dot_general(a, b, NT_DIMS, preferred_element_type=F32)


def _head64_norm(x):
    lo = lax.broadcasted_iota(jnp.int32, (x.shape[0], LANES), 1) < HEAD_DIM
    outs = []
    for j in range(x.shape[1] // LANES):
        blk = x[:, j * LANES:(j + 1) * LANES]
        sq = blk * blk
        s_lo = jnp.sum(jnp.where(lo, sq, 0.0), axis=-1, keepdims=True)
        s_hi = jnp.sum(jnp.where(lo, 0.0, sq), axis=-1, keepdims=True)
        inv = jnp.where(lo, lax.rsqrt(s_lo * (1.0 / HEAD_DIM) + EPS),
                        lax.rsqrt(s_hi * (1.0 / HEAD_DIM) + EPS))
        outs.append(blk * inv)
    return jnp.concatenate(outs, axis=-1)


def _head256_norm(x):
    outs = []
    for h in range(x.shape[1] // X_HEAD_DIM):
        blk = x[:, h * X_HEAD_DIM:(h + 1) * X_HEAD_DIM]
        outs.append(blk * lax.rsqrt(jnp.mean(blk * blk, axis=-1, keepdims=True) + EPS))
    return jnp.concatenate(outs, axis=-1)


def _log_sigmoid(z):
    return jnp.minimum(z, 0.0) - jnp.log(1.0 + jnp.exp(-jnp.abs(z)))


def _gelu_tanh(x):
    return 0.5 * x * (1.0 + jnp.tanh(math.sqrt(2.0 / math.pi) * (x + 0.044715 * (x * x * x))))


def _s5_params_kernel(ldt_ref, arc_ref, aic_ref, arr_ref, air_ref, bre_ref, bim_ref,
                      cre2_ref, cim2_ref, d_ref,
                      t_ref, r_ref, o_ref, a16r_ref, a16i_ref):
    dt = jnp.exp(ldt_ref[0])
    ar, ai = arc_ref[0], aic_ref[0]
    mag = jnp.exp(ar * dt)
    lr, li = mag * jnp.cos(ai * dt), mag * jnp.sin(ai * dt)
    den = ar * ar + ai * ai
    nr = lr - 1.0
    coef_r = (nr * ar + li * ai) / den
    coef_i = (li * ar - nr * ai) / den
    bb_r = coef_r * bre_ref[0] - coef_i * bim_ref[0]
    bb_i = coef_r * bim_ref[0] + coef_i * bre_ref[0]
    arr, air = arr_ref[0], air_ref[0]
    magr = jnp.exp(arr * dt)
    lrr, lir = magr * jnp.cos(air * dt), magr * jnp.sin(air * dt)

    lane_t = lax.broadcasted_iota(jnp.int32, (S5_STATE, CHUNK_W), 1) // S5_GROUP_CH
    lane16 = lax.broadcasted_iota(jnp.int32, (S5_GROUP_CH, CHUNK_W), 1)
    row16 = lax.broadcasted_iota(jnp.int32, (S5_GROUP_CH, CHUNK_W), 0)
    lo128 = lax.broadcasted_iota(jnp.int32, (S5_GROUP_CH, 2 * S5_STATE), 1) < S5_STATE
    c_cat = jnp.where(lo128, cre2_ref[0], -cim2_ref[0])
    d_col = d_ref[0]

    m_r = jnp.zeros((S5_STATE, CHUNK_W), F32)
    m_i = jnp.zeros((S5_STATE, CHUNK_W), F32)
    w_r, w_i = cre2_ref[0], cim2_ref[0]
    for tq in range(CHUNK):
        hit = lane_t == tq
        m_r, m_i = (lr * m_r - li * m_i + jnp.where(hit, bb_r, 0.0),
                    lr * m_i + li * m_r + jnp.where(hit, bb_i, 0.0))
        m_cat = jnp.concatenate([m_r, m_i], axis=0)
        t_rows = jnp.dot(c_cat, m_cat, preferred_element_type=F32, precision=HI)
        t_rows = t_rows + jnp.where(lane16 == tq * S5_GROUP_CH + row16, d_col, 0.0)
        t_ref[0, tq * S5_GROUP_CH:(tq + 1) * S5_GROUP_CH, :] = t_rows.astype(BF16)
        w_r, w_i = w_r * lrr - w_i * lir, w_r * lir + w_i * lrr
        o_ref[0, tq * S5_GROUP_CH:(tq + 1) * S5_GROUP_CH, :] = jnp.where(lo128, w_r, -w_i).astype(BF16)
    r_ref[0, 0:S5_STATE, :] = m_r.astype(BF16)
    r_ref[0, S5_STATE:2 * S5_STATE, :] = m_i.astype(BF16)
    pr, pi = lr, li
    for _ in range(4):
        pr, pi = pr * pr - pi * pi, 2.0 * pr * pi
    a16r_ref[0] = jnp.broadcast_to(pr, (S5_STATE, LANES))
    a16i_ref[0] = jnp.broadcast_to(pi, (S5_STATE, LANES))


def _s5_params(s5_log_dt, s5_a_re, s5_a_im, s5_b_re, s5_b_im, s5_c_re, s5_c_im, s5_d):
    g, p, c = S5_GROUPS, S5_STATE, S5_GROUP_CH
    args = (
        s5_log_dt.reshape(g, 1, 1),
        s5_a_re.reshape(g, p, 1), s5_a_im.reshape(g, p, 1),
        jnp.tile(s5_a_re.reshape(g, 1, p), (1, 1, 2)), jnp.tile(s5_a_im.reshape(g, 1, p), (1, 1, 2)),
        jnp.tile(s5_b_re, (1, 1, CHUNK)), jnp.tile(s5_b_im, (1, 1, CHUNK)),
        jnp.tile(s5_c_re, (1, 1, 2)), jnp.tile(s5_c_im, (1, 1, 2)),
        s5_d.reshape(g, c, 1),
    )
    in_specs = [pl.BlockSpec((1,) + a.shape[1:], lambda i: (i, 0, 0)) for a in args]
    out_shape = (
        jax.ShapeDtypeStruct((g, CHUNK_W, CHUNK_W), BF16),
        jax.ShapeDtypeStruct((g, 2 * p, CHUNK_W), BF16),
        jax.ShapeDtypeStruct((g, CHUNK_W, 2 * p), BF16),
        jax.ShapeDtypeStruct((g, p, LANES), F32),
        jax.ShapeDtypeStruct((g, p, LANES), F32),
    )
    out_specs = tuple(pl.BlockSpec((1,) + s.shape[1:], lambda i: (i, 0, 0)) for s in out_shape)
    return pl.pallas_call(
        _s5_params_kernel, grid=(g,), in_specs=in_specs, out_specs=out_specs, out_shape=out_shape,
        compiler_params=_cparams(1), name="s5_params")(*args)


def _in_proj_kernel(x_ref, g_ref, wqkv_ref, wft_ref, qg_ref, kg_ref, q_ref, k_ref, v_ref, frow_ref):
    hn = _rms(x_ref[0], g_ref[...]).astype(BF16)
    qkv = _dot(hn, wqkv_ref[...])
    q = _head64_norm(qkv[:, :FOX_WIDTH]) * (qg_ref[...] * HEAD_DIM ** -0.5)
    k = _head64_norm(qkv[:, FOX_WIDTH:2 * FOX_WIDTH]) * kg_ref[...]
    q_ref[0] = q.astype(BF16)
    k_ref[0] = k.astype(BF16)
    v_ref[0] = qkv[:, 2 * FOX_WIDTH:].astype(BF16)
    frow_ref[0] = _dot_nt(wft_ref[...], hn)


def _in_proj(x, norm_mix, w_qkv, w_ft, q_gain, k_gain):
    b, l, d = x.shape
    tm = TOK_TILE
    tok = lambda w: pl.BlockSpec((1, tm, w), lambda i, j: (i, j, 0))
    full = lambda a: pl.BlockSpec(a.shape, lambda i, j: (0,) * a.ndim)
    return pl.pallas_call(
        _in_proj_kernel, grid=(b, l // tm),
        in_specs=[tok(d), full(norm_mix), full(w_qkv), full(w_ft), full(q_gain), full(k_gain)],
        out_specs=(tok(FOX_WIDTH), tok(FOX_WIDTH), tok(FOX_WIDTH),
                   pl.BlockSpec((1, 16, tm), lambda i, j: (i, 0, j))),
        out_shape=(jax.ShapeDtypeStruct((b, l, FOX_WIDTH), BF16),) * 3
        + (jax.ShapeDtypeStruct((b, 16, l), F32),),
        compiler_params=_cparams(2), name="in_proj")(x, norm_mix, w_qkv, w_ft, q_gain, k_gain)


def _gate_kernel(frow_ref, bias_ref, crow_ref):
    n_blk = frow_ref.shape[2] // LANES
    upper = (lax.broadcasted_iota(jnp.int32, (LANES, LANES), 0)
             <= lax.broadcasted_iota(jnp.int32, (LANES, LANES), 1)).astype(F32)
    carry = jnp.zeros((16, 1), F32)
    for blk in range(n_blk):
        z = frow_ref[0, :, blk * LANES:(blk + 1) * LANES] + bias_ref[...]
        c = jnp.dot(_log_sigmoid(z), upper, preferred_element_type=F32, precision=HI) + carry
        crow_ref[0, :, blk * LANES:(blk + 1) * LANES] = c
        carry = c[:, LANES - 1:LANES]


def _gate_cumsum(frow, bias_col):
    b, _, l = frow.shape
    return pl.pallas_call(
        _gate_kernel, grid=(b,),
        in_specs=[pl.BlockSpec((1, 16, l), lambda i: (i, 0, 0)), pl.BlockSpec((16, 1), lambda i: (0, 0))],
        out_specs=pl.BlockSpec((1, 16, l), lambda i: (i, 0, 0)),
        out_shape=jax.ShapeDtypeStruct((b, 16, l), F32),
        compiler_params=_cparams(1), name="gate_cumsum")(frow, bias_col)


def _u_proj_kernel(x_ref, g_ref, wut_ref, xt_ref):
    for t in range(CHUNK):
        xt = x_ref[0, :, t * D_MODEL:(t + 1) * D_MODEL]
        hn = _rms(xt, g_ref[...]).astype(BF16)
        zt = _dot_nt(wut_ref[...], hn)
        xt_ref[:, t * S5_GROUP_CH:(t + 1) * S5_GROUP_CH, :] = (
            zt.reshape(S5_GROUPS, S5_GROUP_CH, zt.shape[1]).astype(BF16))


def _u_proj(x, norm_mix, w_ut):
    b, l, d = x.shape
    nck = l // CHUNK
    x3 = x.reshape(b, nck, CHUNK * d)
    return pl.pallas_call(
        _u_proj_kernel, grid=(b,),
        in_specs=[pl.BlockSpec((1, nck, CHUNK * d), lambda i: (i, 0, 0)),
                  pl.BlockSpec(norm_mix.shape, lambda i: (0, 0)),
                  pl.BlockSpec(w_ut.shape, lambda i: (0, 0))],
        out_specs=pl.BlockSpec((S5_GROUPS, CHUNK_W, nck), lambda i: (0, 0, i)),
        out_shape=jax.ShapeDtypeStruct((S5_GROUPS, CHUNK_W, b * nck), BF16),
        compiler_params=_cparams(1), name="u_proj")(x3, norm_mix, w_ut)


def _fox_kernel(q_ref, k_ref, v_ref, c_ref, o_ref):
    blk = ATT_BLOCK
    n_blk = q_ref.shape[1] // blk
    lo_q = lax.broadcasted_iota(jnp.int32, (blk, LANES), 1) < HEAD_DIM
    row = lax.broadcasted_iota(jnp.int32, (blk, blk), 0)
    col = lax.broadcasted_iota(jnp.int32, (blk, blk), 1)
    zero = jnp.zeros((), BF16)

    def q_block(i, _):
        q2 = q_ref[0, pl.ds(pl.multiple_of(i * blk, blk), blk), :]
        q_lo = jnp.where(lo_q, q2, zero)
        q_hi = jnp.where(lo_q, zero, q2)

        def kv_step(j, carry, diagonal):
            m0, l0, m1, l1, acc = carry
            start = pl.multiple_of(j * blk, blk)
            kt = k_ref[0, pl.ds(start, blk), :]
            vt = v_ref[0, pl.ds(start, blk), :]
            s0 = _dot_nt(q_lo, kt) - c_ref[0, 0, pl.ds(j, 1), :]
            s1 = _dot_nt(q_hi, kt) - c_ref[0, 1, pl.ds(j, 1), :]
            if diagonal:
                s0 = jnp.where(col > row, NEG_BIG, s0)
                s1 = jnp.where(col > row, NEG_BIG, s1)
            m0n = jnp.maximum(m0, jnp.max(s0, axis=-1, keepdims=True))
            m1n = jnp.maximum(m1, jnp.max(s1, axis=-1, keepdims=True))
            p0 = jnp.exp(s0 - m0n)
            p1 = jnp.exp(s1 - m1n)
            a0 = jnp.exp(m0 - m0n)
            a1 = jnp.exp(m1 - m1n)
            l0 = a0 * l0 + jnp.sum(p0, axis=-1, keepdims=True)
            l1 = a1 * l1 + jnp.sum(p1, axis=-1, keepdims=True)
            acc = (acc * jnp.where(lo_q, a0, a1)
                   + _dot(p0.astype(BF16), jnp.where(lo_q, vt, zero))
                   + _dot(p1.astype(BF16), jnp.where(lo_q, zero, vt)))
            return m0n, l0, m1n, l1, acc

        init = (jnp.full((blk, 1), NEG_BIG, F32), jnp.zeros((blk, 1), F32),
                jnp.full((blk, 1), NEG_BIG, F32), jnp.zeros((blk, 1), F32),
                jnp.zeros((blk, LANES), F32))
        carry = lax.fori_loop(0, i, functools.partial(kv_step, diagonal=False), init)
        _, l0, _, l1, acc = kv_step(i, carry, diagonal=True)
        out = acc / jnp.where(lo_q, l0, l1)
        o_ref[0, pl.ds(pl.multiple_of(i * blk, blk), blk), :] = out.astype(BF16)
        return 0

    lax.fori_loop(0, n_blk, q_block, 0)


def _fox_attn(q, k, v, crow4):
    b, l, _ = q.shape
    n_pairs = FOX_WIDTH // LANES
    tok = pl.BlockSpec((1, l, LANES), lambda i, j: (i, 0, j))
    return pl.pallas_call(
        _fox_kernel, grid=(b, n_pairs),
        in_specs=[tok, tok, tok,
                  pl.BlockSpec((1, 2, l // ATT_BLOCK, ATT_BLOCK), lambda i, j: (i, j, 0, 0))],
        out_specs=tok,
        out_shape=jax.ShapeDtypeStruct((b, l, FOX_WIDTH), BF16),
        compiler_params=_cparams(2), name="fox_attn")(q, k, v, crow4)


def _s5_chunk_kernel(xt_ref, t_ref, r_ref, o_ref, a16r_ref, a16i_ref, yg_ref, *, chunks_per_seq):
    lane_tile = min(4 * LANES, xt_ref.shape[2])
    n_tiles = xt_ref.shape[2] // lane_tile
    n_levels = chunks_per_seq.bit_length() - 1
    kpos = lax.broadcasted_iota(jnp.int32, (S5_STATE, lane_tile), 1) % chunks_per_seq
    reps = lane_tile // LANES
    for n in range(n_tiles):
        x = xt_ref[0, :, n * lane_tile:(n + 1) * lane_tile]
        r = _dot(r_ref[0], x)
        s_r, s_i = r[:S5_STATE], r[S5_STATE:]
        a_r = jnp.concatenate([a16r_ref[0]] * reps, axis=1)
        a_i = jnp.concatenate([a16i_ref[0]] * reps, axis=1)
        for lvl in range(n_levels):
            sh = 1 << lvl
            keep = kpos >= sh
            p_r = pltpu.roll(s_r, sh, axis=1)
            p_i = pltpu.roll(s_i, sh, axis=1)
            s_r, s_i = (s_r + jnp.where(keep, a_r * p_r - a_i * p_i, 0.0),
                        s_i + jnp.where(keep, a_r * p_i + a_i * p_r, 0.0))
            a_r, a_i = a_r * a_r - a_i * a_i, 2.0 * a_r * a_i
        e_r = jnp.where(kpos >= 1, pltpu.roll(s_r, 1, axis=1), 0.0)
        e_i = jnp.where(kpos >= 1, pltpu.roll(s_i, 1, axis=1), 0.0)
        s_in = jnp.concatenate([e_r, e_i], axis=0).astype(BF16)
        y = _dot(t_ref[0], x) + _dot(o_ref[0], s_in)
        yg = _gelu_tanh(y).astype(BF16)
        yg_ref[:, :, n * lane_tile:(n + 1) * lane_tile] = yg.reshape(CHUNK, S5_GROUP_CH, lane_tile)


def _s5_chunk(xt, t_op, r_op, o_op, a16r, a16i, chunks_per_seq):
    g, _, nb = xt.shape
    per_g = lambda a: pl.BlockSpec((1,) + a.shape[1:], lambda i: (i, 0, 0))
    return pl.pallas_call(
        functools.partial(_s5_chunk_kernel, chunks_per_seq=chunks_per_seq), grid=(g,),
        in_specs=[per_g(xt), per_g(t_op), per_g(r_op), per_g(o_op), per_g(a16r), per_g(a16i)],
        out_specs=pl.BlockSpec((CHUNK, S5_GROUP_CH, nb), lambda i: (0, i, 0)),
        out_shape=jax.ShapeDtypeStruct((CHUNK, S5_WIDTH, nb), BF16),
        compiler_params=_cparams(1), name="s5_chunk")(xt, t_op, r_op, o_op, a16r, a16i)


def _s5_post_kernel(yg_ref, wgt_ref, bg_ref, gn_ref, o_ref):
    for t in range(CHUNK):
        y = yg_ref[t]
        gate = _dot(wgt_ref[...], y) + bg_ref[...]
        z = y.astype(F32) * jax.nn.sigmoid(gate)
        zn = z * lax.rsqrt(jnp.mean(z * z, axis=0, keepdims=True) + EPS) * gn_ref[...]
        o_ref[0, :, t * S5_WIDTH:(t + 1) * S5_WIDTH] = zn.T.astype(BF16)


def _s5_post(yg, w_glu_t, b_glu_col, gain_col, batch):
    _, _, nb = yg.shape
    nck = nb // batch
    full = lambda a: pl.BlockSpec(a.shape, lambda i: (0,) * a.ndim)
    return pl.pallas_call(
        _s5_post_kernel, grid=(batch,),
        in_specs=[pl.BlockSpec((CHUNK, S5_WIDTH, nck), lambda i: (0, 0, i)),
                  full(w_glu_t), full(b_glu_col), full(gain_col)],
        out_specs=pl.BlockSpec((1, nck, CHUNK * S5_WIDTH), lambda i: (i, 0, 0)),
        out_shape=jax.ShapeDtypeStruct((batch, nck, CHUNK * S5_WIDTH), BF16),
        compiler_params=_cparams(1), name="s5_post")(yg, w_glu_t, b_glu_col, gain_col)


def _mix_out_kernel(x_ref, fox_ref, s5_ref, gf_ref, wo_ref, gc_ref, wxq_ref, gq_ref, h_ref, xq_ref):
    fox_n = _rms(fox_ref[0].astype(F32), gf_ref[...]).astype(BF16)
    mixed = jnp.concatenate([fox_n, s5_ref[0]], axis=-1)
    h = x_ref[0] + _dot(mixed, wo_ref[...])
    h_ref[0] = h
    hn = _rms(h, gc_ref[...]).astype(BF16)
    xq = _head256_norm(_dot(hn, wxq_ref[...])) * (gq_ref[...] * X_HEAD_DIM ** -0.5)
    xq_ref[0] = xq.astype(BF16)


def _mix_out(x, fox, s5n, g_fox, w_out, g_cross, w_xq, g_xq):
    b, l, d = x.shape
    tm = TOK_TILE
    tok = lambda w: pl.BlockSpec((1, tm, w), lambda i, j: (i, j, 0))
    full = lambda a: pl.BlockSpec(a.shape, lambda i, j: (0,) * a.ndim)
    return pl.pallas_call(
        _mix_out_kernel, grid=(b, l // tm),
        in_specs=[tok(d), tok(FOX_WIDTH), tok(S5_WIDTH), full(g_fox), full(w_out), full(g_cross),
                  full(w_xq), full(g_xq)],
        out_specs=(tok(d), tok(d)),
        out_shape=(jax.ShapeDtypeStruct((b, l, d), F32), jax.ShapeDtypeStruct((b, l, d), BF16)),
        compiler_params=_cparams(2), name="mix_out")(x, fox, s5n, g_fox, w_out, g_cross, w_xq, g_xq)


def _mem_kv_kernel(mem_ref, g_ref, wkv_ref, gk_ref, xk_ref, xv_ref):
    mn = _rms(mem_ref[0], g_ref[...]).astype(BF16)
    kv = _dot(mn, wkv_ref[...])
    xk_ref[0] = (_head256_norm(kv[:, :D_MODEL]) * gk_ref[...]).astype(BF16)
    xv_ref[0] = kv[:, D_MODEL:].astype(BF16)


def _mem_kv(mem, g_mem, w_xkv, g_xk):
    b, n, d = mem.shape
    full = lambda a: pl.BlockSpec(a.shape, lambda i: (0,) * a.ndim)
    blk = pl.BlockSpec((1, n, d), lambda i: (i, 0, 0))
    return pl.pallas_call(
        _mem_kv_kernel, grid=(b,),
        in_specs=[blk, full(g_mem), full(w_xkv), full(g_xk)],
        out_specs=(blk, blk),
        out_shape=(jax.ShapeDtypeStruct((b, n, d), BF16),) * 2,
        compiler_params=_cparams(1), name="mem_kv")(mem, g_mem, w_xkv, g_xk)


def _cross_attn_kernel(h_ref, xq_ref, xk_ref, xv_ref, wxo_ref, o_ref):
    outs = []
    for hd in range(N_X_HEADS):
        sl = slice(hd * X_HEAD_DIM, (hd + 1) * X_HEAD_DIM)
        s = _dot_nt(xq_ref[0, :, sl], xk_ref[0, :, sl])
        p = jnp.exp(s - jnp.max(s, axis=-1, keepdims=True))
        p = p / jnp.sum(p, axis=-1, keepdims=True)
        outs.append(_dot(p.astype(BF16), xv_ref[0, :, sl]).astype(BF16))
    xo = jnp.concatenate(outs, axis=-1)
    o_ref[0] = h_ref[0] + _dot(xo, wxo_ref[...])


def _cross_attn(h, xq, xk, xv, w_xo):
    b, l, d = h.shape
    tm = TOK_TILE
    tok = pl.BlockSpec((1, tm, d), lambda i, j: (i, j, 0))
    mem = pl.BlockSpec((1, N_MEM, d), lambda i, j: (i, 0, 0))
    return pl.pallas_call(
        _cross_attn_kernel, grid=(b, l // tm),
        in_specs=[tok, tok, mem, mem, pl.BlockSpec(w_xo.shape, lambda i, j: (0, 0))],
        out_specs=tok,
        out_shape=jax.ShapeDtypeStruct((b, l, d), F32),
        compiler_params=_cparams(2), name="cross_attn")(h, xq, xk, xv, w_xo)


def _conv_ffn_kernel(h_ref, g_ref, wg_ref, wu_ref, cw_ref, cb_ref, wd_ref, o_ref, tail_ref):
    @pl.when(pl.program_id(1) == 0)
    def _():
        tail_ref[...] = jnp.zeros_like(tail_ref)

    h = h_ref[0]
    tm = h.shape[0]
    hn = _rms(h, g_ref[...]).astype(BF16)
    row = lax.broadcasted_iota(jnp.int32, (tm, FF_CHUNK), 0)
    acc = h
    for c in range(D_FF // FF_CHUNK):
        sl = slice(c * FF_CHUNK, (c + 1) * FF_CHUNK)
        gate = _dot(hn, wg_ref[:, sl])
        up = _dot(hn, wu_ref[:, sl])
        prev1 = tail_ref[1:2, sl]
        prev2 = tail_ref[0:1, sl]
        g1 = jnp.where(row == 0, prev1, pltpu.roll(gate, 1, axis=0))
        g2 = jnp.where(row == 0, prev2, jnp.where(row == 1, prev1, pltpu.roll(gate, 2, axis=0)))
        conv = cb_ref[:, sl] + cw_ref[0:1, sl] * g2 + cw_ref[1:2, sl] * g1 + cw_ref[2:3, sl] * gate
        tail_ref[0:2, sl] = gate[tm - 2:tm, :]
        act = (conv * jax.nn.sigmoid(conv) * up).astype(BF16)
        acc = acc + _dot(act, wd_ref[sl, :])
    o_ref[0] = acc


def _conv_ffn(h, g_ffn, w_gate, w_up, conv_w, conv_b, w_down):
    b, l, d = h.shape
    tm = TOK_TILE
    tok = pl.BlockSpec((1, tm, d), lambda i, j: (i, j, 0))
    full = lambda a: pl.BlockSpec(a.shape, lambda i, j: (0,) * a.ndim)
    return pl.pallas_call(
        _conv_ffn_kernel, grid=(b, l // tm),
        in_specs=[tok, full(g_ffn), full(w_gate), full(w_up), full(conv_w), full(conv_b), full(w_down)],
        out_specs=tok,
        out_shape=jax.ShapeDtypeStruct((b, l, d), F32),
        scratch_shapes=[pltpu.VMEM((8, D_FF), F32)],
        compiler_params=_cparams(2), name="conv_ffn")(h, g_ffn, w_gate, w_up, conv_w, conv_b, w_down)


def _layer(h, mem, p):
    b, l, d = h.shape
    nck = l // CHUNK
    row = lambda a: a.reshape(1, -1)
    col = lambda a: a.reshape(-1, 1)
    w_in = p["w_in"]
    qkv_end = 3 * FOX_WIDTH
    w_qkv = w_in[:, :qkv_end].astype(BF16)
    w_ft = jnp.pad(w_in[:, qkv_end:qkv_end + N_FOX_HEADS].T, ((0, 16 - N_FOX_HEADS), (0, 0))).astype(BF16)
    w_ut = w_in[:, qkv_end + N_FOX_HEADS:].T.astype(BF16)
    f_bias = jnp.pad(p["fox_f_bias"], (0, 16 - N_FOX_HEADS)).reshape(16, 1)

    t_op, r_op, o_op, a16r, a16i = _s5_params(
        p["s5_log_dt"], p["s5_a_re"], p["s5_a_im"], p["s5_b_re"], p["s5_b_im"],
        p["s5_c_re"], p["s5_c_im"], p["s5_d"])

    q, k, v, frow = _in_proj(h, row(p["norm_mix"]), w_qkv, w_ft,
                             row(jnp.tile(p["fox_q_norm"], N_FOX_HEADS)),
                             row(jnp.tile(p["fox_k_norm"], N_FOX_HEADS)))
    crow = _gate_cumsum(frow, f_bias)
    crow4 = crow.reshape(b, 16, l // ATT_BLOCK, ATT_BLOCK)
    fox = _fox_attn(q, k, v, crow4)

    xt = _u_proj(h, row(p["norm_mix"]), w_ut)
    yg = _s5_chunk(xt, t_op, r_op, o_op, a16r, a16i, nck)
    s5n = _s5_post(yg, p["s5_w_glu"].T.astype(BF16), col(p["s5_b_glu"]), col(p["out_norm_s5"]), b)
    s5n = s5n.reshape(b, l, S5_WIDTH)

    h1, xq = _mix_out(h, fox, s5n, row(p["out_norm_fox"]), p["w_out"].astype(BF16),
                      row(p["norm_cross"]), p["w_xq"].astype(BF16),
                      row(jnp.tile(p["xq_norm"], N_X_HEADS)))
    xk, xv = _mem_kv(mem, row(p["norm_mem"]), p["w_xkv"].astype(BF16),
                     row(jnp.tile(p["xk_norm"], N_X_HEADS)))
    h2 = _cross_attn(h1, xq, xk, xv, p["w_xo"].astype(BF16))

    w_up = p["w_ffn_up"]
    return _conv_ffn(h2, row(p["norm_ffn"]), w_up[:, :D_FF].astype(BF16), w_up[:, D_FF:].astype(BF16),
                     p["ffn_conv_w"], row(p["ffn_conv_b"]), p["w_ffn_down"].astype(BF16))


_PARAM_NAMES = (
    "norm_mix", "w_in", "fox_q_norm", "fox_k_norm", "fox_f_bias", "s5_a_re", "s5_a_im", "s5_log_dt",
    "s5_b_re", "s5_b_im", "s5_c_re", "s5_c_im", "s5_d", "s5_w_glu", "s5_b_glu", "out_norm_fox",
    "out_norm_s5", "w_out", "norm_cross", "norm_mem", "w_xq", "w_xkv", "xq_norm", "xk_norm", "w_xo",
    "norm_ffn", "w_ffn_up", "ffn_conv_w", "ffn_conv_b", "w_ffn_down")


def kernel(x, mem, norm_mix, w_in, fox_q_norm, fox_k_norm, fox_f_bias, s5_a_re, s5_a_im, s5_log_dt, s5_b_re, s5_b_im, s5_c_re, s5_c_im, s5_d, s5_w_glu, s5_b_glu, out_norm_fox, out_norm_s5, w_out, norm_cross, norm_mem, w_xq, w_xkv, xq_norm, xk_norm, w_xo, norm_ffn, w_ffn_up, ffn_conv_w, ffn_conv_b, w_ffn_down):
    stacked = (norm_mix, w_in, fox_q_norm, fox_k_norm, fox_f_bias, s5_a_re, s5_a_im, s5_log_dt,
               s5_b_re, s5_b_im, s5_c_re, s5_c_im, s5_d, s5_w_glu, s5_b_glu, out_norm_fox,
               out_norm_s5, w_out, norm_cross, norm_mem, w_xq, w_xkv, xq_norm, xk_norm, w_xo,
               norm_ffn, w_ffn_up, ffn_conv_w, ffn_conv_b, w_ffn_down)
    h = x
    for layer in range(norm_mix.shape[0]):
        h = _layer(h, mem, {n: a[layer] for n, a in zip(_PARAM_NAMES, stacked)})
    return h
```

```python
import functools
import math

import jax
import jax.numpy as jnp
import numpy as np
from jax import lax
from jax.experimental import pallas as pl
from jax.experimental.pallas import tpu as pltpu

F32 = jnp.float32
BF16 = jnp.bfloat16

D_MODEL = 1024
N_MEM = 256
FOX_WIDTH = 512
HEAD_DIM = 64
N_FOX_HEADS = 8
S5_WIDTH = 512
S5_GROUP_CH = 16
S5_GROUPS = 32
S5_STATE = 64
N_X_HEADS = 4
X_HEAD_DIM = 256
D_FF = 2816
CONV_W = 3
EPS = 1e-6

LANES = 128
CHUNK = 16
CHUNK_W = CHUNK * S5_GROUP_CH
TOK_TILE = 512
ATT_BLOCK = 256
FF_CHUNK = 256
NEG_BIG = -1e30
LOG2E = math.log2(math.e)
HI = lax.Precision.HIGHEST
VMEM_LIMIT = 56 * 1024 * 1024

NT_DIMS = (((1,), (1,)), ((), ()))


def _cparams(n_axes):
    return pltpu.CompilerParams(dimension_semantics=("arbitrary",) * n_axes,
                                vmem_limit_bytes=VMEM_LIMIT)


def _rms(xf, g):
    return xf * lax.rsqrt(jnp.mean(xf * xf, axis=-1, keepdims=True) + EPS) * g


def _dot(a, b):
    return jnp.dot(a, b, preferred_element_type=F32)


def _dot_nt(a, b):
    return lax.dot_general(a, b, NT_DIMS, preferred_element_type=F32)


def _head64_norm(x):
    lo = lax.broadcasted_iota(jnp.int32, (x.shape[0], LANES), 1) < HEAD_DIM
    outs = []
    for j in range(x.shape[1] // LANES):
        blk = x[:, j * LANES:(j + 1) * LANES]
        sq = blk * blk
        s_lo = jnp.sum(jnp.where(lo, sq, 0.0), axis=-1, keepdims=True)
        s_hi = jnp.sum(jnp.where(lo, 0.0, sq), axis=-1, keepdims=True)
        inv = jnp.where(lo, lax.rsqrt(s_lo * (1.0 / HEAD_DIM) + EPS),
                        lax.rsqrt(s_hi * (1.0 / HEAD_DIM) + EPS))
        outs.append(blk * inv)
    return jnp.concatenate(outs, axis=-1)


def _head256_norm(x):
    outs = []
    for h in range(x.shape[1] // X_HEAD_DIM):
        blk = x[:, h * X_HEAD_DIM:(h + 1) * X_HEAD_DIM]
        outs.append(blk * lax.rsqrt(jnp.mean(blk * blk, axis=-1, keepdims=True) + EPS))
    return jnp.concatenate(outs, axis=-1)


def _log_sigmoid(z):
    return jnp.minimum(z, 0.0) - jnp.log(1.0 + jnp.exp(-jnp.abs(z)))


def _gelu_tanh(x):
    return 0.5 * x * (1.0 + jnp.tanh(math.sqrt(2.0 / math.pi) * (x + 0.044715 * (x * x * x))))


def _s5_params_kernel(ldt_ref, arc_ref, aic_ref, arr_ref, air_ref, bre_ref, bim_ref,
                      cre2_ref, cim2_ref, d_ref,
                      t_ref, r_ref, o_ref, a16r_ref, a16i_ref):
    dt = jnp.exp(ldt_ref[0])
    ar, ai = arc_ref[0], aic_ref[0]
    mag = jnp.exp(ar * dt)
    lr, li = mag * jnp.cos(ai * dt), mag * jnp.sin(ai * dt)
    den = ar * ar + ai * ai
    nr = lr - 1.0
    coef_r = (nr * ar + li * ai) / den
    coef_i = (li * ar - nr * ai) / den
    bb_r = coef_r * bre_ref[0] - coef_i * bim_ref[0]
    bb_i = coef_r * bim_ref[0] + coef_i * bre_ref[0]
    arr, air = arr_ref[0], air_ref[0]
    magr = jnp.exp(arr * dt)
    lrr, lir = magr * jnp.cos(air * dt), magr * jnp.sin(air * dt)

    lane_t = lax.broadcasted_iota(jnp.int32, (S5_STATE, CHUNK_W), 1) // S5_GROUP_CH
    lane16 = lax.broadcasted_iota(jnp.int32, (S5_GROUP_CH, CHUNK_W), 1)
    row16 = lax.broadcasted_iota(jnp.int32, (S5_GROUP_CH, CHUNK_W), 0)
    lo128 = lax.broadcasted_iota(jnp.int32, (S5_GROUP_CH, 2 * S5_STATE), 1) < S5_STATE
    c_cat = jnp.where(lo128, cre2_ref[0], -cim2_ref[0])
    d_col = d_ref[0]

    m_r = jnp.zeros((S5_STATE, CHUNK_W), F32)
    m_i = jnp.zeros((S5_STATE, CHUNK_W), F32)
    w_r, w_i = cre2_ref[0], cim2_ref[0]
    for tq in range(CHUNK):
        hit = lane_t == tq
        m_r, m_i = (lr * m_r - li * m_i + jnp.where(hit, bb_r, 0.0),
                    lr * m_i + li * m_r + jnp.where(hit, bb_i, 0.0))
        m_cat = jnp.concatenate([m_r, m_i], axis=0)
        t_rows = jnp.dot(c_cat, m_cat, preferred_element_type=F32, precision=HI)
        t_rows = t_rows + jnp.where(lane16 == tq * S5_GROUP_CH + row16, d_col, 0.0)
        t_ref[0, tq * S5_GROUP_CH:(tq + 1) * S5_GROUP_CH, :] = t_rows.astype(BF16)
        w_r, w_i = w_r * lrr - w_i * lir, w_r * lir + w_i * lrr
        o_ref[0, tq * S5_GROUP_CH:(tq + 1) * S5_GROUP_CH, :] = jnp.where(lo128, w_r, -w_i).astype(BF16)
    r_ref[0, 0:S5_STATE, :] = m_r.astype(BF16)
    r_ref[0, S5_STATE:2 * S5_STATE, :] = m_i.astype(BF16)
    pr, pi = lr, li
    for _ in range(4):
        pr, pi = pr * pr - pi * pi, 2.0 * pr * pi
    a16r_ref[0] = jnp.broadcast_to(pr, (S5_STATE, LANES))
    a16i_ref[0] = jnp.broadcast_to(pi, (S5_STATE, LANES))


def _s5_params(s5_log_dt, s5_a_re, s5_a_im, s5_b_re, s5_b_im, s5_c_re, s5_c_im, s5_d):
    g, p, c = S5_GROUPS, S5_STATE, S5_GROUP_CH
    args = (
        s5_log_dt.reshape(g, 1, 1),
        s5_a_re.reshape(g, p, 1), s5_a_im.reshape(g, p, 1),
        jnp.tile(s5_a_re.reshape(g, 1, p), (1, 1, 2)), jnp.tile(s5_a_im.reshape(g, 1, p), (1, 1, 2)),
        jnp.tile(s5_b_re, (1, 1, CHUNK)), jnp.tile(s5_b_im, (1, 1, CHUNK)),
        jnp.tile(s5_c_re, (1, 1, 2)), jnp.tile(s5_c_im, (1, 1, 2)),
        s5_d.reshape(g, c, 1),
    )
    in_specs = [pl.BlockSpec((1,) + a.shape[1:], lambda i: (i, 0, 0)) for a in args]
    out_shape = (
        jax.ShapeDtypeStruct((g, CHUNK_W, CHUNK_W), BF16),
        jax.ShapeDtypeStruct((g, 2 * p, CHUNK_W), BF16),
        jax.ShapeDtypeStruct((g, CHUNK_W, 2 * p), BF16),
        jax.ShapeDtypeStruct((g, p, LANES), F32),
        jax.ShapeDtypeStruct((g, p, LANES), F32),
    )
    out_specs = tuple(pl.BlockSpec((1,) + s.shape[1:], lambda i: (i, 0, 0)) for s in out_shape)
    return pl.pallas_call(
        _s5_params_kernel, grid=(g,), in_specs=in_specs, out_specs=out_specs, out_shape=out_shape,
        compiler_params=_cparams(1), name="s5_params")(*args)


def _in_proj_kernel(x_ref, g_ref, wqk_ref, wvt_ref, wf_ref, qg_ref, kg_ref, q_ref, k_ref, vt_ref, f_ref):
    hn = _rms(x_ref[0], g_ref[...]).astype(BF16)
    qk = _dot(hn, wqk_ref[...])
    q = _head64_norm(qk[:, :FOX_WIDTH]) * (qg_ref[...] * (HEAD_DIM ** -0.5 * LOG2E))
    k = _head64_norm(qk[:, FOX_WIDTH:]) * kg_ref[...]
    q_ref[0] = q.astype(BF16)
    k_ref[0] = k.astype(BF16)
    vt_ref[0] = _dot_nt(wvt_ref[...], hn).astype(BF16)
    f_ref[0] = _dot(hn, wf_ref[...])


def _in_proj(x, norm_mix, w_qk, w_vt, w_f, q_gain, k_gain):
    b, l, d = x.shape
    tm = TOK_TILE
    tok = lambda w: pl.BlockSpec((1, tm, w), lambda i, j: (i, j, 0))
    full = lambda a: pl.BlockSpec(a.shape, lambda i, j: (0,) * a.ndim)
    return pl.pallas_call(
        _in_proj_kernel, grid=(b, l // tm),
        in_specs=[tok(d), full(norm_mix), full(w_qk), full(w_vt), full(w_f), full(q_gain), full(k_gain)],
        out_specs=(tok(FOX_WIDTH), tok(FOX_WIDTH),
                   pl.BlockSpec((1, FOX_WIDTH, tm), lambda i, j: (i, 0, j)),
                   tok(LANES)),
        out_shape=(jax.ShapeDtypeStruct((b, l, FOX_WIDTH), BF16),) * 2
        + (jax.ShapeDtypeStruct((b, FOX_WIDTH, l), BF16),
           jax.ShapeDtypeStruct((b, l, LANES), F32)),
        compiler_params=_cparams(2), name="in_proj")(x, norm_mix, w_qk, w_vt, w_f, q_gain, k_gain)


def _gate_kernel(f_ref, bias_ref, place_ref, cs_ref):
    n_blk = f_ref.shape[1] // LANES
    lower = (lax.broadcasted_iota(jnp.int32, (LANES, LANES), 0)
             >= lax.broadcasted_iota(jnp.int32, (LANES, LANES), 1)).astype(F32)
    term = lax.broadcasted_iota(jnp.int32, (LANES, FOX_WIDTH), 1) % HEAD_DIM
    carry = jnp.zeros((1, LANES), F32)
    for blk in range(n_blk):
        rows = slice(blk * LANES, (blk + 1) * LANES)
        log2f = _log_sigmoid(f_ref[0, rows, :] + bias_ref[...]) * LOG2E
        c = jnp.dot(lower, log2f, preferred_element_type=F32, precision=HI) + carry
        carry = c[LANES - 1:LANES, :]
        placed = jnp.dot(c, place_ref[...], preferred_element_type=F32, precision=HI)
        hi = placed.astype(BF16)
        rest = placed - hi.astype(F32)
        mid = rest.astype(BF16)
        lo = (rest - mid.astype(F32)).astype(BF16)
        cs_ref[0, rows, :] = jnp.where(term == 0, hi, jnp.where(term == 1, mid, lo))


def _gate_cumsum(fcol, bias_row, place):
    b, l, _ = fcol.shape
    return pl.pallas_call(
        _gate_kernel, grid=(b,),
        in_specs=[pl.BlockSpec((1, l, LANES), lambda i: (i, 0, 0)),
                  pl.BlockSpec(bias_row.shape, lambda i: (0, 0)),
                  pl.BlockSpec(place.shape, lambda i: (0, 0))],
        out_specs=pl.BlockSpec((1, l, FOX_WIDTH), lambda i: (i, 0, 0)),
        out_shape=jax.ShapeDtypeStruct((b, l, FOX_WIDTH), BF16),
        compiler_params=_cparams(1), name="gate_cumsum")(fcol, bias_row, place)


def _u_proj_kernel(x_ref, g_ref, wut_ref, xt_ref):
    for t in range(CHUNK):
        hn = _rms(x_ref[0, :, t, :], g_ref[...]).astype(BF16)
        zt = _dot_nt(wut_ref[...], hn)
        xt_ref[:, t * S5_GROUP_CH:(t + 1) * S5_GROUP_CH, :] = (
            zt.reshape(S5_GROUPS, S5_GROUP_CH, zt.shape[1]).astype(BF16))


def _u_proj(x, norm_mix, w_ut):
    b, l, d = x.shape
    nck = l // CHUNK
    x4 = x.reshape(b, nck, CHUNK, d)
    return pl.pallas_call(
        _u_proj_kernel, grid=(b,),
        in_specs=[pl.BlockSpec((1, nck, CHUNK, d), lambda i: (i, 0, 0, 0)),
                  pl.BlockSpec(norm_mix.shape, lambda i: (0, 0)),
                  pl.BlockSpec(w_ut.shape, lambda i: (0, 0))],
        out_specs=pl.BlockSpec((S5_GROUPS, CHUNK_W, nck), lambda i: (0, 0, i)),
        out_shape=jax.ShapeDtypeStruct((S5_GROUPS, CHUNK_W, b * nck), BF16),
        compiler_params=_cparams(1), name="u_proj")(x4, norm_mix, w_ut)


def _fox_kernel(q_ref, k_ref, cs_ref, vt_ref, o_ref, kaug_ref, s_ref):
    blk = ATT_BLOCK
    seq = q_ref.shape[1]
    lane_k = lax.broadcasted_iota(jnp.int32, (seq, LANES), 1)
    k2 = k_ref[0]
    cs = cs_ref[0]
    kaug_ref[0] = jnp.where(lane_k < HEAD_DIM, k2, cs)
    kaug_ref[1] = jnp.where(lane_k >= HEAD_DIM, k2, cs)

    lane_q = lax.broadcasted_iota(jnp.int32, (blk, LANES), 1)
    minus1 = [jnp.where((lane_q >= HEAD_DIM) & (lane_q < HEAD_DIM + 3), -1.0, 0.0).astype(BF16),
              jnp.where(lane_q < 3, -1.0, 0.0).astype(BF16)]
    own = [lane_q < HEAD_DIM, lane_q >= HEAD_DIM]
    key_pos = lax.broadcasted_iota(jnp.int32, (blk, blk), 0)
    qry_pos = lax.broadcasted_iota(jnp.int32, (blk, blk), 1)

    def scores(i):
        q2 = q_ref[0, i * blk:(i + 1) * blk, :]
        for hh in range(2):
            qa = jnp.where(own[hh], q2, minus1[hh])
            st = _dot_nt(kaug_ref[hh, 0:(i + 1) * blk, :], qa)
            if i > 0:
                s_ref[i % 2, hh, 0:i * blk, :] = st[:i * blk]
            s_ref[i % 2, hh, i * blk:(i + 1) * blk, :] = jnp.where(key_pos > qry_pos, NEG_BIG, st[i * blk:])

    def attend(i):
        keys = (i + 1) * blk
        outs = []
        for hh in range(2):
            st = s_ref[i % 2, hh, 0:keys, :]
            p = jnp.exp2(st - jnp.max(st, axis=0, keepdims=True))
            l = jnp.sum(p, axis=0, keepdims=True)
            acc = _dot(vt_ref[0, hh * HEAD_DIM:(hh + 1) * HEAD_DIM, 0:keys], p.astype(BF16))
            outs.append(acc / l)
        o_ref[0, i * blk:(i + 1) * blk, :] = jnp.concatenate(outs, axis=0).T.astype(BF16)

    n_blk = seq // blk
    scores(0)
    for i in range(n_blk):
        if i + 1 < n_blk:
            scores(i + 1)
        attend(i)


def _fox_attn(q, k, cs, vt):
    b, l, _ = q.shape
    n_pairs = FOX_WIDTH // LANES
    tok = pl.BlockSpec((1, l, LANES), lambda i, j: (i, 0, j))
    return pl.pallas_call(
        _fox_kernel, grid=(b, n_pairs),
        in_specs=[tok, tok, tok, pl.BlockSpec((1, LANES, l), lambda i, j: (i, j, 0))],
        out_specs=tok,
        out_shape=jax.ShapeDtypeStruct((b, l, FOX_WIDTH), BF16),
        scratch_shapes=[pltpu.VMEM((2, l, LANES), BF16), pltpu.VMEM((2, 2, l, ATT_BLOCK), F32)],
        compiler_params=_cparams(2), name="fox_attn")(q, k, cs, vt)


def _s5_chunk_kernel(xt_ref, t_ref, r_ref, o_ref, a16r_ref, a16i_ref, yg_ref, *, chunks_per_seq):
    lane_tile = min(4 * LANES, xt_ref.shape[2])
    n_tiles = xt_ref.shape[2] // lane_tile
    n_levels = chunks_per_seq.bit_length() - 1
    kpos = lax.broadcasted_iota(jnp.int32, (S5_STATE, lane_tile), 1) % chunks_per_seq
    reps = lane_tile // LANES
    for n in range(n_tiles):
        x = xt_ref[0, :, n * lane_tile:(n + 1) * lane_tile]
        r = _dot(r_ref[0], x)
        s_r, s_i = r[:S5_STATE], r[S5_STATE:]
        a_r = jnp.concatenate([a16r_ref[0]] * reps, axis=1)
        a_i = jnp.concatenate([a16i_ref[0]] * reps, axis=1)
        for lvl in range(n_levels):
            sh = 1 << lvl
            keep = kpos >= sh
            p_r = pltpu.roll(s_r, sh, axis=1)
            p_i = pltpu.roll(s_i, sh, axis=1)
            s_r, s_i = (s_r + jnp.where(keep, a_r * p_r - a_i * p_i, 0.0),
                        s_i + jnp.where(keep, a_r * p_i + a_i * p_r, 0.0))
            a_r, a_i = a_r * a_r - a_i * a_i, 2.0 * a_r * a_i
        e_r = jnp.where(kpos >= 1, pltpu.roll(s_r, 1, axis=1), 0.0)
        e_i = jnp.where(kpos >= 1, pltpu.roll(s_i, 1, axis=1), 0.0)
        s_in = jnp.concatenate([e_r, e_i], axis=0).astype(BF16)
        y = _dot(t_ref[0], x) + _dot(o_ref[0], s_in)
        yg = _gelu_tanh(y).astype(BF16)
        yg_ref[:, :, n * lane_tile:(n + 1) * lane_tile] = yg.reshape(CHUNK, S5_GROUP_CH, lane_tile)


def _s5_chunk(xt, t_op, r_op, o_op, a16r, a16i, chunks_per_seq):
    g, _, nb = xt.shape
    per_g = lambda a: pl.BlockSpec((1,) + a.shape[1:], lambda i: (i, 0, 0))
    return pl.pallas_call(
        functools.partial(_s5_chunk_kernel, chunks_per_seq=chunks_per_seq), grid=(g,),
        in_specs=[per_g(xt), per_g(t_op), per_g(r_op), per_g(o_op), per_g(a16r), per_g(a16i)],
        out_specs=pl.BlockSpec((CHUNK, S5_GROUP_CH, nb), lambda i: (0, i, 0)),
        out_shape=jax.ShapeDtypeStruct((CHUNK, S5_WIDTH, nb), BF16),
        compiler_params=_cparams(1), name="s5_chunk")(xt, t_op, r_op, o_op, a16r, a16i)


def _s5_post_kernel(yg_ref, wgt_ref, bg_ref, gn_ref, o_ref):
    for t in range(CHUNK):
        y = yg_ref[t]
        gate = _dot(wgt_ref[...], y) + bg_ref[...]
        z = y.astype(F32) * jax.nn.sigmoid(gate)
        zn = z * lax.rsqrt(jnp.mean(z * z, axis=0, keepdims=True) + EPS) * gn_ref[...]
        o_ref[0, :, t, :] = zn.T.astype(BF16)


def _s5_post(yg, w_glu_t, b_glu_col, gain_col, batch):
    _, _, nb = yg.shape
    nck = nb // batch
    full = lambda a: pl.BlockSpec(a.shape, lambda i: (0,) * a.ndim)
    return pl.pallas_call(
        _s5_post_kernel, grid=(batch,),
        in_specs=[pl.BlockSpec((CHUNK, S5_WIDTH, nck), lambda i: (0, 0, i)),
                  full(w_glu_t), full(b_glu_col), full(gain_col)],
        out_specs=pl.BlockSpec((1, nck, CHUNK, S5_WIDTH), lambda i: (i, 0, 0, 0)),
        out_shape=jax.ShapeDtypeStruct((batch, nck, CHUNK, S5_WIDTH), BF16),
        compiler_params=_cparams(1), name="s5_post")(yg, w_glu_t, b_glu_col, gain_col)


def _mix_out_kernel(x_ref, fox_ref, s5_ref, gf_ref, wo_ref, gc_ref, wxq_ref, gq_ref, h_ref, xq_ref):
    fox_n = _rms(fox_ref[0].astype(F32), gf_ref[...]).astype(BF16)
    mixed = jnp.concatenate([fox_n, s5_ref[0]], axis=-1)
    h = x_ref[0] + _dot(mixed, wo_ref[...])
    h_ref[0] = h
    hn = _rms(h, gc_ref[...]).astype(BF16)
    xq = _head256_norm(_dot(hn, wxq_ref[...])) * (gq_ref[...] * X_HEAD_DIM ** -0.5)
    xq_ref[0] = xq.astype(BF16)


def _mix_out(x, fox, s5n, g_fox, w_out, g_cross, w_xq, g_xq):
    b, l, d = x.shape
    tm = TOK_TILE
    tok = lambda w: pl.BlockSpec((1, tm, w), lambda i, j: (i, j, 0))
    full = lambda a: pl.BlockSpec(a.shape, lambda i, j: (0,) * a.ndim)
    return pl.pallas_call(
        _mix_out_kernel, grid=(b, l // tm),
        in_specs=[tok(d), tok(FOX_WIDTH), tok(S5_WIDTH), full(g_fox), full(w_out), full(g_cross),
                  full(w_xq), full(g_xq)],
        out_specs=(tok(d), tok(d)),
        out_shape=(jax.ShapeDtypeStruct((b, l, d), F32), jax.ShapeDtypeStruct((b, l, d), BF16)),
        compiler_params=_cparams(2), name="mix_out")(x, fox, s5n, g_fox, w_out, g_cross, w_xq, g_xq)


def _mem_kv_kernel(mem_ref, g_ref, wkv_ref, gk_ref, xk_ref, xv_ref):
    mn = _rms(mem_ref[0], g_ref[...]).astype(BF16)
    kv = _dot(mn, wkv_ref[...])
    xk_ref[0] = (_head256_norm(kv[:, :D_MODEL]) * gk_ref[...]).astype(BF16)
    xv_ref[0] = kv[:, D_MODEL:].astype(BF16)


def _mem_kv(mem, g_mem, w_xkv, g_xk):
    b, n, d = mem.shape
    full = lambda a: pl.BlockSpec(a.shape, lambda i: (0,) * a.ndim)
    blk = pl.BlockSpec((1, n, d), lambda i: (i, 0, 0))
    return pl.pallas_call(
        _mem_kv_kernel, grid=(b,),
        in_specs=[blk, full(g_mem), full(w_xkv), full(g_xk)],
        out_specs=(blk, blk),
        out_shape=(jax.ShapeDtypeStruct((b, n, d), BF16),) * 2,
        compiler_params=_cparams(1), name="mem_kv")(mem, g_mem, w_xkv, g_xk)


def _cross_attn_kernel(h_ref, xq_ref, xk_ref, xv_ref, wxo_ref, o_ref):
    outs = []
    for hd in range(N_X_HEADS):
        sl = slice(hd * X_HEAD_DIM, (hd + 1) * X_HEAD_DIM)
        s = _dot_nt(xq_ref[0, :, sl], xk_ref[0, :, sl])
        p = jnp.exp(s - jnp.max(s, axis=-1, keepdims=True))
        p = p / jnp.sum(p, axis=-1, keepdims=True)
        outs.append(_dot(p.astype(BF16), xv_ref[0, :, sl]).astype(BF16))
    xo = jnp.concatenate(outs, axis=-1)
    o_ref[0] = h_ref[0] + _dot(xo, wxo_ref[...])


def _cross_attn(h, xq, xk, xv, w_xo):
    b, l, d = h.shape
    tm = TOK_TILE
    tok = pl.BlockSpec((1, tm, d), lambda i, j: (i, j, 0))
    mem = pl.BlockSpec((1, N_MEM, d), lambda i, j: (i, 0, 0))
    return pl.pallas_call(
        _cross_attn_kernel, grid=(b, l // tm),
        in_specs=[tok, tok, mem, mem, pl.BlockSpec(w_xo.shape, lambda i, j: (0, 0))],
        out_specs=tok,
        out_shape=jax.ShapeDtypeStruct((b, l, d), F32),
        compiler_params=_cparams(2), name="cross_attn")(h, xq, xk, xv, w_xo)


def _conv_ffn_kernel(h_ref, g_ref, wg_ref, wu_ref, cw_ref, cb_ref, wd_ref, o_ref, tail_ref):
    @pl.when(pl.program_id(1) == 0)
    def _():
        tail_ref[...] = jnp.zeros_like(tail_ref)

    h = h_ref[0]
    tm = h.shape[0]
    hn = _rms(h, g_ref[...]).astype(BF16)
    row = lax.broadcasted_iota(jnp.int32, (tm, FF_CHUNK), 0)
    acc = h
    for c in range(D_FF // FF_CHUNK):
        sl = slice(c * FF_CHUNK, (c + 1) * FF_CHUNK)
        gate = _dot(hn, wg_ref[:, sl])
        up = _dot(hn, wu_ref[:, sl])
        prev1 = tail_ref[1:2, sl]
        prev2 = tail_ref[0:1, sl]
        g1 = jnp.where(row == 0, prev1, pltpu.roll(gate, 1, axis=0))
        g2 = jnp.where(row == 0, prev2, jnp.where(row == 1, prev1, pltpu.roll(gate, 2, axis=0)))
        conv = cb_ref[:, sl] + cw_ref[0:1, sl] * g2 + cw_ref[1:2, sl] * g1 + cw_ref[2:3, sl] * gate
        tail_ref[0:2, sl] = gate[tm - 2:tm, :]
        act = (conv * jax.nn.sigmoid(conv) * up).astype(BF16)
        acc = acc + _dot(act, wd_ref[sl, :])
    o_ref[0] = acc


def _conv_ffn(h, g_ffn, w_gate, w_up, conv_w, conv_b, w_down):
    b, l, d = h.shape
    tm = TOK_TILE
    tok = pl.BlockSpec((1, tm, d), lambda i, j: (i, j, 0))
    full = lambda a: pl.BlockSpec(a.shape, lambda i, j: (0,) * a.ndim)
    return pl.pallas_call(
        _conv_ffn_kernel, grid=(b, l // tm),
        in_specs=[tok, full(g_ffn), full(w_gate), full(w_up), full(conv_w), full(conv_b), full(w_down)],
        out_specs=tok,
        out_shape=jax.ShapeDtypeStruct((b, l, d), F32),
        scratch_shapes=[pltpu.VMEM((8, D_FF), F32)],
        compiler_params=_cparams(2), name="conv_ffn")(h, g_ffn, w_gate, w_up, conv_w, conv_b, w_down)


def _gate_placement():
    place = np.zeros((LANES, FOX_WIDTH), np.float32)
    for h in range(N_FOX_HEADS):
        base = (h // 2) * LANES + (HEAD_DIM if h % 2 == 0 else 0)
        place[h, base:base + 3] = 1.0
    return jnp.asarray(place)


def _layer(h, mem, p):
    b, l, d = h.shape
    nck = l // CHUNK
    row = lambda a: a.reshape(1, -1)
    col = lambda a: a.reshape(-1, 1)
    w_in = p["w_in"]
    qk_end, v_end = 2 * FOX_WIDTH, 3 * FOX_WIDTH
    w_qk = w_in[:, :qk_end].astype(BF16)
    w_vt = w_in[:, qk_end:v_end].T.astype(BF16)
    w_f = jnp.pad(w_in[:, v_end:v_end + N_FOX_HEADS], ((0, 0), (0, LANES - N_FOX_HEADS))).astype(BF16)
    w_ut = w_in[:, v_end + N_FOX_HEADS:].T.astype(BF16)
    f_bias = jnp.pad(p["fox_f_bias"], (0, LANES - N_FOX_HEADS)).reshape(1, LANES)

    t_op, r_op, o_op, a16r, a16i = _s5_params(
        p["s5_log_dt"], p["s5_a_re"], p["s5_a_im"], p["s5_b_re"], p["s5_b_im"],
        p["s5_c_re"], p["s5_c_im"], p["s5_d"])

    q, k, vt, fcol = _in_proj(h, row(p["norm_mix"]), w_qk, w_vt, w_f,
                              row(jnp.tile(p["fox_q_norm"], N_FOX_HEADS)),
                              row(jnp.tile(p["fox_k_norm"], N_FOX_HEADS)))
    cs = _gate_cumsum(fcol, f_bias, _gate_placement())
    fox = _fox_attn(q, k, cs, vt)

    xt = _u_proj(h, row(p["norm_mix"]), w_ut)
    yg = _s5_chunk(xt, t_op, r_op, o_op, a16r, a16i, nck)
    s5n = _s5_post(yg, p["s5_w_glu"].T.astype(BF16), col(p["s5_b_glu"]), col(p["out_norm_s5"]), b)
    s5n = s5n.reshape(b, l, S5_WIDTH)

    h1, xq = _mix_out(h, fox, s5n, row(p["out_norm_fox"]), p["w_out"].astype(BF16),
                      row(p["norm_cross"]), p["w_xq"].astype(BF16),
                      row(jnp.tile(p["xq_norm"], N_X_HEADS)))
    xk, xv = _mem_kv(mem, row(p["norm_mem"]), p["w_xkv"].astype(BF16),
                     row(jnp.tile(p["xk_norm"], N_X_HEADS)))
    h2 = _cross_attn(h1, xq, xk, xv, p["w_xo"].astype(BF16))

    w_up = p["w_ffn_up"]
    return _conv_ffn(h2, row(p["norm_ffn"]), w_up[:, :D_FF].astype(BF16), w_up[:, D_FF:].astype(BF16),
                     p["ffn_conv_w"], row(p["ffn_conv_b"]), p["w_ffn_down"].astype(BF16))


_PARAM_NAMES = (
    "norm_mix", "w_in", "fox_q_norm", "fox_k_norm", "fox_f_bias", "s5_a_re", "s5_a_im", "s5_log_dt",
    "s5_b_re", "s5_b_im", "s5_c_re", "s5_c_im", "s5_d", "s5_w_glu", "s5_b_glu", "out_norm_fox",
    "out_norm_s5", "w_out", "norm_cross", "norm_mem", "w_xq", "w_xkv", "xq_norm", "xk_norm", "w_xo",
    "norm_ffn", "w_ffn_up", "ffn_conv_w", "ffn_conv_b", "w_ffn_down")


def kernel(x, mem, norm_mix, w_in, fox_q_norm, fox_k_norm, fox_f_bias, s5_a_re, s5_a_im, s5_log_dt, s5_b_re, s5_b_im, s5_c_re, s5_c_im, s5_d, s5_w_glu, s5_b_glu, out_norm_fox, out_norm_s5, w_out, norm_cross, norm_mem, w_xq, w_xkv, xq_norm, xk_norm, w_xo, norm_ffn, w_ffn_up, ffn_conv_w, ffn_conv_b, w_ffn_down):
    stacked = (norm_mix, w_in, fox_q_norm, fox_k_norm, fox_f_bias, s5_a_re, s5_a_im, s5_log_dt,
               s5_b_re, s5_b_im, s5_c_re, s5_c_im, s5_d, s5_w_glu, s5_b_glu, out_norm_fox,
               out_norm_s5, w_out, norm_cross, norm_mem, w_xq, w_xkv, xq_norm, xk_norm, w_xo,
               norm_ffn, w_ffn_up, ffn_conv_w, ffn_conv_b, w_ffn_down)
    h = x
    for layer in range(norm_mix.shape[0]):
        h = _layer(h, mem, {n: a[layer] for n, a in zip(_PARAM_NAMES, stacked)})
    return h
```

```python
import functools
import math

import jax
import jax.numpy as jnp
import numpy as np
from jax import lax
from jax.experimental import pallas as pl
from jax.experimental.pallas import tpu as pltpu

F32 = jnp.float32
BF16 = jnp.bfloat16

D_MODEL = 1024
N_MEM = 256
FOX_WIDTH = 512
HEAD_DIM = 64
N_FOX_HEADS = 8
S5_WIDTH = 512
S5_GROUP_CH = 16
S5_GROUPS = 32
S5_STATE = 64
N_X_HEADS = 4
X_HEAD_DIM = 256
D_FF = 2816
CONV_W = 3
EPS = 1e-6

LANES = 128
CHUNK = 16
CHUNK_W = CHUNK * S5_GROUP_CH
TOK_TILE = 512
ATT_BLOCK = 256
FF_CHUNK = 256
NEG_BIG = -1e30
LOG2E = math.log2(math.e)
HI = lax.Precision.HIGHEST
VMEM_LIMIT = 56 * 1024 * 1024

NT_DIMS = (((1,), (1,)), ((), ()))


def _cparams(n_axes):
    return pltpu.CompilerParams(dimension_semantics=("arbitrary",) * n_axes,
                                vmem_limit_bytes=VMEM_LIMIT)


def _rms(xf, g):
    return xf * lax.rsqrt(jnp.mean(xf * xf, axis=-1, keepdims=True) + EPS) * g


def _dot(a, b):
    return jnp.dot(a, b, preferred_element_type=F32)


def _dot_nt(a, b):
    return lax.dot_general(a, b, NT_DIMS, preferred_element_type=F32)


def _head64_norm(x):
    lo = lax.broadcasted_iota(jnp.int32, (x.shape[0], LANES), 1) < HEAD_DIM
    outs = []
    for j in range(x.shape[1] // LANES):
        blk = x[:, j * LANES:(j + 1) * LANES]
        sq = blk * blk
        s_lo = jnp.sum(jnp.where(lo, sq, 0.0), axis=-1, keepdims=True)
        s_hi = jnp.sum(jnp.where(lo, 0.0, sq), axis=-1, keepdims=True)
        inv = jnp.where(lo, lax.rsqrt(s_lo * (1.0 / HEAD_DIM) + EPS),
                        lax.rsqrt(s_hi * (1.0 / HEAD_DIM) + EPS))
        outs.append(blk * inv)
    return jnp.concatenate(outs, axis=-1)


def _head256_norm(x):
    outs = []
    for h in range(x.shape[1] // X_HEAD_DIM):
        blk = x[:, h * X_HEAD_DIM:(h + 1) * X_HEAD_DIM]
        outs.append(blk * lax.rsqrt(jnp.mean(blk * blk, axis=-1, keepdims=True) + EPS))
    return jnp.concatenate(outs, axis=-1)


def _log_sigmoid(z):
    return jnp.minimum(z, 0.0) - jnp.log(1.0 + jnp.exp(-jnp.abs(z)))


def _gelu_tanh(x):
    return 0.5 * x * (1.0 + jnp.tanh(math.sqrt(2.0 / math.pi) * (x + 0.044715 * (x * x * x))))


def _s5_params_kernel(ldt_ref, arc_ref, aic_ref, arr_ref, air_ref, bre_ref, bim_ref,
                      cre2_ref, cim2_ref, d_ref,
                      tr_ref, o_ref, a16r_ref, a16i_ref):
    dt = jnp.exp(ldt_ref[0])
    ar, ai = arc_ref[0], aic_ref[0]
    mag = jnp.exp(ar * dt)
    lr, li = mag * jnp.cos(ai * dt), mag * jnp.sin(ai * dt)
    den = ar * ar + ai * ai
    nr = lr - 1.0
    coef_r = (nr * ar + li * ai) / den
    coef_i = (li * ar - nr * ai) / den
    bb_r = coef_r * bre_ref[0] - coef_i * bim_ref[0]
    bb_i = coef_r * bim_ref[0] + coef_i * bre_ref[0]
    arr, air = arr_ref[0], air_ref[0]
    magr = jnp.exp(arr * dt)
    lrr, lir = magr * jnp.cos(air * dt), magr * jnp.sin(air * dt)

    lane_t = lax.broadcasted_iota(jnp.int32, (S5_STATE, CHUNK_W), 1) // S5_GROUP_CH
    lane16 = lax.broadcasted_iota(jnp.int32, (S5_GROUP_CH, CHUNK_W), 1)
    row16 = lax.broadcasted_iota(jnp.int32, (S5_GROUP_CH, CHUNK_W), 0)
    lo128 = lax.broadcasted_iota(jnp.int32, (S5_GROUP_CH, 2 * S5_STATE), 1) < S5_STATE
    c_cat = jnp.where(lo128, cre2_ref[0], -cim2_ref[0])
    d_col = d_ref[0]

    m_r = jnp.zeros((S5_STATE, CHUNK_W), F32)
    m_i = jnp.zeros((S5_STATE, CHUNK_W), F32)
    w_r, w_i = cre2_ref[0], cim2_ref[0]
    for tq in range(CHUNK):
        hit = lane_t == tq
        m_r, m_i = (lr * m_r - li * m_i + jnp.where(hit, bb_r, 0.0),
                    lr * m_i + li * m_r + jnp.where(hit, bb_i, 0.0))
        m_cat = jnp.concatenate([m_r, m_i], axis=0)
        t_rows = jnp.dot(c_cat, m_cat, preferred_element_type=F32, precision=HI)
        t_rows = t_rows + jnp.where(lane16 == tq * S5_GROUP_CH + row16, d_col, 0.0)
        tr_ref[0, tq * S5_GROUP_CH:(tq + 1) * S5_GROUP_CH, :] = t_rows.astype(BF16)
        w_r, w_i = w_r * lrr - w_i * lir, w_r * lir + w_i * lrr
        o_ref[0, tq * S5_GROUP_CH:(tq + 1) * S5_GROUP_CH, :] = jnp.where(lo128, w_r, -w_i).astype(BF16)
    tr_ref[0, CHUNK_W:CHUNK_W + S5_STATE, :] = m_r.astype(BF16)
    tr_ref[0, CHUNK_W + S5_STATE:CHUNK_W + 2 * S5_STATE, :] = m_i.astype(BF16)
    pr, pi = lr, li
    for _ in range(4):
        pr, pi = pr * pr - pi * pi, 2.0 * pr * pi
    a16r_ref[0] = jnp.broadcast_to(pr, (S5_STATE, LANES))
    a16i_ref[0] = jnp.broadcast_to(pi, (S5_STATE, LANES))


def _s5_params(s5_log_dt, s5_a_re, s5_a_im, s5_b_re, s5_b_im, s5_c_re, s5_c_im, s5_d):
    g, p, c = S5_GROUPS, S5_STATE, S5_GROUP_CH
    args = (
        s5_log_dt.reshape(g, 1, 1),
        s5_a_re.reshape(g, p, 1), s5_a_im.reshape(g, p, 1),
        jnp.tile(s5_a_re.reshape(g, 1, p), (1, 1, 2)), jnp.tile(s5_a_im.reshape(g, 1, p), (1, 1, 2)),
        jnp.tile(s5_b_re, (1, 1, CHUNK)), jnp.tile(s5_b_im, (1, 1, CHUNK)),
        jnp.tile(s5_c_re, (1, 1, 2)), jnp.tile(s5_c_im, (1, 1, 2)),
        s5_d.reshape(g, c, 1),
    )
    in_specs = [pl.BlockSpec((1,) + a.shape[1:], lambda i: (i, 0, 0)) for a in args]
    out_shape = (
        jax.ShapeDtypeStruct((g, CHUNK_W + 2 * p, CHUNK_W), BF16),
        jax.ShapeDtypeStruct((g, CHUNK_W, 2 * p), BF16),
        jax.ShapeDtypeStruct((g, p, LANES), F32),
        jax.ShapeDtypeStruct((g, p, LANES), F32),
    )
    out_specs = tuple(pl.BlockSpec((1,) + s.shape[1:], lambda i: (i, 0, 0)) for s in out_shape)
    return pl.pallas_call(
        _s5_params_kernel, grid=(g,), in_specs=in_specs, out_specs=out_specs, out_shape=out_shape,
        compiler_params=_cparams(1), name="s5_params")(*args)


def _in_proj_kernel(x_ref, g_ref, wqk_ref, wvt_ref, qg_ref, kg_ref, q_ref, k_ref, vt_ref):
    hn = _rms(x_ref[0], g_ref[...]).astype(BF16)
    qk = _dot(hn, wqk_ref[...])
    q = _head64_norm(qk[:, :FOX_WIDTH]) * (qg_ref[...] * (HEAD_DIM ** -0.5 * LOG2E))
    k = _head64_norm(qk[:, FOX_WIDTH:]) * kg_ref[...]
    q_ref[0] = q.astype(BF16)
    k_ref[0] = k.astype(BF16)
    vt_ref[0] = _dot_nt(wvt_ref[...], hn).astype(BF16)


def _in_proj(x, norm_mix, w_qk, w_vt, q_gain, k_gain):
    b, l, d = x.shape
    tm = TOK_TILE
    tok = lambda w: pl.BlockSpec((1, tm, w), lambda i, j: (i, j, 0))
    full = lambda a: pl.BlockSpec(a.shape, lambda i, j: (0,) * a.ndim)
    return pl.pallas_call(
        _in_proj_kernel, grid=(b, l // tm),
        in_specs=[tok(d), full(norm_mix), full(w_qk), full(w_vt), full(q_gain), full(k_gain)],
        out_specs=(tok(FOX_WIDTH), tok(FOX_WIDTH), pl.BlockSpec((1, FOX_WIDTH, tm), lambda i, j: (i, 0, j))),
        out_shape=(jax.ShapeDtypeStruct((b, l, FOX_WIDTH), BF16),) * 2
        + (jax.ShapeDtypeStruct((b, FOX_WIDTH, l), BF16),),
        compiler_params=_cparams(2), name="in_proj")(x, norm_mix, w_qk, w_vt, q_gain, k_gain)


def _pack3(v):
    hi = v.astype(BF16).astype(F32)
    rest = v - hi
    mid = rest.astype(BF16).astype(F32)
    lo = rest - mid
    return (hi + pltpu.roll(mid, N_FOX_HEADS, axis=1) + pltpu.roll(lo, 2 * N_FOX_HEADS, axis=1)).astype(BF16)


def _u_gate_kernel(x_ref, g_ref, wuf_ref, bias_ref, place_ref, xt_ref, cs_ref, z_ref, f_ref):
    seq = x_ref.shape[1]
    nck = seq // CHUNK
    n_slab = S5_WIDTH // LANES
    grp_per_slab = LANES // S5_GROUP_CH
    for r in range(seq // TOK_TILE):
        rows = slice(r * TOK_TILE, (r + 1) * TOK_TILE)
        hn = _rms(x_ref[0, rows, :], g_ref[...]).astype(BF16)
        z = _dot(hn, wuf_ref[...])
        for j in range(n_slab):
            z_ref[j, rows, :] = z[:, j * LANES:(j + 1) * LANES]
        f_ref[rows, :] = z[:, S5_WIDTH:]
    for t in range(CHUNK):
        for j in range(n_slab):
            blk = z_ref[j, pl.ds(t, nck, stride=CHUNK), :]
            xt_ref[j * grp_per_slab:(j + 1) * grp_per_slab, t * S5_GROUP_CH:(t + 1) * S5_GROUP_CH, :] = (
                blk.T.reshape(grp_per_slab, S5_GROUP_CH, nck).astype(BF16))

    head_lane = lax.broadcasted_iota(jnp.int32, (LANES, LANES), 1) < N_FOX_HEADS
    lower = (lax.broadcasted_iota(jnp.int32, (LANES, LANES), 0)
             >= lax.broadcasted_iota(jnp.int32, (LANES, LANES), 1)).astype(BF16)
    local = []
    for blk in range(seq // LANES):
        rows = slice(blk * LANES, (blk + 1) * LANES)
        log2f = jnp.where(head_lane, _log_sigmoid(f_ref[rows, :] + bias_ref[...]) * LOG2E, 0.0)
        part = _dot(lower, _pack3(log2f))
        c = part + pltpu.roll(part, LANES - N_FOX_HEADS, axis=1) + pltpu.roll(part, LANES - 2 * N_FOX_HEADS, axis=1)
        local.append(jnp.where(head_lane, c, 0.0))
    carry = jnp.zeros((1, LANES), F32)
    for blk in range(seq // LANES):
        c = local[blk] + carry
        carry = c[LANES - 1:LANES, :]
        cs_ref[0, blk * LANES:(blk + 1) * LANES, :] = _dot(_pack3(c), place_ref[...]).astype(BF16)


def _u_gate(x, norm_mix, w_uf, bias_row, place):
    b, l, d = x.shape
    nck = l // CHUNK
    full = lambda a: pl.BlockSpec(a.shape, lambda i: (0,) * a.ndim)
    return pl.pallas_call(
        _u_gate_kernel, grid=(b,),
        in_specs=[pl.BlockSpec((1, l, d), lambda i: (i, 0, 0)), full(norm_mix), full(w_uf), full(bias_row),
                  full(place)],
        out_specs=(pl.BlockSpec((S5_GROUPS, CHUNK_W, nck), lambda i: (0, 0, i)),
                   pl.BlockSpec((1, l, FOX_WIDTH), lambda i: (i, 0, 0))),
        out_shape=(jax.ShapeDtypeStruct((S5_GROUPS, CHUNK_W, b * nck), BF16),
                   jax.ShapeDtypeStruct((b, l, FOX_WIDTH), BF16)),
        scratch_shapes=[pltpu.VMEM((S5_WIDTH // LANES, l, LANES), F32), pltpu.VMEM((l, LANES), F32)],
        compiler_params=_cparams(1), name="u_gate")(x, norm_mix, w_uf, bias_row, place)


def _fox_kernel(q_ref, k_ref, cs_ref, vt_ref, o_ref, kaug_ref, s_ref):
    blk = ATT_BLOCK
    seq = q_ref.shape[1]
    lane_k = lax.broadcasted_iota(jnp.int32, (seq, LANES), 1)
    k2 = k_ref[0]
    cs = cs_ref[0]
    kaug_ref[0] = jnp.where(lane_k < HEAD_DIM, k2, cs)
    kaug_ref[1] = jnp.where(lane_k >= HEAD_DIM, k2, cs)

    lane_q = lax.broadcasted_iota(jnp.int32, (blk, LANES), 1)
    minus1 = [jnp.where((lane_q >= HEAD_DIM) & (lane_q < HEAD_DIM + 3), -1.0, 0.0).astype(BF16),
              jnp.where(lane_q < 3, -1.0, 0.0).astype(BF16)]
    own = [lane_q < HEAD_DIM, lane_q >= HEAD_DIM]
    key_pos = lax.broadcasted_iota(jnp.int32, (blk, blk), 0)
    qry_pos = lax.broadcasted_iota(jnp.int32, (blk, blk), 1)

    def scores(i):
        q2 = q_ref[0, i * blk:(i + 1) * blk, :]
        for hh in range(2):
            qa = jnp.where(own[hh], q2, minus1[hh])
            st = _dot_nt(kaug_ref[hh, 0:(i + 1) * blk, :], qa)
            if i > 0:
                s_ref[i % 2, hh, 0:i * blk, :] = st[:i * blk]
            s_ref[i % 2, hh, i * blk:(i + 1) * blk, :] = jnp.where(key_pos > qry_pos, NEG_BIG, st[i * blk:])

    def attend(i):
        keys = (i + 1) * blk
        outs = []
        for hh in range(2):
            st = s_ref[i % 2, hh, 0:keys, :]
            p = jnp.exp2(st - jnp.max(st, axis=0, keepdims=True))
            l = jnp.sum(p, axis=0, keepdims=True)
            acc = _dot(vt_ref[0, hh * HEAD_DIM:(hh + 1) * HEAD_DIM, 0:keys], p.astype(BF16))
            outs.append(acc / l)
        o_ref[0, i * blk:(i + 1) * blk, :] = jnp.concatenate(outs, axis=0).T.astype(BF16)

    n_blk = seq // blk
    scores(0)
    for i in range(n_blk):
        if i + 1 < n_blk:
            scores(i + 1)
        attend(i)


def _fox_attn(q, k, cs, vt):
    b, l, _ = q.shape
    n_pairs = FOX_WIDTH // LANES
    tok = pl.BlockSpec((1, l, LANES), lambda i, j: (i, 0, j))
    return pl.pallas_call(
        _fox_kernel, grid=(b, n_pairs),
        in_specs=[tok, tok, tok, pl.BlockSpec((1, LANES, l), lambda i, j: (i, j, 0))],
        out_specs=tok,
        out_shape=jax.ShapeDtypeStruct((b, l, FOX_WIDTH), BF16),
        scratch_shapes=[pltpu.VMEM((2, l, LANES), BF16), pltpu.VMEM((2, 2, l, ATT_BLOCK), F32)],
        compiler_params=_cparams(2), name="fox_attn")(q, k, cs, vt)


def _s5_chunk_kernel(xt_ref, tr_ref, o_ref, a16r_ref, a16i_ref, yg_ref, *, chunks_per_seq):
    lane_tile = min(4 * LANES, xt_ref.shape[2])
    n_tiles = xt_ref.shape[2] // lane_tile
    n_levels = chunks_per_seq.bit_length() - 1
    reps = lane_tile // LANES
    kpos = lax.broadcasted_iota(jnp.int32, (S5_STATE, LANES), 1) % chunks_per_seq
    a_r, a_i = a16r_ref[0], a16i_ref[0]
    coef = []
    for lvl in range(n_levels):
        keep = kpos >= (1 << lvl)
        coef.append((jnp.concatenate([jnp.where(keep, a_r, 0.0)] * reps, axis=1),
                     jnp.concatenate([jnp.where(keep, a_i, 0.0)] * reps, axis=1)))
        a_r, a_i = a_r * a_r - a_i * a_i, 2.0 * a_r * a_i
    first = jnp.concatenate([kpos] * reps, axis=1) == 0
    stage1 = [_dot(tr_ref[0], xt_ref[0, :, n * lane_tile:(n + 1) * lane_tile]) for n in range(n_tiles)]
    for n in range(n_tiles):
        yr = stage1[n]
        s_r, s_i = yr[CHUNK_W:CHUNK_W + S5_STATE], yr[CHUNK_W + S5_STATE:]
        for lvl in range(n_levels):
            c_r, c_i = coef[lvl]
            p_r = pltpu.roll(s_r, 1 << lvl, axis=1)
            p_i = pltpu.roll(s_i, 1 << lvl, axis=1)
            s_r, s_i = s_r + (c_r * p_r - c_i * p_i), s_i + (c_r * p_i + c_i * p_r)
        e_r = jnp.where(first, 0.0, pltpu.roll(s_r, 1, axis=1))
        e_i = jnp.where(first, 0.0, pltpu.roll(s_i, 1, axis=1))
        s_in = jnp.concatenate([e_r, e_i], axis=0).astype(BF16)
        y = yr[:CHUNK_W] + _dot(o_ref[0], s_in)
        yg = _gelu_tanh(y).astype(BF16)
        yg_ref[:, :, n * lane_tile:(n + 1) * lane_tile] = yg.reshape(CHUNK, S5_GROUP_CH, lane_tile)


def _s5_chunk(xt, tr_op, o_op, a16r, a16i, chunks_per_seq):
    g, _, nb = xt.shape
    per_g = lambda a: pl.BlockSpec((1,) + a.shape[1:], lambda i: (i, 0, 0))
    return pl.pallas_call(
        functools.partial(_s5_chunk_kernel, chunks_per_seq=chunks_per_seq), grid=(g,),
        in_specs=[per_g(xt), per_g(tr_op), per_g(o_op), per_g(a16r), per_g(a16i)],
        out_specs=pl.BlockSpec((CHUNK, S5_GROUP_CH, nb), lambda i: (0, i, 0)),
        out_shape=jax.ShapeDtypeStruct((CHUNK, S5_WIDTH, nb), BF16),
        compiler_params=_cparams(1), name="s5_chunk")(xt, tr_op, o_op, a16r, a16i)


def _s5_post_kernel(yg_ref, wgt_ref, bg_ref, gn_ref, o_ref):
    for t in range(CHUNK):
        y = yg_ref[t]
        gate = _dot(wgt_ref[...], y) + bg_ref[...]
        z = y.astype(F32) * jax.nn.sigmoid(gate)
        zn = z * lax.rsqrt(jnp.mean(z * z, axis=0, keepdims=True) + EPS) * gn_ref[...]
        o_ref[0, :, t, :] = zn.T.astype(BF16)


def _s5_post(yg, w_glu_t, b_glu_col, gain_col, batch):
    _, _, nb = yg.shape
    nck = nb // batch
    full = lambda a: pl.BlockSpec(a.shape, lambda i: (0,) * a.ndim)
    return pl.pallas_call(
        _s5_post_kernel, grid=(batch,),
        in_specs=[pl.BlockSpec((CHUNK, S5_WIDTH, nck), lambda i: (0, 0, i)),
                  full(w_glu_t), full(b_glu_col), full(gain_col)],
        out_specs=pl.BlockSpec((1, nck, CHUNK, S5_WIDTH), lambda i: (i, 0, 0, 0)),
        out_shape=jax.ShapeDtypeStruct((batch, nck, CHUNK, S5_WIDTH), BF16),
        compiler_params=_cparams(1), name="s5_post")(yg, w_glu_t, b_glu_col, gain_col)


def _mix_out_kernel(x_ref, fox_ref, s5_ref, gf_ref, wo_ref, gc_ref, wxq_ref, gq_ref, h_ref, xq_ref):
    fox_n = _rms(fox_ref[0].astype(F32), gf_ref[...]).astype(BF16)
    mixed = jnp.concatenate([fox_n, s5_ref[0]], axis=-1)
    h = x_ref[0] + _dot(mixed, wo_ref[...])
    h_ref[0] = h
    hn = _rms(h, gc_ref[...]).astype(BF16)
    xq = _head256_norm(_dot(hn, wxq_ref[...])) * (gq_ref[...] * X_HEAD_DIM ** -0.5)
    xq_ref[0] = xq.astype(BF16)


def _mix_out(x, fox, s5n, g_fox, w_out, g_cross, w_xq, g_xq):
    b, l, d = x.shape
    tm = TOK_TILE
    tok = lambda w: pl.BlockSpec((1, tm, w), lambda i, j: (i, j, 0))
    full = lambda a: pl.BlockSpec(a.shape, lambda i, j: (0,) * a.ndim)
    return pl.pallas_call(
        _mix_out_kernel, grid=(b, l // tm),
        in_specs=[tok(d), tok(FOX_WIDTH), tok(S5_WIDTH), full(g_fox), full(w_out), full(g_cross),
                  full(w_xq), full(g_xq)],
        out_specs=(tok(d), tok(d)),
        out_shape=(jax.ShapeDtypeStruct((b, l, d), F32), jax.ShapeDtypeStruct((b, l, d), BF16)),
        compiler_params=_cparams(2), name="mix_out")(x, fox, s5n, g_fox, w_out, g_cross, w_xq, g_xq)


def _mem_kv_kernel(mem_ref, g_ref, wkv_ref, gk_ref, xk_ref, xv_ref):
    mn = _rms(mem_ref[0], g_ref[...]).astype(BF16)
    kv = _dot(mn, wkv_ref[...])
    xk_ref[0] = (_head256_norm(kv[:, :D_MODEL]) * gk_ref[...]).astype(BF16)
    xv_ref[0] = kv[:, D_MODEL:].astype(BF16)


def _mem_kv(mem, g_mem, w_xkv, g_xk):
    b, n, d = mem.shape
    full = lambda a: pl.BlockSpec(a.shape, lambda i: (0,) * a.ndim)
    blk = pl.BlockSpec((1, n, d), lambda i: (i, 0, 0))
    return pl.pallas_call(
        _mem_kv_kernel, grid=(b,),
        in_specs=[blk, full(g_mem), full(w_xkv), full(g_xk)],
        out_specs=(blk, blk),
        out_shape=(jax.ShapeDtypeStruct((b, n, d), BF16),) * 2,
        compiler_params=_cparams(1), name="mem_kv")(mem, g_mem, w_xkv, g_xk)


def _cross_attn_kernel(h_ref, xq_ref, xk_ref, xv_ref, wxo_ref, o_ref):
    heads = [slice(hd * X_HEAD_DIM, (hd + 1) * X_HEAD_DIM) for hd in range(N_X_HEADS)]
    scores = [_dot_nt(xq_ref[0, :, sl], xk_ref[0, :, sl]) for sl in heads]
    outs = []
    for s, sl in zip(scores, heads):
        p = jnp.exp(s - jnp.max(s, axis=-1, keepdims=True))
        p = p / jnp.sum(p, axis=-1, keepdims=True)
        outs.append(_dot(p.astype(BF16), xv_ref[0, :, sl]).astype(BF16))
    xo = jnp.concatenate(outs, axis=-1)
    o_ref[0] = h_ref[0] + _dot(xo, wxo_ref[...])


def _cross_attn(h, xq, xk, xv, w_xo):
    b, l, d = h.shape
    tm = TOK_TILE
    tok = pl.BlockSpec((1, tm, d), lambda i, j: (i, j, 0))
    mem = pl.BlockSpec((1, N_MEM, d), lambda i, j: (i, 0, 0))
    return pl.pallas_call(
        _cross_attn_kernel, grid=(b, l // tm),
        in_specs=[tok, tok, mem, mem, pl.BlockSpec(w_xo.shape, lambda i, j: (0, 0))],
        out_specs=tok,
        out_shape=jax.ShapeDtypeStruct((b, l, d), F32),
        compiler_params=_cparams(2), name="cross_attn")(h, xq, xk, xv, w_xo)


def _conv_ffn_kernel(h_ref, g_ref, wg_ref, wu_ref, cw_ref, cb_ref, wd_ref, o_ref, tail_ref, act_ref):
    @pl.when(pl.program_id(1) == 0)
    def _():
        tail_ref[...] = jnp.zeros_like(tail_ref)

    h = h_ref[0]
    tm = h.shape[0]
    hn = _rms(h, g_ref[...]).astype(BF16)
    row = lax.broadcasted_iota(jnp.int32, (tm, FF_CHUNK), 0)
    for c in range(D_FF // FF_CHUNK):
        sl = slice(c * FF_CHUNK, (c + 1) * FF_CHUNK)
        gate = _dot(hn, wg_ref[:, sl])
        up = _dot(hn, wu_ref[:, sl])
        prev1 = tail_ref[1:2, sl]
        prev2 = tail_ref[0:1, sl]
        g1 = jnp.where(row == 0, prev1, pltpu.roll(gate, 1, axis=0))
        g2 = jnp.where(row == 0, prev2, jnp.where(row == 1, prev1, pltpu.roll(gate, 2, axis=0)))
        conv = cb_ref[:, sl] + cw_ref[0:1, sl] * g2 + cw_ref[1:2, sl] * g1 + cw_ref[2:3, sl] * gate
        tail_ref[0:2, sl] = gate[tm - 2:tm, :]
        act_ref[:, sl] = (conv * jax.nn.sigmoid(conv) * up).astype(BF16)
    o_ref[0] = h + _dot(act_ref[...], wd_ref[...])


def _conv_ffn(h, g_ffn, w_gate, w_up, conv_w, conv_b, w_down):
    b, l, d = h.shape
    tm = TOK_TILE
    tok = pl.BlockSpec((1, tm, d), lambda i, j: (i, j, 0))
    full = lambda a: pl.BlockSpec(a.shape, lambda i, j: (0,) * a.ndim)
    return pl.pallas_call(
        _conv_ffn_kernel, grid=(b, l // tm),
        in_specs=[tok, full(g_ffn), full(w_gate), full(w_up), full(conv_w), full(conv_b), full(w_down)],
        out_specs=tok,
        out_shape=jax.ShapeDtypeStruct((b, l, d), F32),
        scratch_shapes=[pltpu.VMEM((8, D_FF), F32), pltpu.VMEM((tm, D_FF), BF16)],
        compiler_params=_cparams(2), name="conv_ffn")(h, g_ffn, w_gate, w_up, conv_w, conv_b, w_down)


def _gate_placement():
    place = np.zeros((LANES, FOX_WIDTH), np.float32)
    for h in range(N_FOX_HEADS):
        base = (h // 2) * LANES + (HEAD_DIM if h % 2 == 0 else 0)
        for term in range(3):
            place[term * N_FOX_HEADS + h, base + term] = 1.0
    return jnp.asarray(place, dtype=BF16)


def _layer(h, mem, p):
    b, l, d = h.shape
    nck = l // CHUNK
    row = lambda a: a.reshape(1, -1)
    col = lambda a: a.reshape(-1, 1)
    w_in = p["w_in"]
    qk_end, v_end = 2 * FOX_WIDTH, 3 * FOX_WIDTH
    w_qk = w_in[:, :qk_end].astype(BF16)
    w_vt = w_in[:, qk_end:v_end].T.astype(BF16)
    w_uf = jnp.concatenate(
        [w_in[:, v_end + N_FOX_HEADS:],
         jnp.pad(w_in[:, v_end:v_end + N_FOX_HEADS], ((0, 0), (0, LANES - N_FOX_HEADS)))], axis=1).astype(BF16)
    f_bias = jnp.pad(p["fox_f_bias"], (0, LANES - N_FOX_HEADS)).reshape(1, LANES)

    tr_op, o_op, a16r, a16i = _s5_params(
        p["s5_log_dt"], p["s5_a_re"], p["s5_a_im"], p["s5_b_re"], p["s5_b_im"],
        p["s5_c_re"], p["s5_c_im"], p["s5_d"])

    q, k, vt = _in_proj(h, row(p["norm_mix"]), w_qk, w_vt,
                        row(jnp.tile(p["fox_q_norm"], N_FOX_HEADS)),
                        row(jnp.tile(p["fox_k_norm"], N_FOX_HEADS)))
    xt, cs = _u_gate(h, row(p["norm_mix"]), w_uf, f_bias, _gate_placement())
    fox = _fox_attn(q, k, cs, vt)

    yg = _s5_chunk(xt, tr_op, o_op, a16r, a16i, nck)
    s5n = _s5_post(yg, p["s5_w_glu"].T.astype(BF16), col(p["s5_b_glu"]), col(p["out_norm_s5"]), b)
    s5n = s5n.reshape(b, l, S5_WIDTH)

    h1, xq = _mix_out(h, fox, s5n, row(p["out_norm_fox"]), p["w_out"].astype(BF16),
                      row(p["norm_cross"]), p["w_xq"].astype(BF16),
                      row(jnp.tile(p["xq_norm"], N_X_HEADS)))
    xk, xv = _mem_kv(mem, row(p["norm_mem"]), p["w_xkv"].astype(BF16),
                     row(jnp.tile(p["xk_norm"], N_X_HEADS)))
    h2 = _cross_attn(h1, xq, xk, xv, p["w_xo"].astype(BF16))

    w_up = p["w_ffn_up"]
    return _conv_ffn(h2, row(p["norm_ffn"]), w_up[:, :D_FF].astype(BF16), w_up[:, D_FF:].astype(BF16),
                     p["ffn_conv_w"], row(p["ffn_conv_b"]), p["w_ffn_down"].astype(BF16))


_PARAM_NAMES = (
    "norm_mix", "w_in", "fox_q_norm", "fox_k_norm", "fox_f_bias", "s5_a_re", "s5_a_im", "s5_log_dt",
    "s5_b_re", "s5_b_im", "s5_c_re", "s5_c_im", "s5_d", "s5_w_glu", "s5_b_glu", "out_norm_fox",
    "out_norm_s5", "w_out", "norm_cross", "norm_mem", "w_xq", "w_xkv", "xq_norm", "xk_norm", "w_xo",
    "norm_ffn", "w_ffn_up", "ffn_conv_w", "ffn_conv_b", "w_ffn_down")


def kernel(x, mem, norm_mix, w_in, fox_q_norm, fox_k_norm, fox_f_bias, s5_a_re, s5_a_im, s5_log_dt, s5_b_re, s5_b_im, s5_c_re, s5_c_im, s5_d, s5_w_glu, s5_b_glu, out_norm_fox, out_norm_s5, w_out, norm_cross, norm_mem, w_xq, w_xkv, xq_norm, xk_norm, w_xo, norm_ffn, w_ffn_up, ffn_conv_w, ffn_conv_b, w_ffn_down):
    stacked = (norm_mix, w_in, fox_q_norm, fox_k_norm, fox_f_bias, s5_a_re, s5_a_im, s5_log_dt,
               s5_b_re, s5_b_im, s5_c_re, s5_c_im, s5_d, s5_w_glu, s5_b_glu, out_norm_fox,
               out_norm_s5, w_out, norm_cross, norm_mem, w_xq, w_xkv, xq_norm, xk_norm, w_xo,
               norm_ffn, w_ffn_up, ffn_conv_w, ffn_conv_b, w_ffn_down)
    h = x
    for layer in range(norm_mix.shape[0]):
        h = _layer(h, mem, {n: a[layer] for n, a in zip(_PARAM_NAMES, stacked)})
    return h
```

```python
import functools
import math

import jax
import jax.numpy as jnp
import numpy as np
from jax import lax
from jax.experimental import pallas as pl
from jax.experimental.pallas import tpu as pltpu

F32 = jnp.float32
BF16 = jnp.bfloat16

D_MODEL = 1024
N_MEM = 256
FOX_WIDTH = 512
HEAD_DIM = 64
N_FOX_HEADS = 8
S5_WIDTH = 512
S5_GROUP_CH = 16
S5_GROUPS = 32
S5_STATE = 64
N_X_HEADS = 4
X_HEAD_DIM = 256
D_FF = 2816
CONV_W = 3
EPS = 1e-6

LANES = 128
CHUNK = 16
CHUNK_W = CHUNK * S5_GROUP_CH
TOK_TILE = 512
ATT_BLOCK = 256
FF_CHUNK = 256
NEG_BIG = -1e30
LOG2E = math.log2(math.e)
HI = lax.Precision.HIGHEST
VMEM_LIMIT = 56 * 1024 * 1024

NT_DIMS = (((1,), (1,)), ((), ()))


def _cparams(n_axes):
    return pltpu.CompilerParams(dimension_semantics=("arbitrary",) * n_axes,
                                vmem_limit_bytes=VMEM_LIMIT)


def _rms(xf, g):
    return xf * lax.rsqrt(jnp.mean(xf * xf, axis=-1, keepdims=True) + EPS) * g


def _dot(a, b):
    return jnp.dot(a, b, preferred_element_type=F32)


def _dot_nt(a, b):
    return lax.dot_general(a, b, NT_DIMS, preferred_element_type=F32)


def _head64_norm(x):
    lo = lax.broadcasted_iota(jnp.int32, (x.shape[0], LANES), 1) < HEAD_DIM
    outs = []
    for j in range(x.shape[1] // LANES):
        blk = x[:, j * LANES:(j + 1) * LANES]
        sq = blk * blk
        s_lo = jnp.sum(jnp.where(lo, sq, 0.0), axis=-1, keepdims=True)
        s_hi = jnp.sum(jnp.where(lo, 0.0, sq), axis=-1, keepdims=True)
        inv = jnp.where(lo, lax.rsqrt(s_lo * (1.0 / HEAD_DIM) + EPS),
                        lax.rsqrt(s_hi * (1.0 / HEAD_DIM) + EPS))
        outs.append(blk * inv)
    return jnp.concatenate(outs, axis=-1)


def _head256_norm(x):
    outs = []
    for h in range(x.shape[1] // X_HEAD_DIM):
        blk = x[:, h * X_HEAD_DIM:(h + 1) * X_HEAD_DIM]
        outs.append(blk * lax.rsqrt(jnp.mean(blk * blk, axis=-1, keepdims=True) + EPS))
    return jnp.concatenate(outs, axis=-1)


def _row_halves(rows):
    return [slice(0, rows // 2), slice(rows // 2, rows)]


def _log_sigmoid(z):
    return jnp.minimum(z, 0.0) - jnp.log(1.0 + jnp.exp(-jnp.abs(z)))


def _gelu_tanh(x):
    return 0.5 * x * (1.0 + jnp.tanh(math.sqrt(2.0 / math.pi) * (x + 0.044715 * (x * x * x))))


def _s5_params_kernel(ldt_ref, arc_ref, aic_ref, arr_ref, air_ref, bre_ref, bim_ref,
                      cre2_ref, cim2_ref, d_ref,
                      tr_ref, o_ref, a16r_ref, a16i_ref):
    dt = jnp.exp(ldt_ref[0])
    ar, ai = arc_ref[0], aic_ref[0]
    mag = jnp.exp(ar * dt)
    lr, li = mag * jnp.cos(ai * dt), mag * jnp.sin(ai * dt)
    den = ar * ar + ai * ai
    nr = lr - 1.0
    coef_r = (nr * ar + li * ai) / den
    coef_i = (li * ar - nr * ai) / den
    bb_r = coef_r * bre_ref[0] - coef_i * bim_ref[0]
    bb_i = coef_r * bim_ref[0] + coef_i * bre_ref[0]
    arr, air = arr_ref[0], air_ref[0]
    magr = jnp.exp(arr * dt)
    lrr, lir = magr * jnp.cos(air * dt), magr * jnp.sin(air * dt)

    lane_t = lax.broadcasted_iota(jnp.int32, (S5_STATE, CHUNK_W), 1) // S5_GROUP_CH
    lane16 = lax.broadcasted_iota(jnp.int32, (S5_GROUP_CH, CHUNK_W), 1)
    row16 = lax.broadcasted_iota(jnp.int32, (S5_GROUP_CH, CHUNK_W), 0)
    lo128 = lax.broadcasted_iota(jnp.int32, (S5_GROUP_CH, 2 * S5_STATE), 1) < S5_STATE
    c_cat = jnp.where(lo128, cre2_ref[0], -cim2_ref[0])
    d_col = d_ref[0]

    expo = (CHUNK - 1) - lane_t
    pw_r = jnp.ones((S5_STATE, CHUNK_W), F32)
    pw_i = jnp.zeros((S5_STATE, CHUNK_W), F32)
    sq_r, sq_i = lr, li
    for bit in range(CHUNK.bit_length() - 1):
        take = ((expo >> bit) & 1) == 1
        pw_r, pw_i = (jnp.where(take, pw_r * sq_r - pw_i * sq_i, pw_r),
                      jnp.where(take, pw_r * sq_i + pw_i * sq_r, pw_i))
        sq_r, sq_i = sq_r * sq_r - sq_i * sq_i, 2.0 * sq_r * sq_i
    a16r_ref[0] = jnp.broadcast_to(sq_r, (S5_STATE, LANES))
    a16i_ref[0] = jnp.broadcast_to(sq_i, (S5_STATE, LANES))
    m_r = pw_r * bb_r - pw_i * bb_i
    m_i = pw_r * bb_i + pw_i * bb_r
    tr_ref[0, CHUNK_W:CHUNK_W + S5_STATE, :] = m_r.astype(BF16)
    tr_ref[0, CHUNK_W + S5_STATE:CHUNK_W + 2 * S5_STATE, :] = m_i.astype(BF16)
    k_all = jnp.dot(c_cat, jnp.concatenate([m_r, m_i], axis=0), preferred_element_type=F32, precision=HI)
    w_r, w_i = cre2_ref[0], cim2_ref[0]
    for tq in range(CHUNK):
        shift = (CHUNK - 1 - tq) * S5_GROUP_CH
        t_rows = k_all if shift == 0 else pltpu.roll(k_all, CHUNK_W - shift, axis=1)
        t_rows = (jnp.where(lane16 < (tq + 1) * S5_GROUP_CH, t_rows, 0.0)
                  + jnp.where(lane16 == tq * S5_GROUP_CH + row16, d_col, 0.0))
        tr_ref[0, tq * S5_GROUP_CH:(tq + 1) * S5_GROUP_CH, :] = t_rows.astype(BF16)
        w_r, w_i = w_r * lrr - w_i * lir, w_r * lir + w_i * lrr
        o_ref[0, tq * S5_GROUP_CH:(tq + 1) * S5_GROUP_CH, :] = jnp.where(lo128, w_r, -w_i).astype(BF16)


def _s5_params(s5_log_dt, s5_a_re, s5_a_im, s5_b_re, s5_b_im, s5_c_re, s5_c_im, s5_d):
    g, p, c = S5_GROUPS, S5_STATE, S5_GROUP_CH
    args = (
        s5_log_dt.reshape(g, 1, 1),
        s5_a_re.reshape(g, p, 1), s5_a_im.reshape(g, p, 1),
        jnp.tile(s5_a_re.reshape(g, 1, p), (1, 1, 2)), jnp.tile(s5_a_im.reshape(g, 1, p), (1, 1, 2)),
        jnp.tile(s5_b_re, (1, 1, CHUNK)), jnp.tile(s5_b_im, (1, 1, CHUNK)),
        jnp.tile(s5_c_re, (1, 1, 2)), jnp.tile(s5_c_im, (1, 1, 2)),
        s5_d.reshape(g, c, 1),
    )
    in_specs = [pl.BlockSpec((1,) + a.shape[1:], lambda i: (i, 0, 0)) for a in args]
    out_shape = (
        jax.ShapeDtypeStruct((g, CHUNK_W + 2 * p, CHUNK_W), BF16),
        jax.ShapeDtypeStruct((g, CHUNK_W, 2 * p), BF16),
        jax.ShapeDtypeStruct((g, p, LANES), F32),
        jax.ShapeDtypeStruct((g, p, LANES), F32),
    )
    out_specs = tuple(pl.BlockSpec((1,) + s.shape[1:], lambda i: (i, 0, 0)) for s in out_shape)
    return pl.pallas_call(
        _s5_params_kernel, grid=(g,), in_specs=in_specs, out_specs=out_specs, out_shape=out_shape,
        compiler_params=_cparams(1), name="s5_params")(*args)


def _in_proj_kernel(x_ref, g_ref, wqk_ref, wvt_ref, qg_ref, kg_ref, q_ref, k_ref, vt_ref):
    halves = _row_halves(x_ref.shape[1])
    hn = [_rms(x_ref[0, r, :], g_ref[...]).astype(BF16) for r in halves]
    qk = [_dot(v, wqk_ref[...]) for v in hn]
    vt = [_dot_nt(wvt_ref[...], v) for v in hn]
    for r, v in zip(halves, qk):
        q = _head64_norm(v[:, :FOX_WIDTH]) * (qg_ref[...] * (HEAD_DIM ** -0.5 * LOG2E))
        k = _head64_norm(v[:, FOX_WIDTH:]) * kg_ref[...]
        q_ref[0, r, :] = q.astype(BF16)
        k_ref[0, r, :] = k.astype(BF16)
    for r, v in zip(halves, vt):
        vt_ref[0, :, r] = v.astype(BF16)


def _in_proj(x, norm_mix, w_qk, w_vt, q_gain, k_gain):
    b, l, d = x.shape
    tm = TOK_TILE
    tok = lambda w: pl.BlockSpec((1, tm, w), lambda i, j: (i, j, 0))
    full = lambda a: pl.BlockSpec(a.shape, lambda i, j: (0,) * a.ndim)
    return pl.pallas_call(
        _in_proj_kernel, grid=(b, l // tm),
        in_specs=[tok(d), full(norm_mix), full(w_qk), full(w_vt), full(q_gain), full(k_gain)],
        out_specs=(tok(FOX_WIDTH), tok(FOX_WIDTH), pl.BlockSpec((1, FOX_WIDTH, tm), lambda i, j: (i, 0, j))),
        out_shape=(jax.ShapeDtypeStruct((b, l, FOX_WIDTH), BF16),) * 2
        + (jax.ShapeDtypeStruct((b, FOX_WIDTH, l), BF16),),
        compiler_params=_cparams(2), name="in_proj")(x, norm_mix, w_qk, w_vt, q_gain, k_gain)


def _pack3(v):
    hi = v.astype(BF16).astype(F32)
    rest = v - hi
    mid = rest.astype(BF16).astype(F32)
    lo = rest - mid
    return (hi + pltpu.roll(mid, N_FOX_HEADS, axis=1) + pltpu.roll(lo, 2 * N_FOX_HEADS, axis=1)).astype(BF16)


def _u_gate_kernel(x_ref, g_ref, wuf_ref, bias_ref, place_ref, xt_ref, cs_ref, z_ref, f_ref):
    seq = x_ref.shape[1]
    nck = seq // CHUNK
    n_slab = S5_WIDTH // LANES
    grp_per_slab = LANES // S5_GROUP_CH
    for r in range(seq // TOK_TILE):
        rows = slice(r * TOK_TILE, (r + 1) * TOK_TILE)
        hn = _rms(x_ref[0, rows, :], g_ref[...]).astype(BF16)
        z = _dot(hn, wuf_ref[...])
        for j in range(n_slab):
            z_ref[j, rows, :] = z[:, j * LANES:(j + 1) * LANES]
        f_ref[rows, :] = z[:, S5_WIDTH:]
    for t in range(CHUNK):
        for j in range(n_slab):
            blk = z_ref[j, pl.ds(t, nck, stride=CHUNK), :]
            xt_ref[j * grp_per_slab:(j + 1) * grp_per_slab, t * S5_GROUP_CH:(t + 1) * S5_GROUP_CH, :] = (
                blk.T.reshape(grp_per_slab, S5_GROUP_CH, nck).astype(BF16))

    head_lane = lax.broadcasted_iota(jnp.int32, (LANES, LANES), 1) < N_FOX_HEADS
    lower = (lax.broadcasted_iota(jnp.int32, (LANES, LANES), 0)
             >= lax.broadcasted_iota(jnp.int32, (LANES, LANES), 1)).astype(BF16)
    local = []
    for blk in range(seq // LANES):
        rows = slice(blk * LANES, (blk + 1) * LANES)
        log2f = jnp.where(head_lane, _log_sigmoid(f_ref[rows, :] + bias_ref[...]) * LOG2E, 0.0)
        part = _dot(lower, _pack3(log2f))
        c = part + pltpu.roll(part, LANES - N_FOX_HEADS, axis=1) + pltpu.roll(part, LANES - 2 * N_FOX_HEADS, axis=1)
        local.append(jnp.where(head_lane, c, 0.0))
    carry = jnp.zeros((1, LANES), F32)
    for blk in range(seq // LANES):
        c = local[blk] + carry
        carry = c[LANES - 1:LANES, :]
        cs_ref[0, blk * LANES:(blk + 1) * LANES, :] = _dot(_pack3(c), place_ref[...]).astype(BF16)


def _u_gate(x, norm_mix, w_uf, bias_row, place):
    b, l, d = x.shape
    nck = l // CHUNK
    full = lambda a: pl.BlockSpec(a.shape, lambda i: (0,) * a.ndim)
    return pl.pallas_call(
        _u_gate_kernel, grid=(b,),
        in_specs=[pl.BlockSpec((1, l, d), lambda i: (i, 0, 0)), full(norm_mix), full(w_uf), full(bias_row),
                  full(place)],
        out_specs=(pl.BlockSpec((S5_GROUPS, CHUNK_W, nck), lambda i: (0, 0, i)),
                   pl.BlockSpec((1, l, FOX_WIDTH), lambda i: (i, 0, 0))),
        out_shape=(jax.ShapeDtypeStruct((S5_GROUPS, CHUNK_W, b * nck), BF16),
                   jax.ShapeDtypeStruct((b, l, FOX_WIDTH), BF16)),
        scratch_shapes=[pltpu.VMEM((S5_WIDTH // LANES, l, LANES), F32), pltpu.VMEM((l, LANES), F32)],
        compiler_params=_cparams(1), name="u_gate")(x, norm_mix, w_uf, bias_row, place)


def _fox_kernel(q_ref, k_ref, cs_ref, vt_ref, o_ref, kaug_ref, s_ref):
    blk = ATT_BLOCK
    seq = q_ref.shape[1]
    lane_k = lax.broadcasted_iota(jnp.int32, (seq, LANES), 1)
    k2 = k_ref[0]
    cs = cs_ref[0]
    kaug_ref[0] = jnp.where(lane_k < HEAD_DIM, k2, cs)
    kaug_ref[1] = jnp.where(lane_k >= HEAD_DIM, k2, cs)

    lane_q = lax.broadcasted_iota(jnp.int32, (blk, LANES), 1)
    minus1 = [jnp.where((lane_q >= HEAD_DIM) & (lane_q < HEAD_DIM + 3), -1.0, 0.0).astype(BF16),
              jnp.where(lane_q < 3, -1.0, 0.0).astype(BF16)]
    own = [lane_q < HEAD_DIM, lane_q >= HEAD_DIM]
    key_pos = lax.broadcasted_iota(jnp.int32, (blk, blk), 0)
    qry_pos = lax.broadcasted_iota(jnp.int32, (blk, blk), 1)

    def scores(i):
        q2 = q_ref[0, i * blk:(i + 1) * blk, :]
        for hh in range(2):
            qa = jnp.where(own[hh], q2, minus1[hh])
            st = _dot_nt(kaug_ref[hh, 0:(i + 1) * blk, :], qa)
            if i > 0:
                s_ref[i % 2, hh, 0:i * blk, :] = st[:i * blk]
            s_ref[i % 2, hh, i * blk:(i + 1) * blk, :] = jnp.where(key_pos > qry_pos, NEG_BIG, st[i * blk:])

    def attend(i):
        keys = (i + 1) * blk
        outs = []
        for hh in range(2):
            st = s_ref[i % 2, hh, 0:keys, :]
            p = jnp.exp2(st - jnp.max(st, axis=0, keepdims=True))
            l = jnp.sum(p, axis=0, keepdims=True)
            acc = _dot(vt_ref[0, hh * HEAD_DIM:(hh + 1) * HEAD_DIM, 0:keys], p.astype(BF16))
            outs.append(acc / l)
        o_ref[0, i * blk:(i + 1) * blk, :] = jnp.concatenate(outs, axis=0).T.astype(BF16)

    n_blk = seq // blk
    scores(0)
    for i in range(n_blk):
        if i + 1 < n_blk:
            scores(i + 1)
        attend(i)


def _fox_attn(q, k, cs, vt):
    b, l, _ = q.shape
    n_pairs = FOX_WIDTH // LANES
    tok = pl.BlockSpec((1, l, LANES), lambda i, j: (i, 0, j))
    return pl.pallas_call(
        _fox_kernel, grid=(b, n_pairs),
        in_specs=[tok, tok, tok, pl.BlockSpec((1, LANES, l), lambda i, j: (i, j, 0))],
        out_specs=tok,
        out_shape=jax.ShapeDtypeStruct((b, l, FOX_WIDTH), BF16),
        scratch_shapes=[pltpu.VMEM((2, l, LANES), BF16), pltpu.VMEM((2, 2, l, ATT_BLOCK), F32)],
        compiler_params=_cparams(2), name="fox_attn")(q, k, cs, vt)


def _s5_chunk_kernel(xt_ref, tr_ref, o_ref, a16r_ref, a16i_ref, yg_ref, *, chunks_per_seq):
    lane_tile = min(4 * LANES, xt_ref.shape[2])
    n_tiles = xt_ref.shape[2] // lane_tile
    n_levels = chunks_per_seq.bit_length() - 1
    reps = lane_tile // LANES
    kpos = lax.broadcasted_iota(jnp.int32, (S5_STATE, LANES), 1) % chunks_per_seq
    a_r, a_i = a16r_ref[0], a16i_ref[0]
    coef = []
    for lvl in range(n_levels):
        keep = kpos >= (1 << lvl)
        coef.append((jnp.concatenate([jnp.where(keep, a_r, 0.0)] * reps, axis=1),
                     jnp.concatenate([jnp.where(keep, a_i, 0.0)] * reps, axis=1)))
        a_r, a_i = a_r * a_r - a_i * a_i, 2.0 * a_r * a_i
    first = jnp.concatenate([kpos] * reps, axis=1) == 0
    stage1 = [_dot(tr_ref[0], xt_ref[0, :, n * lane_tile:(n + 1) * lane_tile]) for n in range(n_tiles)]
    for n in range(n_tiles):
        yr = stage1[n]
        s_r, s_i = yr[CHUNK_W:CHUNK_W + S5_STATE], yr[CHUNK_W + S5_STATE:]
        for lvl in range(n_levels):
            c_r, c_i = coef[lvl]
            p_r = pltpu.roll(s_r, 1 << lvl, axis=1)
            p_i = pltpu.roll(s_i, 1 << lvl, axis=1)
            s_r, s_i = s_r + (c_r * p_r - c_i * p_i), s_i + (c_r * p_i + c_i * p_r)
        e_r = jnp.where(first, 0.0, pltpu.roll(s_r, 1, axis=1))
        e_i = jnp.where(first, 0.0, pltpu.roll(s_i, 1, axis=1))
        s_in = jnp.concatenate([e_r, e_i], axis=0).astype(BF16)
        y = yr[:CHUNK_W] + _dot(o_ref[0], s_in)
        yg = _gelu_tanh(y).astype(BF16)
        yg_ref[:, :, n * lane_tile:(n + 1) * lane_tile] = yg.reshape(CHUNK, S5_GROUP_CH, lane_tile)


def _s5_chunk(xt, tr_op, o_op, a16r, a16i, chunks_per_seq):
    g, _, nb = xt.shape
    per_g = lambda a: pl.BlockSpec((1,) + a.shape[1:], lambda i: (i, 0, 0))
    return pl.pallas_call(
        functools.partial(_s5_chunk_kernel, chunks_per_seq=chunks_per_seq), grid=(g,),
        in_specs=[per_g(xt), per_g(tr_op), per_g(o_op), per_g(a16r), per_g(a16i)],
        out_specs=pl.BlockSpec((CHUNK, S5_GROUP_CH, nb), lambda i: (0, i, 0)),
        out_shape=jax.ShapeDtypeStruct((CHUNK, S5_WIDTH, nb), BF16),
        compiler_params=_cparams(1), name="s5_chunk")(xt, tr_op, o_op, a16r, a16i)


def _s5_post_kernel(yg_ref, wgt_ref, bg_ref, gn_ref, o_ref):
    for t in range(CHUNK):
        y = yg_ref[t]
        gate = _dot(wgt_ref[...], y) + bg_ref[...]
        z = y.astype(F32) * jax.nn.sigmoid(gate)
        zn = z * lax.rsqrt(jnp.mean(z * z, axis=0, keepdims=True) + EPS) * gn_ref[...]
        o_ref[0, :, t, :] = zn.T.astype(BF16)


def _s5_post(yg, w_glu_t, b_glu_col, gain_col, batch):
    _, _, nb = yg.shape
    nck = nb // batch
    full = lambda a: pl.BlockSpec(a.shape, lambda i: (0,) * a.ndim)
    return pl.pallas_call(
        _s5_post_kernel, grid=(batch,),
        in_specs=[pl.BlockSpec((CHUNK, S5_WIDTH, nck), lambda i: (0, 0, i)),
                  full(w_glu_t), full(b_glu_col), full(gain_col)],
        out_specs=pl.BlockSpec((1, nck, CHUNK, S5_WIDTH), lambda i: (i, 0, 0, 0)),
        out_shape=jax.ShapeDtypeStruct((batch, nck, CHUNK, S5_WIDTH), BF16),
        compiler_params=_cparams(1), name="s5_post")(yg, w_glu_t, b_glu_col, gain_col)


def _mix_out_kernel(x_ref, fox_ref, s5_ref, gf_ref, wo_ref, gc_ref, wxq_ref, gq_ref, h_ref, xq_ref):
    halves = _row_halves(x_ref.shape[1])
    mixed = [jnp.concatenate([_rms(fox_ref[0, r, :].astype(F32), gf_ref[...]).astype(BF16), s5_ref[0, r, :]],
                             axis=-1) for r in halves]
    proj = [_dot(m, wo_ref[...]) for m in mixed]
    hn = []
    for r, pr in zip(halves, proj):
        h = x_ref[0, r, :] + pr
        h_ref[0, r, :] = h
        hn.append(_rms(h, gc_ref[...]).astype(BF16))
    xq = [_dot(v, wxq_ref[...]) for v in hn]
    for r, v in zip(halves, xq):
        xq_ref[0, r, :] = (_head256_norm(v) * (gq_ref[...] * X_HEAD_DIM ** -0.5)).astype(BF16)


def _mix_out(x, fox, s5n, g_fox, w_out, g_cross, w_xq, g_xq):
    b, l, d = x.shape
    tm = TOK_TILE
    tok = lambda w: pl.BlockSpec((1, tm, w), lambda i, j: (i, j, 0))
    full = lambda a: pl.BlockSpec(a.shape, lambda i, j: (0,) * a.ndim)
    return pl.pallas_call(
        _mix_out_kernel, grid=(b, l // tm),
        in_specs=[tok(d), tok(FOX_WIDTH), tok(S5_WIDTH), full(g_fox), full(w_out), full(g_cross),
                  full(w_xq), full(g_xq)],
        out_specs=(tok(d), tok(d)),
        out_shape=(jax.ShapeDtypeStruct((b, l, d), F32), jax.ShapeDtypeStruct((b, l, d), BF16)),
        compiler_params=_cparams(2), name="mix_out")(x, fox, s5n, g_fox, w_out, g_cross, w_xq, g_xq)


def _mem_kv_kernel(mem_ref, g_ref, wkv_ref, gk_ref, xk_ref, xv_ref):
    mn = _rms(mem_ref[0], g_ref[...]).astype(BF16)
    kv = _dot(mn, wkv_ref[...])
    xk_ref[0] = (_head256_norm(kv[:, :D_MODEL]) * gk_ref[...]).astype(BF16)
    xv_ref[0] = kv[:, D_MODEL:].astype(BF16)


def _mem_kv(mem, g_mem, w_xkv, g_xk):
    b, n, d = mem.shape
    full = lambda a: pl.BlockSpec(a.shape, lambda i: (0,) * a.ndim)
    blk = pl.BlockSpec((1, n, d), lambda i: (i, 0, 0))
    return pl.pallas_call(
        _mem_kv_kernel, grid=(b,),
        in_specs=[blk, full(g_mem), full(w_xkv), full(g_xk)],
        out_specs=(blk, blk),
        out_shape=(jax.ShapeDtypeStruct((b, n, d), BF16),) * 2,
        compiler_params=_cparams(1), name="mem_kv")(mem, g_mem, w_xkv, g_xk)


def _cross_attn_kernel(h_ref, xq_ref, xk_ref, xv_ref, wxo_ref, o_ref):
    heads = [slice(hd * X_HEAD_DIM, (hd + 1) * X_HEAD_DIM) for hd in range(N_X_HEADS)]
    halves = _row_halves(h_ref.shape[1])
    scores = [[_dot_nt(xq_ref[0, r, sl], xk_ref[0, :, sl]) for sl in heads] for r in halves]
    xo = []
    for sc in scores:
        outs = []
        for s, sl in zip(sc, heads):
            p = jnp.exp(s - jnp.max(s, axis=-1, keepdims=True))
            p = p / jnp.sum(p, axis=-1, keepdims=True)
            outs.append(_dot(p.astype(BF16), xv_ref[0, :, sl]).astype(BF16))
        xo.append(jnp.concatenate(outs, axis=-1))
    for r, v in zip(halves, xo):
        o_ref[0, r, :] = h_ref[0, r, :] + _dot(v, wxo_ref[...])


def _cross_attn(h, xq, xk, xv, w_xo):
    b, l, d = h.shape
    tm = TOK_TILE
    tok = pl.BlockSpec((1, tm, d), lambda i, j: (i, j, 0))
    mem = pl.BlockSpec((1, N_MEM, d), lambda i, j: (i, 0, 0))
    return pl.pallas_call(
        _cross_attn_kernel, grid=(b, l // tm),
        in_specs=[tok, tok, mem, mem, pl.BlockSpec(w_xo.shape, lambda i, j: (0, 0))],
        out_specs=tok,
        out_shape=jax.ShapeDtypeStruct((b, l, d), F32),
        compiler_params=_cparams(2), name="cross_attn")(h, xq, xk, xv, w_xo)


def _conv_ffn_kernel(h_ref, g_ref, wg_ref, wu_ref, cw_ref, cb_ref, wd_ref, o_ref, tail_ref, act_ref):
    @pl.when(pl.program_id(1) == 0)
    def _():
        tail_ref[...] = jnp.zeros_like(tail_ref)

    h = h_ref[0]
    tm = h.shape[0]
    hn = _rms(h, g_ref[...]).astype(BF16)
    row = lax.broadcasted_iota(jnp.int32, (tm, FF_CHUNK), 0)
    for c in range(D_FF // FF_CHUNK):
        sl = slice(c * FF_CHUNK, (c + 1) * FF_CHUNK)
        gate = _dot(hn, wg_ref[:, sl])
        up = _dot(hn, wu_ref[:, sl])
        prev1 = tail_ref[1:2, sl]
        prev2 = tail_ref[0:1, sl]
        g1 = jnp.where(row == 0, prev1, pltpu.roll(gate, 1, axis=0))
        g2 = jnp.where(row == 0, prev2, jnp.where(row == 1, prev1, pltpu.roll(gate, 2, axis=0)))
        conv = cb_ref[:, sl] + cw_ref[0:1, sl] * g2 + cw_ref[1:2, sl] * g1 + cw_ref[2:3, sl] * gate
        tail_ref[0:2, sl] = gate[tm - 2:tm, :]
        act_ref[:, sl] = (conv * jax.nn.sigmoid(conv) * up).astype(BF16)
    o_ref[0] = h + _dot(act_ref[...], wd_ref[...])


def _conv_ffn(h, g_ffn, w_gate, w_up, conv_w, conv_b, w_down):
    b, l, d = h.shape
    tm = TOK_TILE
    tok = pl.BlockSpec((1, tm, d), lambda i, j: (i, j, 0))
    full = lambda a: pl.BlockSpec(a.shape, lambda i, j: (0,) * a.ndim)
    return pl.pallas_call(
        _conv_ffn_kernel, grid=(b, l // tm),
        in_specs=[tok, full(g_ffn), full(w_gate), full(w_up), full(conv_w), full(conv_b), full(w_down)],
        out_specs=tok,
        out_shape=jax.ShapeDtypeStruct((b, l, d), F32),
        scratch_shapes=[pltpu.VMEM((8, D_FF), F32), pltpu.VMEM((tm, D_FF), BF16)],
        compiler_params=_cparams(2), name="conv_ffn")(h, g_ffn, w_gate, w_up, conv_w, conv_b, w_down)


def _gate_placement():
    place = np.zeros((LANES, FOX_WIDTH), np.float32)
    for h in range(N_FOX_HEADS):
        base = (h // 2) * LANES + (HEAD_DIM if h % 2 == 0 else 0)
        for term in range(3):
            place[term * N_FOX_HEADS + h, base + term] = 1.0
    return jnp.asarray(place, dtype=BF16)


def _layer(h, mem, p):
    b, l, d = h.shape
    nck = l // CHUNK
    row = lambda a: a.reshape(1, -1)
    col = lambda a: a.reshape(-1, 1)
    w_in = p["w_in"]
    qk_end, v_end = 2 * FOX_WIDTH, 3 * FOX_WIDTH
    w_qk = w_in[:, :qk_end].astype(BF16)
    w_vt = w_in[:, qk_end:v_end].T.astype(BF16)
    w_uf = jnp.concatenate(
        [w_in[:, v_end + N_FOX_HEADS:],
         jnp.pad(w_in[:, v_end:v_end + N_FOX_HEADS], ((0, 0), (0, LANES - N_FOX_HEADS)))], axis=1).astype(BF16)
    f_bias = jnp.pad(p["fox_f_bias"], (0, LANES - N_FOX_HEADS)).reshape(1, LANES)

    tr_op, o_op, a16r, a16i = _s5_params(
        p["s5_log_dt"], p["s5_a_re"], p["s5_a_im"], p["s5_b_re"], p["s5_b_im"],
        p["s5_c_re"], p["s5_c_im"], p["s5_d"])

    q, k, vt = _in_proj(h, row(p["norm_mix"]), w_qk, w_vt,
                        row(jnp.tile(p["fox_q_norm"], N_FOX_HEADS)),
                        row(jnp.tile(p["fox_k_norm"], N_FOX_HEADS)))
    xt, cs = _u_gate(h, row(p["norm_mix"]), w_uf, f_bias, _gate_placement())
    fox = _fox_attn(q, k, cs, vt)

    yg = _s5_chunk(xt, tr_op, o_op, a16r, a16i, nck)
    lane_bcast = lambda a: jnp.broadcast_to(col(a), (a.shape[0], LANES))
    s5n = _s5_post(yg, p["s5_w_glu"].T.astype(BF16), lane_bcast(p["s5_b_glu"]), lane_bcast(p["out_norm_s5"]), b)
    s5n = s5n.reshape(b, l, S5_WIDTH)

    h1, xq = _mix_out(h, fox, s5n, row(p["out_norm_fox"]), p["w_out"].astype(BF16),
                      row(p["norm_cross"]), p["w_xq"].astype(BF16),
                      row(jnp.tile(p["xq_norm"], N_X_HEADS)))
    xk, xv = _mem_kv(mem, row(p["norm_mem"]), p["w_xkv"].astype(BF16),
                     row(jnp.tile(p["xk_norm"], N_X_HEADS)))
    h2 = _cross_attn(h1, xq, xk, xv, p["w_xo"].astype(BF16))

    w_up = p["w_ffn_up"]
    return _conv_ffn(h2, row(p["norm_ffn"]), w_up[:, :D_FF].astype(BF16), w_up[:, D_FF:].astype(BF16),
                     p["ffn_conv_w"], row(p["ffn_conv_b"]), p["w_ffn_down"].astype(BF16))


_PARAM_NAMES = (
    "norm_mix", "w_in", "fox_q_norm", "fox_k_norm", "fox_f_bias", "s5_a_re", "s5_a_im", "s5_log_dt",
    "s5_b_re", "s5_b_im", "s5_c_re", "s5_c_im", "s5_d", "s5_w_glu", "s5_b_glu", "out_norm_fox",
    "out_norm_s5", "w_out", "norm_cross", "norm_mem", "w_xq", "w_xkv", "xq_norm", "xk_norm", "w_xo",
    "norm_ffn", "w_ffn_up", "ffn_conv_w", "ffn_conv_b", "w_ffn_down")


def kernel(x, mem, norm_mix, w_in, fox_q_norm, fox_k_norm, fox_f_bias, s5_a_re, s5_a_im, s5_log_dt, s5_b_re, s5_b_im, s5_c_re, s5_c_im, s5_d, s5_w_glu, s5_b_glu, out_norm_fox, out_norm_s5, w_out, norm_cross, norm_mem, w_xq, w_xkv, xq_norm, xk_norm, w_xo, norm_ffn, w_ffn_up, ffn_conv_w, ffn_conv_b, w_ffn_down):
    stacked = (norm_mix, w_in, fox_q_norm, fox_k_norm, fox_f_bias, s5_a_re, s5_a_im, s5_log_dt,
               s5_b_re, s5_b_im, s5_c_re, s5_c_im, s5_d, s5_w_glu, s5_b_glu, out_norm_fox,
               out_norm_s5, w_out, norm_cross, norm_mem, w_xq, w_xkv, xq_norm, xk_norm, w_xo,
               norm_ffn, w_ffn_up, ffn_conv_w, ffn_conv_b, w_ffn_down)
    h = x
    for layer in range(norm_mix.shape[0]):
        h = _layer(h, mem, {n: a[layer] for n, a in zip(_PARAM_NAMES, stacked)})
    return h
```

```python
import functools
import math

import jax
import jax.numpy as jnp
import numpy as np
from jax import lax
from jax.experimental import pallas as pl
from jax.experimental.pallas import tpu as pltpu

F32 = jnp.float32
BF16 = jnp.bfloat16

D_MODEL = 1024
N_MEM = 256
FOX_WIDTH = 512
HEAD_DIM = 64
N_FOX_HEADS = 8
S5_WIDTH = 512
S5_GROUP_CH = 16
S5_GROUPS = 32
S5_STATE = 64
N_X_HEADS = 4
X_HEAD_DIM = 256
D_FF = 2816
CONV_W = 3
EPS = 1e-6

LANES = 128
CHUNK = 16
CHUNK_W = CHUNK * S5_GROUP_CH
TOK_TILE = 512
ATT_BLOCK = 256
FF_CHUNK = 256
NEG_BIG = -1e30
LOG2E = math.log2(math.e)
HI = lax.Precision.HIGHEST
VMEM_LIMIT = 56 * 1024 * 1024

NT_DIMS = (((1,), (1,)), ((), ()))


def _cparams(n_axes):
    return pltpu.CompilerParams(dimension_semantics=("arbitrary",) * n_axes,
                                vmem_limit_bytes=VMEM_LIMIT)


def _rms(xf, g):
    return xf * lax.rsqrt(jnp.mean(xf * xf, axis=-1, keepdims=True) + EPS) * g


def _dot(a, b):
    return jnp.dot(a, b, preferred_element_type=F32)


def _dot_nt(a, b):
    return lax.dot_general(a, b, NT_DIMS, preferred_element_type=F32)


def _head64_norm(x):
    lo = lax.broadcasted_iota(jnp.int32, (x.shape[0], LANES), 1) < HEAD_DIM
    outs = []
    for j in range(x.shape[1] // LANES):
        blk = x[:, j * LANES:(j + 1) * LANES]
        sq = blk * blk
        s_lo = jnp.sum(jnp.where(lo, sq, 0.0), axis=-1, keepdims=True)
        s_hi = jnp.sum(jnp.where(lo, 0.0, sq), axis=-1, keepdims=True)
        inv = jnp.where(lo, lax.rsqrt(s_lo * (1.0 / HEAD_DIM) + EPS),
                        lax.rsqrt(s_hi * (1.0 / HEAD_DIM) + EPS))
        outs.append(blk * inv)
    return jnp.concatenate(outs, axis=-1)


def _head256_norm(x):
    outs = []
    for h in range(x.shape[1] // X_HEAD_DIM):
        blk = x[:, h * X_HEAD_DIM:(h + 1) * X_HEAD_DIM]
        outs.append(blk * lax.rsqrt(jnp.mean(blk * blk, axis=-1, keepdims=True) + EPS))
    return jnp.concatenate(outs, axis=-1)


def _row_halves(rows):
    return [slice(0, rows // 2), slice(rows // 2, rows)]


def _log_sigmoid(z):
    return jnp.minimum(z, 0.0) - jnp.log(1.0 + jnp.exp(-jnp.abs(z)))


def _gelu_tanh(x):
    return 0.5 * x * (1.0 + jnp.tanh(math.sqrt(2.0 / math.pi) * (x + 0.044715 * (x * x * x))))


def _s5_params_kernel(ldt_ref, arc_ref, aic_ref, arr_ref, air_ref, bre_ref, bim_ref,
                      cre2_ref, cim2_ref, d_ref,
                      tr_ref, o_ref, a16r_ref, a16i_ref):
    dt = jnp.exp(ldt_ref[0])
    ar, ai = arc_ref[0], aic_ref[0]
    mag = jnp.exp(ar * dt)
    lr, li = mag * jnp.cos(ai * dt), mag * jnp.sin(ai * dt)
    den = ar * ar + ai * ai
    nr = lr - 1.0
    coef_r = (nr * ar + li * ai) / den
    coef_i = (li * ar - nr * ai) / den
    bb_r = coef_r * bre_ref[0] - coef_i * bim_ref[0]
    bb_i = coef_r * bim_ref[0] + coef_i * bre_ref[0]
    arr, air = arr_ref[0], air_ref[0]
    magr = jnp.exp(arr * dt)
    lrr, lir = magr * jnp.cos(air * dt), magr * jnp.sin(air * dt)

    lane_t = lax.broadcasted_iota(jnp.int32, (S5_STATE, CHUNK_W), 1) // S5_GROUP_CH
    lane16 = lax.broadcasted_iota(jnp.int32, (S5_GROUP_CH, CHUNK_W), 1)
    row16 = lax.broadcasted_iota(jnp.int32, (S5_GROUP_CH, CHUNK_W), 0)
    lo128 = lax.broadcasted_iota(jnp.int32, (S5_GROUP_CH, 2 * S5_STATE), 1) < S5_STATE
    c_cat = jnp.where(lo128, cre2_ref[0], -cim2_ref[0])
    d_col = d_ref[0]

    expo = (CHUNK - 1) - lane_t
    pw_r = jnp.ones((S5_STATE, CHUNK_W), F32)
    pw_i = jnp.zeros((S5_STATE, CHUNK_W), F32)
    sq_r, sq_i = lr, li
    for bit in range(CHUNK.bit_length() - 1):
        take = ((expo >> bit) & 1) == 1
        pw_r, pw_i = (jnp.where(take, pw_r * sq_r - pw_i * sq_i, pw_r),
                      jnp.where(take, pw_r * sq_i + pw_i * sq_r, pw_i))
        sq_r, sq_i = sq_r * sq_r - sq_i * sq_i, 2.0 * sq_r * sq_i
    a16r_ref[0] = jnp.broadcast_to(sq_r, (S5_STATE, LANES))
    a16i_ref[0] = jnp.broadcast_to(sq_i, (S5_STATE, LANES))
    m_r = pw_r * bb_r - pw_i * bb_i
    m_i = pw_r * bb_i + pw_i * bb_r
    tr_ref[0, CHUNK_W:CHUNK_W + S5_STATE, :] = m_r.astype(BF16)
    tr_ref[0, CHUNK_W + S5_STATE:CHUNK_W + 2 * S5_STATE, :] = m_i.astype(BF16)
    k_all = jnp.dot(c_cat, jnp.concatenate([m_r, m_i], axis=0), preferred_element_type=F32, precision=HI)
    w_r, w_i = cre2_ref[0], cim2_ref[0]
    for tq in range(CHUNK):
        shift = (CHUNK - 1 - tq) * S5_GROUP_CH
        t_rows = k_all if shift == 0 else pltpu.roll(k_all, CHUNK_W - shift, axis=1)
        t_rows = (jnp.where(lane16 < (tq + 1) * S5_GROUP_CH, t_rows, 0.0)
                  + jnp.where(lane16 == tq * S5_GROUP_CH + row16, d_col, 0.0))
        tr_ref[0, tq * S5_GROUP_CH:(tq + 1) * S5_GROUP_CH, :] = t_rows.astype(BF16)
        w_r, w_i = w_r * lrr - w_i * lir, w_r * lir + w_i * lrr
        o_ref[0, tq * S5_GROUP_CH:(tq + 1) * S5_GROUP_CH, :] = jnp.where(lo128, w_r, -w_i).astype(BF16)


def _s5_params(s5_log_dt, s5_a_re, s5_a_im, s5_b_re, s5_b_im, s5_c_re, s5_c_im, s5_d):
    g, p, c = S5_GROUPS, S5_STATE, S5_GROUP_CH
    args = (
        s5_log_dt.reshape(g, 1, 1),
        s5_a_re.reshape(g, p, 1), s5_a_im.reshape(g, p, 1),
        jnp.tile(s5_a_re.reshape(g, 1, p), (1, 1, 2)), jnp.tile(s5_a_im.reshape(g, 1, p), (1, 1, 2)),
        jnp.tile(s5_b_re, (1, 1, CHUNK)), jnp.tile(s5_b_im, (1, 1, CHUNK)),
        jnp.tile(s5_c_re, (1, 1, 2)), jnp.tile(s5_c_im, (1, 1, 2)),
        s5_d.reshape(g, c, 1),
    )
    in_specs = [pl.BlockSpec((1,) + a.shape[1:], lambda i: (i, 0, 0)) for a in args]
    out_shape = (
        jax.ShapeDtypeStruct((g, CHUNK_W + 2 * p, CHUNK_W), BF16),
        jax.ShapeDtypeStruct((g, CHUNK_W, 2 * p), BF16),
        jax.ShapeDtypeStruct((g, p, LANES), F32),
        jax.ShapeDtypeStruct((g, p, LANES), F32),
    )
    out_specs = tuple(pl.BlockSpec((1,) + s.shape[1:], lambda i: (i, 0, 0)) for s in out_shape)
    return pl.pallas_call(
        _s5_params_kernel, grid=(g,), in_specs=in_specs, out_specs=out_specs, out_shape=out_shape,
        compiler_params=_cparams(1), name="s5_params")(*args)


def _in_proj_kernel(x_ref, g_ref, wqk_ref, wvt_ref, qg_ref, kg_ref, q_ref, k_ref, vt_ref):
    halves = _row_halves(x_ref.shape[1])
    hn = [_rms(x_ref[0, r, :], g_ref[...]).astype(BF16) for r in halves]
    qk = [_dot(v, wqk_ref[...]) for v in hn]
    vt = [_dot_nt(wvt_ref[...], v) for v in hn]
    for r, v in zip(halves, qk):
        q = _head64_norm(v[:, :FOX_WIDTH]) * (qg_ref[...] * (HEAD_DIM ** -0.5 * LOG2E))
        k = _head64_norm(v[:, FOX_WIDTH:]) * kg_ref[...]
        q_ref[0, r, :] = q.astype(BF16)
        k_ref[0, r, :] = k.astype(BF16)
    for r, v in zip(halves, vt):
        vt_ref[0, :, r] = v.astype(BF16)


def _in_proj(x, norm_mix, w_qk, w_vt, q_gain, k_gain):
    b, l, d = x.shape
    tm = TOK_TILE
    tok = lambda w: pl.BlockSpec((1, tm, w), lambda i, j: (i, j, 0))
    full = lambda a: pl.BlockSpec(a.shape, lambda i, j: (0,) * a.ndim)
    return pl.pallas_call(
        _in_proj_kernel, grid=(b, l // tm),
        in_specs=[tok(d), full(norm_mix), full(w_qk), full(w_vt), full(q_gain), full(k_gain)],
        out_specs=(tok(FOX_WIDTH), tok(FOX_WIDTH), pl.BlockSpec((1, FOX_WIDTH, tm), lambda i, j: (i, 0, j))),
        out_shape=(jax.ShapeDtypeStruct((b, l, FOX_WIDTH), BF16),) * 2
        + (jax.ShapeDtypeStruct((b, FOX_WIDTH, l), BF16),),
        compiler_params=_cparams(2), name="in_proj")(x, norm_mix, w_qk, w_vt, q_gain, k_gain)


def _pack3(v):
    hi = v.astype(BF16).astype(F32)
    rest = v - hi
    mid = rest.astype(BF16).astype(F32)
    lo = rest - mid
    return (hi + pltpu.roll(mid, N_FOX_HEADS, axis=1) + pltpu.roll(lo, 2 * N_FOX_HEADS, axis=1)).astype(BF16)


def _u_gate_kernel(x_ref, g_ref, wuf_ref, bias_ref, place_ref, xt_ref, cs_ref, z_ref, f_ref):
    seq = x_ref.shape[1]
    nck = seq // CHUNK
    n_slab = S5_WIDTH // LANES
    grp_per_slab = LANES // S5_GROUP_CH
    for r in range(seq // TOK_TILE):
        rows = slice(r * TOK_TILE, (r + 1) * TOK_TILE)
        hn = _rms(x_ref[0, rows, :], g_ref[...]).astype(BF16)
        z = _dot(hn, wuf_ref[...])
        for j in range(n_slab):
            z_ref[j, rows, :] = z[:, j * LANES:(j + 1) * LANES]
        f_ref[rows, :] = z[:, S5_WIDTH:]
    for t in range(CHUNK):
        for j in range(n_slab):
            blk = z_ref[j, pl.ds(t, nck, stride=CHUNK), :]
            xt_ref[j * grp_per_slab:(j + 1) * grp_per_slab, t * S5_GROUP_CH:(t + 1) * S5_GROUP_CH, :] = (
                blk.T.reshape(grp_per_slab, S5_GROUP_CH, nck).astype(BF16))

    head_lane = lax.broadcasted_iota(jnp.int32, (LANES, LANES), 1) < N_FOX_HEADS
    lower = (lax.broadcasted_iota(jnp.int32, (LANES, LANES), 0)
             >= lax.broadcasted_iota(jnp.int32, (LANES, LANES), 1)).astype(BF16)
    local = []
    for blk in range(seq // LANES):
        rows = slice(blk * LANES, (blk + 1) * LANES)
        log2f = jnp.where(head_lane, _log_sigmoid(f_ref[rows, :] + bias_ref[...]) * LOG2E, 0.0)
        part = _dot(lower, _pack3(log2f))
        c = part + pltpu.roll(part, LANES - N_FOX_HEADS, axis=1) + pltpu.roll(part, LANES - 2 * N_FOX_HEADS, axis=1)
        local.append(jnp.where(head_lane, c, 0.0))
    carry = jnp.zeros((1, LANES), F32)
    for blk in range(seq // LANES):
        c = local[blk] + carry
        carry = c[LANES - 1:LANES, :]
        cs_ref[0, blk * LANES:(blk + 1) * LANES, :] = _dot(_pack3(c), place_ref[...]).astype(BF16)


def _u_gate(x, norm_mix, w_uf, bias_row, place):
    b, l, d = x.shape
    nck = l // CHUNK
    full = lambda a: pl.BlockSpec(a.shape, lambda i: (0,) * a.ndim)
    return pl.pallas_call(
        _u_gate_kernel, grid=(b,),
        in_specs=[pl.BlockSpec((1, l, d), lambda i: (i, 0, 0)), full(norm_mix), full(w_uf), full(bias_row),
                  full(place)],
        out_specs=(pl.BlockSpec((S5_GROUPS, CHUNK_W, nck), lambda i: (0, 0, i)),
                   pl.BlockSpec((1, l, FOX_WIDTH), lambda i: (i, 0, 0))),
        out_shape=(jax.ShapeDtypeStruct((S5_GROUPS, CHUNK_W, b * nck), BF16),
                   jax.ShapeDtypeStruct((b, l, FOX_WIDTH), BF16)),
        scratch_shapes=[pltpu.VMEM((S5_WIDTH // LANES, l, LANES), F32), pltpu.VMEM((l, LANES), F32)],
        compiler_params=_cparams(1), name="u_gate")(x, norm_mix, w_uf, bias_row, place)


def _fox_kernel(q_ref, k_ref, cs_ref, vt_ref, o_ref, kaug_ref, s_ref):
    blk = ATT_BLOCK
    seq = q_ref.shape[1]
    lane_k = lax.broadcasted_iota(jnp.int32, (seq, LANES), 1)
    k2 = k_ref[0]
    cs = cs_ref[0]
    kaug_ref[0] = jnp.where(lane_k < HEAD_DIM, k2, cs)
    kaug_ref[1] = jnp.where(lane_k >= HEAD_DIM, k2, cs)

    lane_q = lax.broadcasted_iota(jnp.int32, (blk, LANES), 1)
    minus1 = [jnp.where((lane_q >= HEAD_DIM) & (lane_q < HEAD_DIM + 3), -1.0, 0.0).astype(BF16),
              jnp.where(lane_q < 3, -1.0, 0.0).astype(BF16)]
    own = [lane_q < HEAD_DIM, lane_q >= HEAD_DIM]
    key_pos = lax.broadcasted_iota(jnp.int32, (blk, blk), 0)
    qry_pos = lax.broadcasted_iota(jnp.int32, (blk, blk), 1)

    def scores(i):
        q2 = q_ref[0, i * blk:(i + 1) * blk, :]
        for hh in range(2):
            qa = jnp.where(own[hh], q2, minus1[hh])
            st = _dot_nt(kaug_ref[hh, 0:(i + 1) * blk, :], qa)
            if i > 0:
                s_ref[i % 2, hh, 0:i * blk, :] = st[:i * blk]
            s_ref[i % 2, hh, i * blk:(i + 1) * blk, :] = jnp.where(key_pos > qry_pos, NEG_BIG, st[i * blk:])

    def attend(i):
        keys = (i + 1) * blk
        outs = []
        for hh in range(2):
            st = s_ref[i % 2, hh, 0:keys, :]
            p = jnp.exp2(st - jnp.max(st, axis=0, keepdims=True))
            l = jnp.sum(p, axis=0, keepdims=True)
            acc = _dot(vt_ref[0, hh * HEAD_DIM:(hh + 1) * HEAD_DIM, 0:keys], p.astype(BF16))
            outs.append(acc / l)
        o_ref[0, i * blk:(i + 1) * blk, :] = jnp.concatenate(outs, axis=0).T.astype(BF16)

    n_blk = seq // blk
    scores(0)
    for i in range(n_blk):
        if i + 1 < n_blk:
            scores(i + 1)
        attend(i)


def _fox_attn(q, k, cs, vt):
    b, l, _ = q.shape
    n_pairs = FOX_WIDTH // LANES
    tok = pl.BlockSpec((1, l, LANES), lambda i, j: (i, 0, j))
    return pl.pallas_call(
        _fox_kernel, grid=(b, n_pairs),
        in_specs=[tok, tok, tok, pl.BlockSpec((1, LANES, l), lambda i, j: (i, j, 0))],
        out_specs=tok,
        out_shape=jax.ShapeDtypeStruct((b, l, FOX_WIDTH), BF16),
        scratch_shapes=[pltpu.VMEM((2, l, LANES), BF16), pltpu.VMEM((2, 2, l, ATT_BLOCK), F32)],
        compiler_params=_cparams(2), name="fox_attn")(q, k, cs, vt)


def _s5_chunk_kernel(xt_ref, tr_ref, o_ref, a16r_ref, a16i_ref, yg_ref, *, chunks_per_seq):
    lane_tile = min(4 * LANES, xt_ref.shape[2])
    n_tiles = xt_ref.shape[2] // lane_tile
    n_levels = chunks_per_seq.bit_length() - 1
    reps = lane_tile // LANES
    kpos = lax.broadcasted_iota(jnp.int32, (S5_STATE, LANES), 1) % chunks_per_seq
    a_r, a_i = a16r_ref[0], a16i_ref[0]
    coef = []
    for lvl in range(n_levels):
        keep = kpos >= (1 << lvl)
        coef.append((jnp.concatenate([jnp.where(keep, a_r, 0.0)] * reps, axis=1),
                     jnp.concatenate([jnp.where(keep, a_i, 0.0)] * reps, axis=1)))
        a_r, a_i = a_r * a_r - a_i * a_i, 2.0 * a_r * a_i
    first = jnp.concatenate([kpos] * reps, axis=1) == 0
    stage1 = [_dot(tr_ref[0], xt_ref[0, :, n * lane_tile:(n + 1) * lane_tile]) for n in range(n_tiles)]
    states = [(yr[CHUNK_W:CHUNK_W + S5_STATE], yr[CHUNK_W + S5_STATE:]) for yr in stage1]
    for lvl in range(n_levels):
        c_r, c_i = coef[lvl]
        rolled = [(pltpu.roll(s_r, 1 << lvl, axis=1), pltpu.roll(s_i, 1 << lvl, axis=1)) for s_r, s_i in states]
        states = [(s_r + (c_r * p_r - c_i * p_i), s_i + (c_r * p_i + c_i * p_r))
                  for (s_r, s_i), (p_r, p_i) in zip(states, rolled)]
    entering = [jnp.concatenate([jnp.where(first, 0.0, pltpu.roll(s_r, 1, axis=1)),
                                 jnp.where(first, 0.0, pltpu.roll(s_i, 1, axis=1))], axis=0).astype(BF16)
                for s_r, s_i in states]
    carried = [_dot(o_ref[0], s_in) for s_in in entering]
    for n in range(n_tiles):
        yg = _gelu_tanh(stage1[n][:CHUNK_W] + carried[n]).astype(BF16)
        yg_ref[:, :, n * lane_tile:(n + 1) * lane_tile] = yg.reshape(CHUNK, S5_GROUP_CH, lane_tile)


def _s5_chunk(xt, tr_op, o_op, a16r, a16i, chunks_per_seq):
    g, _, nb = xt.shape
    per_g = lambda a: pl.BlockSpec((1,) + a.shape[1:], lambda i: (i, 0, 0))
    return pl.pallas_call(
        functools.partial(_s5_chunk_kernel, chunks_per_seq=chunks_per_seq), grid=(g,),
        in_specs=[per_g(xt), per_g(tr_op), per_g(o_op), per_g(a16r), per_g(a16i)],
        out_specs=pl.BlockSpec((CHUNK, S5_GROUP_CH, nb), lambda i: (0, i, 0)),
        out_shape=jax.ShapeDtypeStruct((CHUNK, S5_WIDTH, nb), BF16),
        compiler_params=_cparams(1), name="s5_chunk")(xt, tr_op, o_op, a16r, a16i)


def _s5_post_kernel(yg_ref, wgt_ref, bg_ref, gn_ref, o_ref):
    for t in range(CHUNK):
        y = yg_ref[t]
        gate = _dot(wgt_ref[...], y) + bg_ref[...]
        z = y.astype(F32) * jax.nn.sigmoid(gate)
        zn = z * lax.rsqrt(jnp.mean(z * z, axis=0, keepdims=True) + EPS) * gn_ref[...]
        o_ref[0, :, t, :] = zn.T


def _s5_post(yg, w_glu_t, b_glu_col, gain_col, batch):
    _, _, nb = yg.shape
    nck = nb // batch
    full = lambda a: pl.BlockSpec(a.shape, lambda i: (0,) * a.ndim)
    return pl.pallas_call(
        _s5_post_kernel, grid=(batch,),
        in_specs=[pl.BlockSpec((CHUNK, S5_WIDTH, nck), lambda i: (0, 0, i)),
                  full(w_glu_t), full(b_glu_col), full(gain_col)],
        out_specs=pl.BlockSpec((1, nck, CHUNK, S5_WIDTH), lambda i: (i, 0, 0, 0)),
        out_shape=jax.ShapeDtypeStruct((batch, nck, CHUNK, S5_WIDTH), F32),
        compiler_params=_cparams(1), name="s5_post")(yg, w_glu_t, b_glu_col, gain_col)


def _mix_cross_kernel(x_ref, fox_ref, s5_ref, gf_ref, wo_ref, gc_ref, wxq_ref, gq_ref, xk_ref, xv_ref, wxo_ref,
                      o_ref):
    halves = _row_halves(x_ref.shape[1])
    heads = [slice(hd * X_HEAD_DIM, (hd + 1) * X_HEAD_DIM) for hd in range(N_X_HEADS)]
    mixed = [jnp.concatenate([_rms(fox_ref[0, r, :].astype(F32), gf_ref[...]).astype(BF16),
                              s5_ref[0, r, :].astype(BF16)], axis=-1) for r in halves]
    proj = [_dot(m, wo_ref[...]) for m in mixed]
    h1 = [x_ref[0, r, :] + pr for r, pr in zip(halves, proj)]
    xq_raw = [_dot(_rms(h, gc_ref[...]).astype(BF16), wxq_ref[...]) for h in h1]
    xq = [(_head256_norm(v) * (gq_ref[...] * X_HEAD_DIM ** -0.5)).astype(BF16) for v in xq_raw]
    scores = [[_dot_nt(q[:, sl], xk_ref[0, :, sl]) for sl in heads] for q in xq]
    xo = []
    for sc in scores:
        outs = []
        for s, sl in zip(sc, heads):
            p = jnp.exp(s - jnp.max(s, axis=-1, keepdims=True))
            p = p / jnp.sum(p, axis=-1, keepdims=True)
            outs.append(_dot(p.astype(BF16), xv_ref[0, :, sl]).astype(BF16))
        xo.append(jnp.concatenate(outs, axis=-1))
    for r, h, v in zip(halves, h1, xo):
        o_ref[0, r, :] = h + _dot(v, wxo_ref[...])


def _mix_cross(x, fox, s5n, g_fox, w_out, g_cross, w_xq, g_xq, xk, xv, w_xo):
    b, l, d = x.shape
    tm = TOK_TILE
    tok = lambda w: pl.BlockSpec((1, tm, w), lambda i, j: (i, j, 0))
    full = lambda a: pl.BlockSpec(a.shape, lambda i, j: (0,) * a.ndim)
    mem = pl.BlockSpec((1, N_MEM, d), lambda i, j: (i, 0, 0))
    return pl.pallas_call(
        _mix_cross_kernel, grid=(b, l // tm),
        in_specs=[tok(d), tok(FOX_WIDTH), tok(S5_WIDTH), full(g_fox), full(w_out), full(g_cross),
                  full(w_xq), full(g_xq), mem, mem, full(w_xo)],
        out_specs=tok(d),
        out_shape=jax.ShapeDtypeStruct((b, l, d), F32),
        compiler_params=_cparams(2), name="mix_cross")(x, fox, s5n, g_fox, w_out, g_cross, w_xq, g_xq, xk, xv,
                                                        w_xo)


def _mem_kv_kernel(mem_ref, g_ref, wkv_ref, gk_ref, xk_ref, xv_ref):
    mn = _rms(mem_ref[0], g_ref[...]).astype(BF16)
    kv = _dot(mn, wkv_ref[...])
    xk_ref[0] = (_head256_norm(kv[:, :D_MODEL]) * gk_ref[...]).astype(BF16)
    xv_ref[0] = kv[:, D_MODEL:].astype(BF16)


def _mem_kv(mem, g_mem, w_xkv, g_xk):
    b, n, d = mem.shape
    full = lambda a: pl.BlockSpec(a.shape, lambda i: (0,) * a.ndim)
    blk = pl.BlockSpec((1, n, d), lambda i: (i, 0, 0))
    return pl.pallas_call(
        _mem_kv_kernel, grid=(b,),
        in_specs=[blk, full(g_mem), full(w_xkv), full(g_xk)],
        out_specs=(blk, blk),
        out_shape=(jax.ShapeDtypeStruct((b, n, d), BF16),) * 2,
        compiler_params=_cparams(1), name="mem_kv")(mem, g_mem, w_xkv, g_xk)


def _conv_ffn_kernel(h_ref, g_ref, wg_ref, wu_ref, cw_ref, cb_ref, wd_ref, o_ref, tail_ref, act_ref):
    @pl.when(pl.program_id(1) == 0)
    def _():
        tail_ref[...] = jnp.zeros_like(tail_ref)

    h = h_ref[0]
    tm = h.shape[0]
    hn = _rms(h, g_ref[...]).astype(BF16)
    row = lax.broadcasted_iota(jnp.int32, (tm, FF_CHUNK), 0)
    for c in range(D_FF // FF_CHUNK):
        sl = slice(c * FF_CHUNK, (c + 1) * FF_CHUNK)
        gate = _dot(hn, wg_ref[:, sl])
        up = _dot(hn, wu_ref[:, sl])
        prev1 = tail_ref[1:2, sl]
        prev2 = tail_ref[0:1, sl]
        g1 = jnp.where(row == 0, prev1, pltpu.roll(gate, 1, axis=0))
        g2 = jnp.where(row == 0, prev2, jnp.where(row == 1, prev1, pltpu.roll(gate, 2, axis=0)))
        conv = cb_ref[:, sl] + cw_ref[0:1, sl] * g2 + cw_ref[1:2, sl] * g1 + cw_ref[2:3, sl] * gate
        tail_ref[0:2, sl] = gate[tm - 2:tm, :]
        act_ref[:, sl] = (conv * jax.nn.sigmoid(conv) * up).astype(BF16)
    o_ref[0] = h + _dot(act_ref[...], wd_ref[...])


def _conv_ffn(h, g_ffn, w_gate, w_up, conv_w, conv_b, w_down):
    b, l, d = h.shape
    tm = TOK_TILE
    tok = pl.BlockSpec((1, tm, d), lambda i, j: (i, j, 0))
    full = lambda a: pl.BlockSpec(a.shape, lambda i, j: (0,) * a.ndim)
    return pl.pallas_call(
        _conv_ffn_kernel, grid=(b, l // tm),
        in_specs=[tok, full(g_ffn), full(w_gate), full(w_up), full(conv_w), full(conv_b), full(w_down)],
        out_specs=tok,
        out_shape=jax.ShapeDtypeStruct((b, l, d), F32),
        scratch_shapes=[pltpu.VMEM((8, D_FF), F32), pltpu.VMEM((tm, D_FF), BF16)],
        compiler_params=_cparams(2), name="conv_ffn")(h, g_ffn, w_gate, w_up, conv_w, conv_b, w_down)


def _gate_placement():
    place = np.zeros((LANES, FOX_WIDTH), np.float32)
    for h in range(N_FOX_HEADS):
        base = (h // 2) * LANES + (HEAD_DIM if h % 2 == 0 else 0)
        for term in range(3):
            place[term * N_FOX_HEADS + h, base + term] = 1.0
    return jnp.asarray(place, dtype=BF16)


def _layer(h, mem, p):
    b, l, d = h.shape
    nck = l // CHUNK
    row = lambda a: a.reshape(1, -1)
    col = lambda a: a.reshape(-1, 1)
    w_in = p["w_in"]
    qk_end, v_end = 2 * FOX_WIDTH, 3 * FOX_WIDTH
    w_qk = w_in[:, :qk_end].astype(BF16)
    w_vt = w_in[:, qk_end:v_end].T.astype(BF16)
    w_uf = jnp.concatenate(
        [w_in[:, v_end + N_FOX_HEADS:],
         jnp.pad(w_in[:, v_end:v_end + N_FOX_HEADS], ((0, 0), (0, LANES - N_FOX_HEADS)))], axis=1).astype(BF16)
    f_bias = jnp.pad(p["fox_f_bias"], (0, LANES - N_FOX_HEADS)).reshape(1, LANES)

    tr_op, o_op, a16r, a16i = _s5_params(
        p["s5_log_dt"], p["s5_a_re"], p["s5_a_im"], p["s5_b_re"], p["s5_b_im"],
        p["s5_c_re"], p["s5_c_im"], p["s5_d"])

    q, k, vt = _in_proj(h, row(p["norm_mix"]), w_qk, w_vt,
                        row(jnp.tile(p["fox_q_norm"], N_FOX_HEADS)),
                        row(jnp.tile(p["fox_k_norm"], N_FOX_HEADS)))
    xt, cs = _u_gate(h, row(p["norm_mix"]), w_uf, f_bias, _gate_placement())
    fox = _fox_attn(q, k, cs, vt)

    yg = _s5_chunk(xt, tr_op, o_op, a16r, a16i, nck)
    lane_bcast = lambda a: jnp.broadcast_to(col(a), (a.shape[0], LANES))
    s5n = _s5_post(yg, p["s5_w_glu"].T.astype(BF16), lane_bcast(p["s5_b_glu"]), lane_bcast(p["out_norm_s5"]), b)
    s5n = s5n.reshape(b, l, S5_WIDTH)

    xk, xv = _mem_kv(mem, row(p["norm_mem"]), p["w_xkv"].astype(BF16),
                     row(jnp.tile(p["xk_norm"], N_X_HEADS)))
    h2 = _mix_cross(h, fox, s5n, row(p["out_norm_fox"]), p["w_out"].astype(BF16),
                    row(p["norm_cross"]), p["w_xq"].astype(BF16),
                    row(jnp.tile(p["xq_norm"], N_X_HEADS)), xk, xv, p["w_xo"].astype(BF16))

    w_up = p["w_ffn_up"]
    return _conv_ffn(h2, row(p["norm_ffn"]), w_up[:, :D_FF].astype(BF16), w_up[:, D_FF:].astype(BF16),
                     p["ffn_conv_w"], row(p["ffn_conv_b"]), p["w_ffn_down"].astype(BF16))


_PARAM_NAMES = (
    "norm_mix", "w_in", "fox_q_norm", "fox_k_norm", "fox_f_bias", "s5_a_re", "s5_a_im", "s5_log_dt",
    "s5_b_re", "s5_b_im", "s5_c_re", "s5_c_im", "s5_d", "s5_w_glu", "s5_b_glu", "out_norm_fox",
    "out_norm_s5", "w_out", "norm_cross", "norm_mem", "w_xq", "w_xkv", "xq_norm", "xk_norm", "w_xo",
    "norm_ffn", "w_ffn_up", "ffn_conv_w", "ffn_conv_b", "w_ffn_down")


def kernel(x, mem, norm_mix, w_in, fox_q_norm, fox_k_norm, fox_f_bias, s5_a_re, s5_a_im, s5_log_dt, s5_b_re, s5_b_im, s5_c_re, s5_c_im, s5_d, s5_w_glu, s5_b_glu, out_norm_fox, out_norm_s5, w_out, norm_cross, norm_mem, w_xq, w_xkv, xq_norm, xk_norm, w_xo, norm_ffn, w_ffn_up, ffn_conv_w, ffn_conv_b, w_ffn_down):
    stacked = (norm_mix, w_in, fox_q_norm, fox_k_norm, fox_f_bias, s5_a_re, s5_a_im, s5_log_dt,
               s5_b_re, s5_b_im, s5_c_re, s5_c_im, s5_d, s5_w_glu, s5_b_glu, out_norm_fox,
               out_norm_s5, w_out, norm_cross, norm_mem, w_xq, w_xkv, xq_norm, xk_norm, w_xo,
               norm_ffn, w_ffn_up, ffn_conv_w, ffn_conv_b, w_ffn_down)
    h = x
    for layer in range(norm_mix.shape[0]):
        h = _layer(h, mem, {n: a[layer] for n, a in zip(_PARAM_NAMES, stacked)})
    return h
```

```python
import functools
import math

import jax
import jax.numpy as jnp
import numpy as np
from jax import lax
from jax.experimental import pallas as pl
from jax.experimental.pallas import tpu as pltpu

F32 = jnp.float32
BF16 = jnp.bfloat16

D_MODEL = 1024
N_MEM = 256
FOX_WIDTH = 512
HEAD_DIM = 64
N_FOX_HEADS = 8
S5_WIDTH = 512
S5_GROUP_CH = 16
S5_GROUPS = 32
S5_STATE = 64
N_X_HEADS = 4
X_HEAD_DIM = 256
D_FF = 2816
CONV_W = 3
EPS = 1e-6

LANES = 128
CHUNK = 16
CHUNK_W = CHUNK * S5_GROUP_CH
TOK_TILE = 512
ATT_BLOCK = 256
FF_CHUNK = 256
NEG_BIG = -1e30
LOG2E = math.log2(math.e)
HI = lax.Precision.HIGHEST
VMEM_LIMIT = 56 * 1024 * 1024
IN_PROJ_VMEM_LIMIT = 60 * 1024 * 1024

NT_DIMS = (((1,), (1,)), ((), ()))


def _cparams(n_axes):
    return pltpu.CompilerParams(dimension_semantics=("arbitrary",) * n_axes,
                                vmem_limit_bytes=VMEM_LIMIT)


def _rms(xf, g):
    return xf * lax.rsqrt(jnp.mean(xf * xf, axis=-1, keepdims=True) + EPS) * g


def _dot(a, b):
    return jnp.dot(a, b, preferred_element_type=F32)


def _dot_nt(a, b):
    return lax.dot_general(a, b, NT_DIMS, preferred_element_type=F32)


def _head64_norm(x):
    lo = lax.broadcasted_iota(jnp.int32, (x.shape[0], LANES), 1) < HEAD_DIM
    outs = []
    for j in range(x.shape[1] // LANES):
        blk = x[:, j * LANES:(j + 1) * LANES]
        sq = blk * blk
        s_lo = jnp.sum(jnp.where(lo, sq, 0.0), axis=-1, keepdims=True)
        s_hi = jnp.sum(jnp.where(lo, 0.0, sq), axis=-1, keepdims=True)
        inv = jnp.where(lo, lax.rsqrt(s_lo * (1.0 / HEAD_DIM) + EPS),
                        lax.rsqrt(s_hi * (1.0 / HEAD_DIM) + EPS))
        outs.append(blk * inv)
    return jnp.concatenate(outs, axis=-1)


def _head256_norm(x):
    outs = []
    for h in range(x.shape[1] // X_HEAD_DIM):
        blk = x[:, h * X_HEAD_DIM:(h + 1) * X_HEAD_DIM]
        outs.append(blk * lax.rsqrt(jnp.mean(blk * blk, axis=-1, keepdims=True) + EPS))
    return jnp.concatenate(outs, axis=-1)


def _row_halves(rows):
    return [slice(0, rows // 2), slice(rows // 2, rows)]


def _log_sigmoid(z):
    return jnp.minimum(z, 0.0) - jnp.log(1.0 + jnp.exp(-jnp.abs(z)))


def _gelu_tanh(x):
    k = math.sqrt(2.0 / math.pi)
    half = 0.5 * x
    return half + half * jnp.tanh(x * (k + (k * 0.044715) * (x * x)))


def _s5_params_kernel(*refs):
    for gi in range(refs[0].shape[0]):
        _s5_group_operators(gi, *refs)


def _s5_group_operators(gi, ldt_ref, arc_ref, aic_ref, arr_ref, air_ref, bre_ref, bim_ref, cre2_ref, cim2_ref,
                        d_ref, tr_ref, o_ref, a16r_ref, a16i_ref):
    dt = jnp.exp(ldt_ref[gi])
    ar, ai = arc_ref[gi], aic_ref[gi]
    mag = jnp.exp(ar * dt)
    lr, li = mag * jnp.cos(ai * dt), mag * jnp.sin(ai * dt)
    den = ar * ar + ai * ai
    nr = lr - 1.0
    coef_r = (nr * ar + li * ai) / den
    coef_i = (li * ar - nr * ai) / den
    bb_r = coef_r * bre_ref[gi] - coef_i * bim_ref[gi]
    bb_i = coef_r * bim_ref[gi] + coef_i * bre_ref[gi]
    arr, air = arr_ref[gi], air_ref[gi]
    magr = jnp.exp(arr * dt)
    lrr, lir = magr * jnp.cos(air * dt), magr * jnp.sin(air * dt)

    lane_t = lax.broadcasted_iota(jnp.int32, (S5_STATE, CHUNK_W), 1) // S5_GROUP_CH
    lane16 = lax.broadcasted_iota(jnp.int32, (S5_GROUP_CH, CHUNK_W), 1)
    row16 = lax.broadcasted_iota(jnp.int32, (S5_GROUP_CH, CHUNK_W), 0)
    lo128 = lax.broadcasted_iota(jnp.int32, (S5_GROUP_CH, 2 * S5_STATE), 1) < S5_STATE
    c_cat = jnp.where(lo128, cre2_ref[gi], -cim2_ref[gi])
    d_col = d_ref[gi]

    expo = (CHUNK - 1) - lane_t
    pw_r = jnp.ones((S5_STATE, CHUNK_W), F32)
    pw_i = jnp.zeros((S5_STATE, CHUNK_W), F32)
    sq_r, sq_i = lr, li
    for bit in range(CHUNK.bit_length() - 1):
        take = ((expo >> bit) & 1) == 1
        pw_r, pw_i = (jnp.where(take, pw_r * sq_r - pw_i * sq_i, pw_r),
                      jnp.where(take, pw_r * sq_i + pw_i * sq_r, pw_i))
        sq_r, sq_i = sq_r * sq_r - sq_i * sq_i, 2.0 * sq_r * sq_i
    a16r_ref[gi] = jnp.broadcast_to(sq_r, (S5_STATE, LANES))
    a16i_ref[gi] = jnp.broadcast_to(sq_i, (S5_STATE, LANES))
    m_r = pw_r * bb_r - pw_i * bb_i
    m_i = pw_r * bb_i + pw_i * bb_r
    tr_ref[gi, CHUNK_W:CHUNK_W + S5_STATE, :] = m_r.astype(BF16)
    tr_ref[gi, CHUNK_W + S5_STATE:CHUNK_W + 2 * S5_STATE, :] = m_i.astype(BF16)
    k_all = jnp.dot(c_cat, jnp.concatenate([m_r, m_i], axis=0), preferred_element_type=F32, precision=HI)
    w_r, w_i = cre2_ref[gi], cim2_ref[gi]
    for tq in range(CHUNK):
        shift = (CHUNK - 1 - tq) * S5_GROUP_CH
        t_rows = k_all if shift == 0 else pltpu.roll(k_all, CHUNK_W - shift, axis=1)
        t_rows = (jnp.where(lane16 < (tq + 1) * S5_GROUP_CH, t_rows, 0.0)
                  + jnp.where(lane16 == tq * S5_GROUP_CH + row16, d_col, 0.0))
        tr_ref[gi, tq * S5_GROUP_CH:(tq + 1) * S5_GROUP_CH, :] = t_rows.astype(BF16)
        w_r, w_i = w_r * lrr - w_i * lir, w_r * lir + w_i * lrr
        o_ref[gi, tq * S5_GROUP_CH:(tq + 1) * S5_GROUP_CH, :] = jnp.where(lo128, w_r, -w_i).astype(BF16)


def _s5_params(s5_log_dt, s5_a_re, s5_a_im, s5_b_re, s5_b_im, s5_c_re, s5_c_im, s5_d):
    g, p, c = S5_GROUPS, S5_STATE, S5_GROUP_CH
    args = (
        s5_log_dt.reshape(g, 1, 1),
        s5_a_re.reshape(g, p, 1), s5_a_im.reshape(g, p, 1),
        jnp.tile(s5_a_re.reshape(g, 1, p), (1, 1, 2)), jnp.tile(s5_a_im.reshape(g, 1, p), (1, 1, 2)),
        jnp.tile(s5_b_re, (1, 1, CHUNK)), jnp.tile(s5_b_im, (1, 1, CHUNK)),
        jnp.tile(s5_c_re, (1, 1, 2)), jnp.tile(s5_c_im, (1, 1, 2)),
        s5_d.reshape(g, c, 1),
    )
    per_step = 4
    in_specs = [pl.BlockSpec((per_step,) + a.shape[1:], lambda i: (i, 0, 0)) for a in args]
    out_shape = (
        jax.ShapeDtypeStruct((g, CHUNK_W + 2 * p, CHUNK_W), BF16),
        jax.ShapeDtypeStruct((g, CHUNK_W, 2 * p), BF16),
        jax.ShapeDtypeStruct((g, p, LANES), F32),
        jax.ShapeDtypeStruct((g, p, LANES), F32),
    )
    out_specs = tuple(pl.BlockSpec((per_step,) + s.shape[1:], lambda i: (i, 0, 0)) for s in out_shape)
    return pl.pallas_call(
        _s5_params_kernel, grid=(g // per_step,), in_specs=in_specs, out_specs=out_specs, out_shape=out_shape,
        compiler_params=_cparams(1), name="s5_params")(*args)


def _pack3(v):
    hi = v.astype(BF16).astype(F32)
    rest = v - hi
    mid = rest.astype(BF16).astype(F32)
    lo = rest - mid
    return (hi + pltpu.roll(mid, N_FOX_HEADS, axis=1) + pltpu.roll(lo, 2 * N_FOX_HEADS, axis=1)).astype(BF16)


def _in_proj_kernel(x_ref, g_ref, wuf_ref, wqk_ref, wvt_ref, qg_ref, kg_ref, bias_ref, place_ref,
                    q_ref, k_ref, vt_ref, xt_ref, cs_ref, z_ref, f_ref):
    seq = x_ref.shape[1]
    nck = seq // CHUNK
    n_slab = S5_WIDTH // LANES
    grp_per_slab = LANES // S5_GROUP_CH
    tiles = [slice(r * TOK_TILE, (r + 1) * TOK_TILE) for r in range(seq // TOK_TILE)]
    gate_blocks = seq // LANES
    head_lane = lax.broadcasted_iota(jnp.int32, (LANES, LANES), 1) < N_FOX_HEADS
    lower = (lax.broadcasted_iota(jnp.int32, (LANES, LANES), 0)
             >= lax.broadcasted_iota(jnp.int32, (LANES, LANES), 1)).astype(BF16)

    def normed(rows):
        return _rms(x_ref[0, rows, :], g_ref[...]).astype(BF16)

    for rows in tiles:
        z = _dot(normed(rows), wuf_ref[...])
        for j in range(n_slab):
            z_ref[j, rows, :] = z[:, j * LANES:(j + 1) * LANES]
        f_ref[rows, :] = z[:, S5_WIDTH:]

    local = []
    for n, rows in enumerate(tiles):
        hn = normed(rows)
        qk = _dot(hn, wqk_ref[...])
        vt = _dot_nt(wvt_ref[...], hn)
        for t in range(n * CHUNK // len(tiles), (n + 1) * CHUNK // len(tiles)):
            for j in range(n_slab):
                blk = z_ref[j, pl.ds(t, nck, stride=CHUNK), :]
                xt_ref[j * grp_per_slab:(j + 1) * grp_per_slab, t * S5_GROUP_CH:(t + 1) * S5_GROUP_CH, :] = (
                    blk.T.reshape(grp_per_slab, S5_GROUP_CH, nck).astype(BF16))
        for blk in range(n * gate_blocks // len(tiles), (n + 1) * gate_blocks // len(tiles)):
            log2f = jnp.where(head_lane, _log_sigmoid(f_ref[blk * LANES:(blk + 1) * LANES, :] + bias_ref[...])
                              * LOG2E, 0.0)
            part = _dot(lower, _pack3(log2f))
            c = (part + pltpu.roll(part, LANES - N_FOX_HEADS, axis=1)
                 + pltpu.roll(part, LANES - 2 * N_FOX_HEADS, axis=1))
            local.append(jnp.where(head_lane, c, 0.0))
        q = _head64_norm(qk[:, :FOX_WIDTH]) * (qg_ref[...] * (HEAD_DIM ** -0.5 * LOG2E))
        k = _head64_norm(qk[:, FOX_WIDTH:]) * kg_ref[...]
        q_ref[0, rows, :] = q.astype(BF16)
        k_ref[0, rows, :] = k.astype(BF16)
        vt_ref[0, :, rows] = vt.astype(BF16)

    carry = jnp.zeros((1, LANES), F32)
    for blk in range(gate_blocks):
        c = local[blk] + carry
        carry = c[LANES - 1:LANES, :]
        cs_ref[0, blk * LANES:(blk + 1) * LANES, :] = _dot(_pack3(c), place_ref[...]).astype(BF16)


def _in_proj(x, norm_mix, w_uf, w_qk, w_vt, q_gain, k_gain, bias_row, place):
    b, l, d = x.shape
    nck = l // CHUNK
    full = lambda a: pl.BlockSpec(a.shape, lambda i: (0,) * a.ndim, pipeline_mode=pl.Buffered(1))
    tok = pl.BlockSpec((1, l, FOX_WIDTH), lambda i: (i, 0, 0))
    return pl.pallas_call(
        _in_proj_kernel, grid=(b,),
        in_specs=[pl.BlockSpec((1, l, d), lambda i: (i, 0, 0)), full(norm_mix), full(w_uf), full(w_qk),
                  full(w_vt), full(q_gain), full(k_gain), full(bias_row), full(place)],
        out_specs=(tok, tok, pl.BlockSpec((1, FOX_WIDTH, l), lambda i: (i, 0, 0)),
                   pl.BlockSpec((S5_GROUPS, CHUNK_W, nck), lambda i: (0, 0, i)), tok),
        out_shape=(jax.ShapeDtypeStruct((b, l, FOX_WIDTH), BF16),) * 2
        + (jax.ShapeDtypeStruct((b, FOX_WIDTH, l), BF16),
           jax.ShapeDtypeStruct((S5_GROUPS, CHUNK_W, b * nck), BF16),
           jax.ShapeDtypeStruct((b, l, FOX_WIDTH), BF16)),
        scratch_shapes=[pltpu.VMEM((S5_WIDTH // LANES, l, LANES), F32), pltpu.VMEM((l, LANES), F32)],
        compiler_params=pltpu.CompilerParams(dimension_semantics=("arbitrary",),
                                             vmem_limit_bytes=IN_PROJ_VMEM_LIMIT),
        name="in_proj")(x, norm_mix, w_uf, w_qk, w_vt, q_gain, k_gain, bias_row, place)


def _fox_kernel(q_ref, k_ref, cs_ref, vt_ref, o_ref, kaug_ref, s_ref):
    blk = ATT_BLOCK
    seq = q_ref.shape[1]
    lane_k = lax.broadcasted_iota(jnp.int32, (seq, LANES), 1)
    k2 = k_ref[0]
    cs = cs_ref[0]
    kaug_ref[0] = jnp.where(lane_k < HEAD_DIM, k2, cs)
    kaug_ref[1] = jnp.where(lane_k >= HEAD_DIM, k2, cs)

    lane_q = lax.broadcasted_iota(jnp.int32, (blk, LANES), 1)
    minus1 = [jnp.where((lane_q >= HEAD_DIM) & (lane_q < HEAD_DIM + 3), -1.0, 0.0).astype(BF16),
              jnp.where(lane_q < 3, -1.0, 0.0).astype(BF16)]
    own = [lane_q < HEAD_DIM, lane_q >= HEAD_DIM]
    key_pos = lax.broadcasted_iota(jnp.int32, (blk, blk), 0)
    qry_pos = lax.broadcasted_iota(jnp.int32, (blk, blk), 1)

    def scores(i):
        q2 = q_ref[0, i * blk:(i + 1) * blk, :]
        for hh in range(2):
            qa = jnp.where(own[hh], q2, minus1[hh])
            st = _dot_nt(kaug_ref[hh, 0:(i + 1) * blk, :], qa)
            if i > 0:
                s_ref[i % 2, hh, 0:i * blk, :] = st[:i * blk]
            s_ref[i % 2, hh, i * blk:(i + 1) * blk, :] = jnp.where(key_pos > qry_pos, NEG_BIG, st[i * blk:])

    def attend(i):
        keys = (i + 1) * blk
        outs = []
        for hh in range(2):
            st = s_ref[i % 2, hh, 0:keys, :]
            p = jnp.exp2(st - jnp.max(st, axis=0, keepdims=True))
            l = jnp.sum(p, axis=0, keepdims=True)
            acc = _dot(vt_ref[0, hh * HEAD_DIM:(hh + 1) * HEAD_DIM, 0:keys], p.astype(BF16))
            outs.append(acc / l)
        o_ref[0, i * blk:(i + 1) * blk, :] = jnp.concatenate(outs, axis=0).T.astype(BF16)

    n_blk = seq // blk
    scores(0)
    for i in range(n_blk):
        if i + 1 < n_blk:
            scores(i + 1)
        attend(i)


def _fox_attn(q, k, cs, vt):
    b, l, _ = q.shape
    n_pairs = FOX_WIDTH // LANES
    tok = pl.BlockSpec((1, l, LANES), lambda i, j: (i, 0, j))
    return pl.pallas_call(
        _fox_kernel, grid=(b, n_pairs),
        in_specs=[tok, tok, tok, pl.BlockSpec((1, LANES, l), lambda i, j: (i, j, 0))],
        out_specs=tok,
        out_shape=jax.ShapeDtypeStruct((b, l, FOX_WIDTH), BF16),
        scratch_shapes=[pltpu.VMEM((2, l, LANES), BF16), pltpu.VMEM((2, 2, l, ATT_BLOCK), F32)],
        compiler_params=_cparams(2), name="fox_attn")(q, k, cs, vt)


def _s5_chunk_kernel(xt_ref, tr_ref, o_ref, a16r_ref, a16i_ref, yg_ref, *, chunks_per_seq):
    lane_tile = min(4 * LANES, xt_ref.shape[2])
    n_tiles = xt_ref.shape[2] // lane_tile
    n_levels = chunks_per_seq.bit_length() - 1
    reps = lane_tile // LANES
    kpos = lax.broadcasted_iota(jnp.int32, (S5_STATE, LANES), 1) % chunks_per_seq
    a_r, a_i = a16r_ref[0], a16i_ref[0]
    coef = []
    for lvl in range(n_levels):
        keep = kpos >= (1 << lvl)
        coef.append((jnp.concatenate([jnp.where(keep, a_r, 0.0)] * reps, axis=1),
                     jnp.concatenate([jnp.where(keep, a_i, 0.0)] * reps, axis=1)))
        a_r, a_i = a_r * a_r - a_i * a_i, 2.0 * a_r * a_i
    first = jnp.concatenate([kpos] * reps, axis=1) == 0
    stage1 = [_dot(tr_ref[0], xt_ref[0, :, n * lane_tile:(n + 1) * lane_tile]) for n in range(n_tiles)]
    states = [(yr[CHUNK_W:CHUNK_W + S5_STATE], yr[CHUNK_W + S5_STATE:]) for yr in stage1]
    for lvl in range(n_levels):
        c_r, c_i = coef[lvl]
        rolled = [(pltpu.roll(s_r, 1 << lvl, axis=1), pltpu.roll(s_i, 1 << lvl, axis=1)) for s_r, s_i in states]
        states = [(s_r + (c_r * p_r - c_i * p_i), s_i + (c_r * p_i + c_i * p_r))
                  for (s_r, s_i), (p_r, p_i) in zip(states, rolled)]
    entering = [jnp.concatenate([jnp.where(first, 0.0, pltpu.roll(s_r, 1, axis=1)),
                                 jnp.where(first, 0.0, pltpu.roll(s_i, 1, axis=1))], axis=0).astype(BF16)
                for s_r, s_i in states]
    carried = [_dot(o_ref[0], s_in) for s_in in entering]
    for n in range(n_tiles):
        yg = _gelu_tanh(stage1[n][:CHUNK_W] + carried[n]).astype(BF16)
        yg_ref[:, :, n * lane_tile:(n + 1) * lane_tile] = yg.reshape(CHUNK, S5_GROUP_CH, lane_tile)


def _s5_chunk(xt, tr_op, o_op, a16r, a16i, chunks_per_seq):
    g, _, nb = xt.shape
    per_g = lambda a: pl.BlockSpec((1,) + a.shape[1:], lambda i: (i, 0, 0))
    return pl.pallas_call(
        functools.partial(_s5_chunk_kernel, chunks_per_seq=chunks_per_seq), grid=(g,),
        in_specs=[per_g(xt), per_g(tr_op), per_g(o_op), per_g(a16r), per_g(a16i)],
        out_specs=pl.BlockSpec((CHUNK, S5_GROUP_CH, nb), lambda i: (0, i, 0)),
        out_shape=jax.ShapeDtypeStruct((CHUNK, S5_WIDTH, nb), BF16),
        compiler_params=_cparams(1), name="s5_chunk")(xt, tr_op, o_op, a16r, a16i)


def _s5_post_kernel(yg_ref, wgt_ref, bg_ref, gn_ref, o_ref):
    for t in range(CHUNK):
        y = yg_ref[t]
        gate = _dot(wgt_ref[...], y) + bg_ref[...]
        z = y.astype(F32) * jax.nn.sigmoid(gate)
        zn = z * lax.rsqrt(jnp.mean(z * z, axis=0, keepdims=True) + EPS) * gn_ref[...]
        o_ref[0, :, t, :] = zn.T


def _s5_post(yg, w_glu_t, b_glu_col, gain_col, batch):
    _, _, nb = yg.shape
    nck = nb // batch
    full = lambda a: pl.BlockSpec(a.shape, lambda i: (0,) * a.ndim)
    return pl.pallas_call(
        _s5_post_kernel, grid=(batch,),
        in_specs=[pl.BlockSpec((CHUNK, S5_WIDTH, nck), lambda i: (0, 0, i)),
                  full(w_glu_t), full(b_glu_col), full(gain_col)],
        out_specs=pl.BlockSpec((1, nck, CHUNK, S5_WIDTH), lambda i: (i, 0, 0, 0)),
        out_shape=jax.ShapeDtypeStruct((batch, nck, CHUNK, S5_WIDTH), F32),
        compiler_params=_cparams(1), name="s5_post")(yg, w_glu_t, b_glu_col, gain_col)


def _mix_cross_kernel(x_ref, fox_ref, s5_ref, gf_ref, wo_ref, gc_ref, wxq_ref, gq_ref, xk_ref, xv_ref, wxo_ref,
                      o_ref):
    halves = _row_halves(x_ref.shape[1])
    heads = [slice(hd * X_HEAD_DIM, (hd + 1) * X_HEAD_DIM) for hd in range(N_X_HEADS)]
    mixed = [jnp.concatenate([_rms(fox_ref[0, r, :].astype(F32), gf_ref[...]).astype(BF16),
                              s5_ref[0, r, :].astype(BF16)], axis=-1) for r in halves]
    proj = [_dot(m, wo_ref[...]) for m in mixed]
    h1 = [x_ref[0, r, :] + pr for r, pr in zip(halves, proj)]
    xq_raw = [_dot(_rms(h, gc_ref[...]).astype(BF16), wxq_ref[...]) for h in h1]
    xq = [(_head256_norm(v) * (gq_ref[...] * X_HEAD_DIM ** -0.5)).astype(BF16) for v in xq_raw]
    scores = [[_dot_nt(q[:, sl], xk_ref[0, :, sl]) for sl in heads] for q in xq]
    xo = []
    for sc in scores:
        outs = []
        for s, sl in zip(sc, heads):
            p = jnp.exp(s - jnp.max(s, axis=-1, keepdims=True))
            p = p / jnp.sum(p, axis=-1, keepdims=True)
            outs.append(_dot(p.astype(BF16), xv_ref[0, :, sl]).astype(BF16))
        xo.append(jnp.concatenate(outs, axis=-1))
    for r, h, v in zip(halves, h1, xo):
        o_ref[0, r, :] = h + _dot(v, wxo_ref[...])


def _mix_cross(x, fox, s5n, g_fox, w_out, g_cross, w_xq, g_xq, xk, xv, w_xo):
    b, l, d = x.shape
    tm = TOK_TILE
    tok = lambda w: pl.BlockSpec((1, tm, w), lambda i, j: (i, j, 0))
    full = lambda a: pl.BlockSpec(a.shape, lambda i, j: (0,) * a.ndim)
    mem = pl.BlockSpec((1, N_MEM, d), lambda i, j: (i, 0, 0))
    return pl.pallas_call(
        _mix_cross_kernel, grid=(b, l // tm),
        in_specs=[tok(d), tok(FOX_WIDTH), tok(S5_WIDTH), full(g_fox), full(w_out), full(g_cross),
                  full(w_xq), full(g_xq), mem, mem, full(w_xo)],
        out_specs=tok(d),
        out_shape=jax.ShapeDtypeStruct((b, l, d), F32),
        compiler_params=_cparams(2), name="mix_cross")(x, fox, s5n, g_fox, w_out, g_cross, w_xq, g_xq, xk, xv,
                                                        w_xo)


def _mem_kv_kernel(mem_ref, g_ref, wkv_ref, gk_ref, xk_ref, xv_ref):
    mn = _rms(mem_ref[0], g_ref[...]).astype(BF16)
    kv = _dot(mn, wkv_ref[...])
    xk_ref[0] = (_head256_norm(kv[:, :D_MODEL]) * gk_ref[...]).astype(BF16)
    xv_ref[0] = kv[:, D_MODEL:].astype(BF16)


def _mem_kv(mem, g_mem, w_xkv, g_xk):
    b, n, d = mem.shape
    full = lambda a: pl.BlockSpec(a.shape, lambda i: (0,) * a.ndim)
    blk = pl.BlockSpec((1, n, d), lambda i: (i, 0, 0))
    return pl.pallas_call(
        _mem_kv_kernel, grid=(b,),
        in_specs=[blk, full(g_mem), full(w_xkv), full(g_xk)],
        out_specs=(blk, blk),
        out_shape=(jax.ShapeDtypeStruct((b, n, d), BF16),) * 2,
        compiler_params=_cparams(1), name="mem_kv")(mem, g_mem, w_xkv, g_xk)


def _conv_ffn_kernel(h_ref, g_ref, wg_ref, wu_ref, cw_ref, cb_ref, wd_ref, o_ref, tail_ref, act_ref):
    @pl.when(pl.program_id(1) == 0)
    def _():
        tail_ref[...] = jnp.zeros_like(tail_ref)

    h = h_ref[0]
    tm = h.shape[0]
    hn = _rms(h, g_ref[...]).astype(BF16)
    row = lax.broadcasted_iota(jnp.int32, (tm, FF_CHUNK), 0)
    for c in range(D_FF // FF_CHUNK):
        sl = slice(c * FF_CHUNK, (c + 1) * FF_CHUNK)
        gate = _dot(hn, wg_ref[:, sl])
        up = _dot(hn, wu_ref[:, sl])
        prev1 = tail_ref[1:2, sl]
        prev2 = tail_ref[0:1, sl]
        g1 = jnp.where(row == 0, prev1, pltpu.roll(gate, 1, axis=0))
        g2 = jnp.where(row == 0, prev2, jnp.where(row == 1, prev1, pltpu.roll(gate, 2, axis=0)))
        conv = cb_ref[:, sl] + cw_ref[0:1, sl] * g2 + cw_ref[1:2, sl] * g1 + cw_ref[2:3, sl] * gate
        tail_ref[0:2, sl] = gate[tm - 2:tm, :]
        act_ref[:, sl] = (conv * jax.nn.sigmoid(conv) * up).astype(BF16)
    o_ref[0] = h + _dot(act_ref[...], wd_ref[...])


def _conv_ffn(h, g_ffn, w_gate, w_up, conv_w, conv_b, w_down):
    b, l, d = h.shape
    tm = TOK_TILE
    tok = pl.BlockSpec((1, tm, d), lambda i, j: (i, j, 0))
    full = lambda a: pl.BlockSpec(a.shape, lambda i, j: (0,) * a.ndim)
    return pl.pallas_call(
        _conv_ffn_kernel, grid=(b, l // tm),
        in_specs=[tok, full(g_ffn), full(w_gate), full(w_up), full(conv_w), full(conv_b), full(w_down)],
        out_specs=tok,
        out_shape=jax.ShapeDtypeStruct((b, l, d), F32),
        scratch_shapes=[pltpu.VMEM((8, D_FF), F32), pltpu.VMEM((tm, D_FF), BF16)],
        compiler_params=_cparams(2), name="conv_ffn")(h, g_ffn, w_gate, w_up, conv_w, conv_b, w_down)


def _gate_placement():
    place = np.zeros((LANES, FOX_WIDTH), np.float32)
    for h in range(N_FOX_HEADS):
        base = (h // 2) * LANES + (HEAD_DIM if h % 2 == 0 else 0)
        for term in range(3):
            place[term * N_FOX_HEADS + h, base + term] = 1.0
    return jnp.asarray(place, dtype=BF16)


def _layer(h, mem, p):
    b, l, d = h.shape
    nck = l // CHUNK
    row = lambda a: a.reshape(1, -1)
    col = lambda a: a.reshape(-1, 1)
    w_in = p["w_in"]
    qk_end, v_end = 2 * FOX_WIDTH, 3 * FOX_WIDTH
    w_qk = w_in[:, :qk_end].astype(BF16)
    w_vt = w_in[:, qk_end:v_end].T.astype(BF16)
    w_uf = jnp.concatenate(
        [w_in[:, v_end + N_FOX_HEADS:],
         jnp.pad(w_in[:, v_end:v_end + N_FOX_HEADS], ((0, 0), (0, LANES - N_FOX_HEADS)))], axis=1).astype(BF16)
    f_bias = jnp.pad(p["fox_f_bias"], (0, LANES - N_FOX_HEADS)).reshape(1, LANES)

    tr_op, o_op, a16r, a16i = _s5_params(
        p["s5_log_dt"], p["s5_a_re"], p["s5_a_im"], p["s5_b_re"], p["s5_b_im"],
        p["s5_c_re"], p["s5_c_im"], p["s5_d"])

    q, k, vt, xt, cs = _in_proj(h, row(p["norm_mix"]), w_uf, w_qk, w_vt,
                                row(jnp.tile(p["fox_q_norm"], N_FOX_HEADS)),
                                row(jnp.tile(p["fox_k_norm"], N_FOX_HEADS)), f_bias, _gate_placement())
    fox = _fox_attn(q, k, cs, vt)

    yg = _s5_chunk(xt, tr_op, o_op, a16r, a16i, nck)
    lane_bcast = lambda a: jnp.broadcast_to(col(a), (a.shape[0], LANES))
    s5n = _s5_post(yg, p["s5_w_glu"].T.astype(BF16), lane_bcast(p["s5_b_glu"]), lane_bcast(p["out_norm_s5"]), b)
    s5n = s5n.reshape(b, l, S5_WIDTH)

    xk, xv = _mem_kv(mem, row(p["norm_mem"]), p["w_xkv"].astype(BF16),
                     row(jnp.tile(p["xk_norm"], N_X_HEADS)))
    h2 = _mix_cross(h, fox, s5n, row(p["out_norm_fox"]), p["w_out"].astype(BF16),
                    row(p["norm_cross"]), p["w_xq"].astype(BF16),
                    row(jnp.tile(p["xq_norm"], N_X_HEADS)), xk, xv, p["w_xo"].astype(BF16))

    w_up = p["w_ffn_up"]
    return _conv_ffn(h2, row(p["norm_ffn"]), w_up[:, :D_FF].astype(BF16), w_up[:, D_FF:].astype(BF16),
                     p["ffn_conv_w"], row(p["ffn_conv_b"]), p["w_ffn_down"].astype(BF16))


_PARAM_NAMES = (
    "norm_mix", "w_in", "fox_q_norm", "fox_k_norm", "fox_f_bias", "s5_a_re", "s5_a_im", "s5_log_dt",
    "s5_b_re", "s5_b_im", "s5_c_re", "s5_c_im", "s5_d", "s5_w_glu", "s5_b_glu", "out_norm_fox",
    "out_norm_s5", "w_out", "norm_cross", "norm_mem", "w_xq", "w_xkv", "xq_norm", "xk_norm", "w_xo",
    "norm_ffn", "w_ffn_up", "ffn_conv_w", "ffn_conv_b", "w_ffn_down")


def kernel(x, mem, norm_mix, w_in, fox_q_norm, fox_k_norm, fox_f_bias, s5_a_re, s5_a_im, s5_log_dt, s5_b_re, s5_b_im, s5_c_re, s5_c_im, s5_d, s5_w_glu, s5_b_glu, out_norm_fox, out_norm_s5, w_out, norm_cross, norm_mem, w_xq, w_xkv, xq_norm, xk_norm, w_xo, norm_ffn, w_ffn_up, ffn_conv_w, ffn_conv_b, w_ffn_down):
    stacked = (norm_mix, w_in, fox_q_norm, fox_k_norm, fox_f_bias, s5_a_re, s5_a_im, s5_log_dt,
               s5_b_re, s5_b_im, s5_c_re, s5_c_im, s5_d, s5_w_glu, s5_b_glu, out_norm_fox,
               out_norm_s5, w_out, norm_cross, norm_mem, w_xq, w_xkv, xq_norm, xk_norm, w_xo,
               norm_ffn, w_ffn_up, ffn_conv_w, ffn_conv_b, w_ffn_down)
    h = x
    for layer in range(norm_mix.shape[0]):
        h = _layer(h, mem, {n: a[layer] for n, a in zip(_PARAM_NAMES, stacked)})
    return h
```

```python
import functools
import math

import jax
import jax.numpy as jnp
import numpy as np
from jax import lax
from jax.experimental import pallas as pl
from jax.experimental.pallas import tpu as pltpu

F32 = jnp.float32
BF16 = jnp.bfloat16

D_MODEL = 1024
N_MEM = 256
FOX_WIDTH = 512
HEAD_DIM = 64
N_FOX_HEADS = 8
S5_WIDTH = 512
S5_GROUP_CH = 16
S5_GROUPS = 32
S5_STATE = 64
N_X_HEADS = 4
X_HEAD_DIM = 256
D_FF = 2816
CONV_W = 3
EPS = 1e-6

LANES = 128
CHUNK = 16
CHUNK_W = CHUNK * S5_GROUP_CH
TOK_TILE = 512
ATT_BLOCK = 256
FF_CHUNK = 256
NEG_BIG = -1e30
LOG2E = math.log2(math.e)
HI = lax.Precision.HIGHEST
VMEM_LIMIT = 56 * 1024 * 1024
IN_PROJ_VMEM_LIMIT = 60 * 1024 * 1024

NT_DIMS = (((1,), (1,)), ((), ()))


def _cparams(n_axes):
    return pltpu.CompilerParams(dimension_semantics=("arbitrary",) * n_axes,
                                vmem_limit_bytes=VMEM_LIMIT)


def _rms(xf, g):
    return xf * lax.rsqrt(jnp.mean(xf * xf, axis=-1, keepdims=True) + EPS) * g


def _dot(a, b):
    return jnp.dot(a, b, preferred_element_type=F32)


def _dot_nt(a, b):
    return lax.dot_general(a, b, NT_DIMS, preferred_element_type=F32)


def _head64_norm(x):
    lo = lax.broadcasted_iota(jnp.int32, (x.shape[0], LANES), 1) < HEAD_DIM
    outs = []
    for j in range(x.shape[1] // LANES):
        blk = x[:, j * LANES:(j + 1) * LANES]
        sq = blk * blk
        s_lo = jnp.sum(jnp.where(lo, sq, 0.0), axis=-1, keepdims=True)
        s_hi = jnp.sum(jnp.where(lo, 0.0, sq), axis=-1, keepdims=True)
        inv = jnp.where(lo, lax.rsqrt(s_lo * (1.0 / HEAD_DIM) + EPS),
                        lax.rsqrt(s_hi * (1.0 / HEAD_DIM) + EPS))
        outs.append(blk * inv)
    return jnp.concatenate(outs, axis=-1)


def _head256_norm(x):
    outs = []
    for h in range(x.shape[1] // X_HEAD_DIM):
        blk = x[:, h * X_HEAD_DIM:(h + 1) * X_HEAD_DIM]
        outs.append(blk * lax.rsqrt(jnp.mean(blk * blk, axis=-1, keepdims=True) + EPS))
    return jnp.concatenate(outs, axis=-1)


def _row_halves(rows):
    return [slice(0, rows // 2), slice(rows // 2, rows)]


def _sigmoid(z):
    return 0.5 * jnp.tanh(0.5 * z) + 0.5


def _log_sigmoid(z):
    return jnp.minimum(z, 0.0) - jnp.log(1.0 + jnp.exp(-jnp.abs(z)))


def _gelu_tanh(x):
    k = math.sqrt(2.0 / math.pi)
    half = 0.5 * x
    return half + half * jnp.tanh(x * (k + (k * 0.044715) * (x * x)))


def _s5_params_kernel(*refs):
    for gi in range(refs[0].shape[0]):
        _s5_group_operators(gi, *refs)


def _s5_group_operators(gi, ldt_ref, arc_ref, aic_ref, arr_ref, air_ref, bre_ref, bim_ref, cre2_ref, cim2_ref,
                        d_ref, tr_ref, o_ref, a16r_ref, a16i_ref):
    dt = jnp.exp(ldt_ref[gi])
    ar, ai = arc_ref[gi], aic_ref[gi]
    mag = jnp.exp(ar * dt)
    lr, li = mag * jnp.cos(ai * dt), mag * jnp.sin(ai * dt)
    den = ar * ar + ai * ai
    nr = lr - 1.0
    coef_r = (nr * ar + li * ai) / den
    coef_i = (li * ar - nr * ai) / den
    bb_r = coef_r * bre_ref[gi] - coef_i * bim_ref[gi]
    bb_i = coef_r * bim_ref[gi] + coef_i * bre_ref[gi]
    arr, air = arr_ref[gi], air_ref[gi]
    magr = jnp.exp(arr * dt)
    lrr, lir = magr * jnp.cos(air * dt), magr * jnp.sin(air * dt)

    lane_t = lax.broadcasted_iota(jnp.int32, (S5_STATE, CHUNK_W), 1) // S5_GROUP_CH
    lane16 = lax.broadcasted_iota(jnp.int32, (S5_GROUP_CH, CHUNK_W), 1)
    row16 = lax.broadcasted_iota(jnp.int32, (S5_GROUP_CH, CHUNK_W), 0)
    lo128 = lax.broadcasted_iota(jnp.int32, (S5_GROUP_CH, 2 * S5_STATE), 1) < S5_STATE
    c_cat = jnp.where(lo128, cre2_ref[gi], -cim2_ref[gi])
    d_col = d_ref[gi]

    expo = (CHUNK - 1) - lane_t
    pw_r = jnp.ones((S5_STATE, CHUNK_W), F32)
    pw_i = jnp.zeros((S5_STATE, CHUNK_W), F32)
    sq_r, sq_i = lr, li
    for bit in range(CHUNK.bit_length() - 1):
        take = ((expo >> bit) & 1) == 1
        pw_r, pw_i = (jnp.where(take, pw_r * sq_r - pw_i * sq_i, pw_r),
                      jnp.where(take, pw_r * sq_i + pw_i * sq_r, pw_i))
        sq_r, sq_i = sq_r * sq_r - sq_i * sq_i, 2.0 * sq_r * sq_i
    a16r_ref[gi] = jnp.broadcast_to(sq_r, (S5_STATE, LANES))
    a16i_ref[gi] = jnp.broadcast_to(sq_i, (S5_STATE, LANES))
    m_r = pw_r * bb_r - pw_i * bb_i
    m_i = pw_r * bb_i + pw_i * bb_r
    tr_ref[gi, CHUNK_W:CHUNK_W + S5_STATE, :] = m_r.astype(BF16)
    tr_ref[gi, CHUNK_W + S5_STATE:CHUNK_W + 2 * S5_STATE, :] = m_i.astype(BF16)
    k_all = jnp.dot(c_cat, jnp.concatenate([m_r, m_i], axis=0), preferred_element_type=F32, precision=HI)
    w_r, w_i = cre2_ref[gi], cim2_ref[gi]
    for tq in range(CHUNK):
        shift = (CHUNK - 1 - tq) * S5_GROUP_CH
        t_rows = k_all if shift == 0 else pltpu.roll(k_all, CHUNK_W - shift, axis=1)
        t_rows = (jnp.where(lane16 < (tq + 1) * S5_GROUP_CH, t_rows, 0.0)
                  + jnp.where(lane16 == tq * S5_GROUP_CH + row16, d_col, 0.0))
        tr_ref[gi, tq * S5_GROUP_CH:(tq + 1) * S5_GROUP_CH, :] = t_rows.astype(BF16)
        w_r, w_i = w_r * lrr - w_i * lir, w_r * lir + w_i * lrr
        o_ref[gi, tq * S5_GROUP_CH:(tq + 1) * S5_GROUP_CH, :] = jnp.where(lo128, w_r, -w_i).astype(BF16)


def _s5_params(s5_log_dt, s5_a_re, s5_a_im, s5_b_re, s5_b_im, s5_c_re, s5_c_im, s5_d):
    g, p, c = S5_GROUPS, S5_STATE, S5_GROUP_CH
    args = (
        s5_log_dt.reshape(g, 1, 1),
        s5_a_re.reshape(g, p, 1), s5_a_im.reshape(g, p, 1),
        jnp.tile(s5_a_re.reshape(g, 1, p), (1, 1, 2)), jnp.tile(s5_a_im.reshape(g, 1, p), (1, 1, 2)),
        jnp.tile(s5_b_re, (1, 1, CHUNK)), jnp.tile(s5_b_im, (1, 1, CHUNK)),
        jnp.tile(s5_c_re, (1, 1, 2)), jnp.tile(s5_c_im, (1, 1, 2)),
        s5_d.reshape(g, c, 1),
    )
    per_step = 4
    in_specs = [pl.BlockSpec((per_step,) + a.shape[1:], lambda i: (i, 0, 0)) for a in args]
    out_shape = (
        jax.ShapeDtypeStruct((g, CHUNK_W + 2 * p, CHUNK_W), BF16),
        jax.ShapeDtypeStruct((g, CHUNK_W, 2 * p), BF16),
        jax.ShapeDtypeStruct((g, p, LANES), F32),
        jax.ShapeDtypeStruct((g, p, LANES), F32),
    )
    out_specs = tuple(pl.BlockSpec((per_step,) + s.shape[1:], lambda i: (i, 0, 0)) for s in out_shape)
    return pl.pallas_call(
        _s5_params_kernel, grid=(g // per_step,), in_specs=in_specs, out_specs=out_specs, out_shape=out_shape,
        compiler_params=_cparams(1), name="s5_params")(*args)


def _pack3(v):
    hi = v.astype(BF16).astype(F32)
    rest = v - hi
    mid = rest.astype(BF16).astype(F32)
    lo = rest - mid
    return (hi + pltpu.roll(mid, N_FOX_HEADS, axis=1) + pltpu.roll(lo, 2 * N_FOX_HEADS, axis=1)).astype(BF16)


def _in_proj_kernel(x_ref, g_ref, wuf_ref, wqk_ref, wvt_ref, qg_ref, kg_ref, bias_ref, place_ref,
                    q_ref, k_ref, vt_ref, xt_ref, cs_ref, z_ref, f_ref):
    seq = x_ref.shape[1]
    nck = seq // CHUNK
    n_slab = S5_WIDTH // LANES
    grp_per_slab = LANES // S5_GROUP_CH
    tiles = [slice(r * TOK_TILE, (r + 1) * TOK_TILE) for r in range(seq // TOK_TILE)]
    gate_blocks = seq // LANES
    head_lane = lax.broadcasted_iota(jnp.int32, (LANES, LANES), 1) < N_FOX_HEADS
    lower = (lax.broadcasted_iota(jnp.int32, (LANES, LANES), 0)
             >= lax.broadcasted_iota(jnp.int32, (LANES, LANES), 1)).astype(BF16)

    def normed(rows):
        return _rms(x_ref[0, rows, :], g_ref[...]).astype(BF16)

    for rows in tiles:
        z = _dot(normed(rows), wuf_ref[...])
        for j in range(n_slab):
            z_ref[j, rows, :] = z[:, j * LANES:(j + 1) * LANES]
        f_ref[rows, :] = z[:, S5_WIDTH:]

    local = []
    for n, rows in enumerate(tiles):
        hn = normed(rows)
        qk = _dot(hn, wqk_ref[...])
        vt = _dot_nt(wvt_ref[...], hn)
        for t in range(n * CHUNK // len(tiles), (n + 1) * CHUNK // len(tiles)):
            for j in range(n_slab):
                blk = z_ref[j, pl.ds(t, nck, stride=CHUNK), :]
                xt_ref[j * grp_per_slab:(j + 1) * grp_per_slab, t * S5_GROUP_CH:(t + 1) * S5_GROUP_CH, :] = (
                    blk.T.reshape(grp_per_slab, S5_GROUP_CH, nck).astype(BF16))
        for blk in range(n * gate_blocks // len(tiles), (n + 1) * gate_blocks // len(tiles)):
            log2f = jnp.where(head_lane, _log_sigmoid(f_ref[blk * LANES:(blk + 1) * LANES, :] + bias_ref[...])
                              * LOG2E, 0.0)
            part = _dot(lower, _pack3(log2f))
            c = (part + pltpu.roll(part, LANES - N_FOX_HEADS, axis=1)
                 + pltpu.roll(part, LANES - 2 * N_FOX_HEADS, axis=1))
            local.append(jnp.where(head_lane, c, 0.0))
        q = _head64_norm(qk[:, :FOX_WIDTH]) * (qg_ref[...] * (HEAD_DIM ** -0.5 * LOG2E))
        k = _head64_norm(qk[:, FOX_WIDTH:]) * kg_ref[...]
        q_ref[0, rows, :] = q.astype(BF16)
        k_ref[0, rows, :] = k.astype(BF16)
        vt_ref[0, :, rows] = vt.astype(BF16)

    carry = jnp.zeros((1, LANES), F32)
    for blk in range(gate_blocks):
        c = local[blk] + carry
        carry = c[LANES - 1:LANES, :]
        cs_ref[0, blk * LANES:(blk + 1) * LANES, :] = _dot(_pack3(c), place_ref[...]).astype(BF16)


def _in_proj(x, norm_mix, w_uf, w_qk, w_vt, q_gain, k_gain, bias_row, place):
    b, l, d = x.shape
    nck = l // CHUNK
    full = lambda a: pl.BlockSpec(a.shape, lambda i: (0,) * a.ndim, pipeline_mode=pl.Buffered(1))
    tok = pl.BlockSpec((1, l, FOX_WIDTH), lambda i: (i, 0, 0))
    return pl.pallas_call(
        _in_proj_kernel, grid=(b,),
        in_specs=[pl.BlockSpec((1, l, d), lambda i: (i, 0, 0)), full(norm_mix), full(w_uf), full(w_qk),
                  full(w_vt), full(q_gain), full(k_gain), full(bias_row), full(place)],
        out_specs=(tok, tok, pl.BlockSpec((1, FOX_WIDTH, l), lambda i: (i, 0, 0)),
                   pl.BlockSpec((S5_GROUPS, CHUNK_W, nck), lambda i: (0, 0, i)), tok),
        out_shape=(jax.ShapeDtypeStruct((b, l, FOX_WIDTH), BF16),) * 2
        + (jax.ShapeDtypeStruct((b, FOX_WIDTH, l), BF16),
           jax.ShapeDtypeStruct((S5_GROUPS, CHUNK_W, b * nck), BF16),
           jax.ShapeDtypeStruct((b, l, FOX_WIDTH), BF16)),
        scratch_shapes=[pltpu.VMEM((S5_WIDTH // LANES, l, LANES), F32), pltpu.VMEM((l, LANES), F32)],
        compiler_params=pltpu.CompilerParams(dimension_semantics=("arbitrary",),
                                             vmem_limit_bytes=IN_PROJ_VMEM_LIMIT),
        name="in_proj")(x, norm_mix, w_uf, w_qk, w_vt, q_gain, k_gain, bias_row, place)


def _fox_kernel(q_ref, k_ref, cs_ref, vt_ref, o_ref, kaug_ref, s_ref):
    blk = ATT_BLOCK
    seq = q_ref.shape[1]
    lane_k = lax.broadcasted_iota(jnp.int32, (seq, LANES), 1)
    k2 = k_ref[0]
    cs = cs_ref[0]
    kaug_ref[0] = jnp.where(lane_k < HEAD_DIM, k2, cs)
    kaug_ref[1] = jnp.where(lane_k >= HEAD_DIM, k2, cs)

    lane_q = lax.broadcasted_iota(jnp.int32, (blk, LANES), 1)
    minus1 = [jnp.where((lane_q >= HEAD_DIM) & (lane_q < HEAD_DIM + 3), -1.0, 0.0).astype(BF16),
              jnp.where(lane_q < 3, -1.0, 0.0).astype(BF16)]
    own = [lane_q < HEAD_DIM, lane_q >= HEAD_DIM]
    key_pos = lax.broadcasted_iota(jnp.int32, (blk, blk), 0)
    qry_pos = lax.broadcasted_iota(jnp.int32, (blk, blk), 1)

    def scores(i):
        q2 = q_ref[0, i * blk:(i + 1) * blk, :]
        for hh in range(2):
            qa = jnp.where(own[hh], q2, minus1[hh])
            st = _dot_nt(kaug_ref[hh, 0:(i + 1) * blk, :], qa)
            if i > 0:
                s_ref[i % 2, hh, 0:i * blk, :] = st[:i * blk]
            s_ref[i % 2, hh, i * blk:(i + 1) * blk, :] = jnp.where(key_pos > qry_pos, NEG_BIG, st[i * blk:])

    def attend(i):
        outs = []
        for hh in range(2):
            tile = lambda j: s_ref[i % 2, hh, j * blk:(j + 1) * blk, :]
            m = jnp.max(tile(0), axis=0, keepdims=True)
            for j in range(1, i + 1):
                m = jnp.maximum(m, jnp.max(tile(j), axis=0, keepdims=True))
            l = jnp.zeros((1, blk), F32)
            acc = jnp.zeros((HEAD_DIM, blk), F32)
            for j in range(i + 1):
                p = jnp.exp2(tile(j) - m)
                l = l + jnp.sum(p, axis=0, keepdims=True)
                acc = acc + _dot(vt_ref[0, hh * HEAD_DIM:(hh + 1) * HEAD_DIM, j * blk:(j + 1) * blk],
                                 p.astype(BF16))
            outs.append(acc / l)
        o_ref[0, i * blk:(i + 1) * blk, :] = jnp.concatenate(outs, axis=0).T.astype(BF16)

    n_blk = seq // blk
    scores(0)
    for i in range(n_blk):
        if i + 1 < n_blk:
            scores(i + 1)
        attend(i)


def _fox_attn(q, k, cs, vt):
    b, l, _ = q.shape
    n_pairs = FOX_WIDTH // LANES
    tok = pl.BlockSpec((1, l, LANES), lambda i, j: (i, 0, j))
    return pl.pallas_call(
        _fox_kernel, grid=(b, n_pairs),
        in_specs=[tok, tok, tok, pl.BlockSpec((1, LANES, l), lambda i, j: (i, j, 0))],
        out_specs=tok,
        out_shape=jax.ShapeDtypeStruct((b, l, FOX_WIDTH), BF16),
        scratch_shapes=[pltpu.VMEM((2, l, LANES), BF16), pltpu.VMEM((2, 2, l, ATT_BLOCK), F32)],
        compiler_params=_cparams(2), name="fox_attn")(q, k, cs, vt)


def _s5_chunk_kernel(xt_ref, tr_ref, o_ref, a16r_ref, a16i_ref, yg_ref, *, chunks_per_seq):
    lane_tile = LANES
    n_tiles = xt_ref.shape[2] // lane_tile
    n_levels = chunks_per_seq.bit_length() - 1
    reps = lane_tile // LANES
    kpos = lax.broadcasted_iota(jnp.int32, (S5_STATE, LANES), 1) % chunks_per_seq
    a_r, a_i = a16r_ref[0], a16i_ref[0]
    coef = []
    for lvl in range(n_levels):
        keep = kpos >= (1 << lvl)
        coef.append((jnp.concatenate([jnp.where(keep, a_r, 0.0)] * reps, axis=1),
                     jnp.concatenate([jnp.where(keep, a_i, 0.0)] * reps, axis=1)))
        a_r, a_i = a_r * a_r - a_i * a_i, 2.0 * a_r * a_i
    first = jnp.concatenate([kpos] * reps, axis=1) == 0
    stage1 = [_dot(tr_ref[0], xt_ref[0, :, n * lane_tile:(n + 1) * lane_tile]) for n in range(n_tiles)]
    states = [(yr[CHUNK_W:CHUNK_W + S5_STATE], yr[CHUNK_W + S5_STATE:]) for yr in stage1]
    for lvl in range(n_levels):
        c_r, c_i = coef[lvl]
        rolled = [(pltpu.roll(s_r, 1 << lvl, axis=1), pltpu.roll(s_i, 1 << lvl, axis=1)) for s_r, s_i in states]
        states = [(s_r + (c_r * p_r - c_i * p_i), s_i + (c_r * p_i + c_i * p_r))
                  for (s_r, s_i), (p_r, p_i) in zip(states, rolled)]
    entering = [jnp.concatenate([jnp.where(first, 0.0, pltpu.roll(s_r, 1, axis=1)),
                                 jnp.where(first, 0.0, pltpu.roll(s_i, 1, axis=1))], axis=0).astype(BF16)
                for s_r, s_i in states]
    carried = [_dot(o_ref[0], s_in) for s_in in entering]
    for n in range(n_tiles):
        yg = _gelu_tanh(stage1[n][:CHUNK_W] + carried[n]).astype(BF16)
        yg_ref[:, :, n * lane_tile:(n + 1) * lane_tile] = yg.reshape(CHUNK, S5_GROUP_CH, lane_tile)


def _s5_chunk(xt, tr_op, o_op, a16r, a16i, chunks_per_seq):
    g, _, nb = xt.shape
    per_g = lambda a: pl.BlockSpec((1,) + a.shape[1:], lambda i: (i, 0, 0))
    return pl.pallas_call(
        functools.partial(_s5_chunk_kernel, chunks_per_seq=chunks_per_seq), grid=(g,),
        in_specs=[per_g(xt), per_g(tr_op), per_g(o_op), per_g(a16r), per_g(a16i)],
        out_specs=pl.BlockSpec((CHUNK, S5_GROUP_CH, nb), lambda i: (0, i, 0)),
        out_shape=jax.ShapeDtypeStruct((CHUNK, S5_WIDTH, nb), BF16),
        compiler_params=_cparams(1), name="s5_chunk")(xt, tr_op, o_op, a16r, a16i)


def _s5_post_kernel(yg_ref, wgt_ref, bg_ref, gn_ref, o_ref):
    nck = yg_ref.shape[2]
    bias = jnp.concatenate([bg_ref[...]] * 2, axis=1)
    gain = jnp.concatenate([gn_ref[...]] * 2, axis=1)
    for t in range(0, CHUNK, 2):
        y = jnp.concatenate([yg_ref[t], yg_ref[t + 1]], axis=1)
        z = y.astype(F32) * _sigmoid(_dot(wgt_ref[...], y) + bias)
        zn = z * lax.rsqrt(jnp.mean(z * z, axis=0, keepdims=True) + EPS) * gain
        o_ref[0, :, t, :] = zn[:, :nck].T
        o_ref[0, :, t + 1, :] = zn[:, nck:].T


def _s5_post(yg, w_glu_t, b_glu_col, gain_col, batch):
    _, _, nb = yg.shape
    nck = nb // batch
    full = lambda a: pl.BlockSpec(a.shape, lambda i: (0,) * a.ndim)
    return pl.pallas_call(
        _s5_post_kernel, grid=(batch,),
        in_specs=[pl.BlockSpec((CHUNK, S5_WIDTH, nck), lambda i: (0, 0, i)),
                  full(w_glu_t), full(b_glu_col), full(gain_col)],
        out_specs=pl.BlockSpec((1, nck, CHUNK, S5_WIDTH), lambda i: (i, 0, 0, 0)),
        out_shape=jax.ShapeDtypeStruct((batch, nck, CHUNK, S5_WIDTH), F32),
        compiler_params=_cparams(1), name="s5_post")(yg, w_glu_t, b_glu_col, gain_col)


def _mix_cross_kernel(x_ref, fox_ref, s5_ref, gf_ref, wo_ref, gc_ref, wxq_ref, gq_ref, xk_ref, xv_ref, wxo_ref,
                      o_ref):
    halves = _row_halves(x_ref.shape[1])
    heads = [slice(hd * X_HEAD_DIM, (hd + 1) * X_HEAD_DIM) for hd in range(N_X_HEADS)]
    mixed = [jnp.concatenate([_rms(fox_ref[0, r, :].astype(F32), gf_ref[...]).astype(BF16),
                              s5_ref[0, r, :].astype(BF16)], axis=-1) for r in halves]
    proj = [_dot(m, wo_ref[...]) for m in mixed]
    h1 = [x_ref[0, r, :] + pr for r, pr in zip(halves, proj)]
    xq_raw = [_dot(_rms(h, gc_ref[...]).astype(BF16), wxq_ref[...]) for h in h1]
    xq = [(_head256_norm(v) * (gq_ref[...] * X_HEAD_DIM ** -0.5)).astype(BF16) for v in xq_raw]
    scores = [[_dot_nt(q[:, sl], xk_ref[0, :, sl]) for sl in heads] for q in xq]
    xo = []
    for sc in scores:
        outs = []
        for s, sl in zip(sc, heads):
            p = jnp.exp(s - jnp.max(s, axis=-1, keepdims=True))
            p = p / jnp.sum(p, axis=-1, keepdims=True)
            outs.append(_dot(p.astype(BF16), xv_ref[0, :, sl]).astype(BF16))
        xo.append(jnp.concatenate(outs, axis=-1))
    for r, h, v in zip(halves, h1, xo):
        o_ref[0, r, :] = h + _dot(v, wxo_ref[...])


def _mix_cross(x, fox, s5n, g_fox, w_out, g_cross, w_xq, g_xq, xk, xv, w_xo):
    b, l, d = x.shape
    tm = 2 * TOK_TILE
    tok = lambda w: pl.BlockSpec((1, tm, w), lambda i, j: (i, j, 0))
    full = lambda a: pl.BlockSpec(a.shape, lambda i, j: (0,) * a.ndim)
    mem = pl.BlockSpec((1, N_MEM, d), lambda i, j: (i, 0, 0))
    return pl.pallas_call(
        _mix_cross_kernel, grid=(b, l // tm),
        in_specs=[tok(d), tok(FOX_WIDTH), tok(S5_WIDTH), full(g_fox), full(w_out), full(g_cross),
                  full(w_xq), full(g_xq), mem, mem, full(w_xo)],
        out_specs=tok(d),
        out_shape=jax.ShapeDtypeStruct((b, l, d), F32),
        compiler_params=_cparams(2), name="mix_cross")(x, fox, s5n, g_fox, w_out, g_cross, w_xq, g_xq, xk, xv,
                                                        w_xo)


def _mem_kv_kernel(mem_ref, g_ref, wkv_ref, gk_ref, xk_ref, xv_ref):
    mn = _rms(mem_ref[0], g_ref[...]).astype(BF16)
    kv = _dot(mn, wkv_ref[...])
    xk_ref[0] = (_head256_norm(kv[:, :D_MODEL]) * gk_ref[...]).astype(BF16)
    xv_ref[0] = kv[:, D_MODEL:].astype(BF16)


def _mem_kv(mem, g_mem, w_xkv, g_xk):
    b, n, d = mem.shape
    full = lambda a: pl.BlockSpec(a.shape, lambda i: (0,) * a.ndim)
    blk = pl.BlockSpec((1, n, d), lambda i: (i, 0, 0))
    return pl.pallas_call(
        _mem_kv_kernel, grid=(b,),
        in_specs=[blk, full(g_mem), full(w_xkv), full(g_xk)],
        out_specs=(blk, blk),
        out_shape=(jax.ShapeDtypeStruct((b, n, d), BF16),) * 2,
        compiler_params=_cparams(1), name="mem_kv")(mem, g_mem, w_xkv, g_xk)


def _conv_ffn_kernel(h_ref, g_ref, wg_ref, wu_ref, cw_ref, cb_ref, wd_ref, o_ref, tail_ref, act_ref):
    @pl.when(pl.program_id(1) == 0)
    def _():
        tail_ref[...] = jnp.zeros_like(tail_ref)

    h = h_ref[0]
    tm = h.shape[0]
    hn = _rms(h, g_ref[...]).astype(BF16)
    row = lax.broadcasted_iota(jnp.int32, (tm, FF_CHUNK), 0)
    for c in range(D_FF // FF_CHUNK):
        sl = slice(c * FF_CHUNK, (c + 1) * FF_CHUNK)
        gate = _dot(hn, wg_ref[:, sl])
        up = _dot(hn, wu_ref[:, sl])
        prev1 = tail_ref[1:2, sl]
        prev2 = tail_ref[0:1, sl]
        g1 = jnp.where(row == 0, prev1, pltpu.roll(gate, 1, axis=0))
        g2 = jnp.where(row == 0, prev2, jnp.where(row == 1, prev1, pltpu.roll(gate, 2, axis=0)))
        conv = cb_ref[:, sl] + cw_ref[0:1, sl] * g2 + cw_ref[1:2, sl] * g1 + cw_ref[2:3, sl] * gate
        tail_ref[0:2, sl] = gate[tm - 2:tm, :]
        act_ref[:, sl] = (conv * _sigmoid(conv) * up).astype(BF16)
    o_ref[0] = h + _dot(act_ref[...], wd_ref[...])


def _conv_ffn(h, g_ffn, w_gate, w_up, conv_w, conv_b, w_down):
    b, l, d = h.shape
    tm = TOK_TILE
    tok = pl.BlockSpec((1, tm, d), lambda i, j: (i, j, 0))
    full = lambda a: pl.BlockSpec(a.shape, lambda i, j: (0,) * a.ndim)
    return pl.pallas_call(
        _conv_ffn_kernel, grid=(b, l // tm),
        in_specs=[tok, full(g_ffn), full(w_gate), full(w_up), full(conv_w), full(conv_b), full(w_down)],
        out_specs=tok,
        out_shape=jax.ShapeDtypeStruct((b, l, d), F32),
        scratch_shapes=[pltpu.VMEM((8, D_FF), F32), pltpu.VMEM((tm, D_FF), BF16)],
        compiler_params=_cparams(2), name="conv_ffn")(h, g_ffn, w_gate, w_up, conv_w, conv_b, w_down)


def _gate_placement():
    place = np.zeros((LANES, FOX_WIDTH), np.float32)
    for h in range(N_FOX_HEADS):
        base = (h // 2) * LANES + (HEAD_DIM if h % 2 == 0 else 0)
        for term in range(3):
            place[term * N_FOX_HEADS + h, base + term] = 1.0
    return jnp.asarray(place, dtype=BF16)


def _layer(h, mem, p):
    b, l, d = h.shape
    nck = l // CHUNK
    row = lambda a: a.reshape(1, -1)
    col = lambda a: a.reshape(-1, 1)
    w_in = p["w_in"]
    qk_end, v_end = 2 * FOX_WIDTH, 3 * FOX_WIDTH
    w_qk = w_in[:, :qk_end].astype(BF16)
    w_vt = w_in[:, qk_end:v_end].T.astype(BF16)
    w_uf = jnp.concatenate(
        [w_in[:, v_end + N_FOX_HEADS:],
         jnp.pad(w_in[:, v_end:v_end + N_FOX_HEADS], ((0, 0), (0, LANES - N_FOX_HEADS)))], axis=1).astype(BF16)
    f_bias = jnp.pad(p["fox_f_bias"], (0, LANES - N_FOX_HEADS)).reshape(1, LANES)

    tr_op, o_op, a16r, a16i = _s5_params(
        p["s5_log_dt"], p["s5_a_re"], p["s5_a_im"], p["s5_b_re"], p["s5_b_im"],
        p["s5_c_re"], p["s5_c_im"], p["s5_d"])

    q, k, vt, xt, cs = _in_proj(h, row(p["norm_mix"]), w_uf, w_qk, w_vt,
                                row(jnp.tile(p["fox_q_norm"], N_FOX_HEADS)),
                                row(jnp.tile(p["fox_k_norm"], N_FOX_HEADS)), f_bias, _gate_placement())
    fox = _fox_attn(q, k, cs, vt)

    yg = _s5_chunk(xt, tr_op, o_op, a16r, a16i, nck)
    lane_bcast = lambda a: jnp.broadcast_to(col(a), (a.shape[0], LANES))
    s5n = _s5_post(yg, p["s5_w_glu"].T.astype(BF16), lane_bcast(p["s5_b_glu"]), lane_bcast(p["out_norm_s5"]), b)
    s5n = s5n.reshape(b, l, S5_WIDTH)

    xk, xv = _mem_kv(mem, row(p["norm_mem"]), p["w_xkv"].astype(BF16),
                     row(jnp.tile(p["xk_norm"], N_X_HEADS)))
    h2 = _mix_cross(h, fox, s5n, row(p["out_norm_fox"]), p["w_out"].astype(BF16),
                    row(p["norm_cross"]), p["w_xq"].astype(BF16),
                    row(jnp.tile(p["xq_norm"], N_X_HEADS)), xk, xv, p["w_xo"].astype(BF16))

    w_up = p["w_ffn_up"]
    return _conv_ffn(h2, row(p["norm_ffn"]), w_up[:, :D_FF].astype(BF16), w_up[:, D_FF:].astype(BF16),
                     p["ffn_conv_w"], row(p["ffn_conv_b"]), p["w_ffn_down"].astype(BF16))


_PARAM_NAMES = (
    "norm_mix", "w_in", "fox_q_norm", "fox_k_norm", "fox_f_bias", "s5_a_re", "s5_a_im", "s5_log_dt",
    "s5_b_re", "s5_b_im", "s5_c_re", "s5_c_im", "s5_d", "s5_w_glu", "s5_b_glu", "out_norm_fox",
    "out_norm_s5", "w_out", "norm_cross", "norm_mem", "w_xq", "w_xkv", "xq_norm", "xk_norm", "w_xo",
    "norm_ffn", "w_ffn_up", "ffn_conv_w", "ffn_conv_b", "w_ffn_down")


def kernel(x, mem, norm_mix, w_in, fox_q_norm, fox_k_norm, fox_f_bias, s5_a_re, s5_a_im, s5_log_dt, s5_b_re, s5_b_im, s5_c_re, s5_c_im, s5_d, s5_w_glu, s5_b_glu, out_norm_fox, out_norm_s5, w_out, norm_cross, norm_mem, w_xq, w_xkv, xq_norm, xk_norm, w_xo, norm_ffn, w_ffn_up, ffn_conv_w, ffn_conv_b, w_ffn_down):
    stacked = (norm_mix, w_in, fox_q_norm, fox_k_norm, fox_f_bias, s5_a_re, s5_a_im, s5_log_dt,
               s5_b_re, s5_b_im, s5_c_re, s5_c_im, s5_d, s5_w_glu, s5_b_glu, out_norm_fox,
               out_norm_s5, w_out, norm_cross, norm_mem, w_xq, w_xkv, xq_norm, xk_norm, w_xo,
               norm_ffn, w_ffn_up, ffn_conv_w, ffn_conv_b, w_ffn_down)
    h = x
    for layer in range(norm_mix.shape[0]):
        h = _layer(h, mem, {n: a[layer] for n, a in zip(_PARAM_NAMES, stacked)})
    return h
```

```python
import functools
import math

import jax
import jax.numpy as jnp
import numpy as np
from jax import lax
from jax.experimental import pallas as pl
from jax.experimental.pallas import tpu as pltpu

F32 = jnp.float32
BF16 = jnp.bfloat16

D_MODEL = 1024
N_MEM = 256
FOX_WIDTH = 512
HEAD_DIM = 64
N_FOX_HEADS = 8
S5_WIDTH = 512
S5_GROUP_CH = 16
S5_GROUPS = 32
S5_STATE = 64
N_X_HEADS = 4
X_HEAD_DIM = 256
D_FF = 2816
CONV_W = 3
EPS = 1e-6

LANES = 128
CHUNK = 16
CHUNK_W = CHUNK * S5_GROUP_CH
TOK_TILE = 512
ATT_BLOCK = 256
FF_CHUNK = 256
NEG_BIG = -1e30
LOG2E = math.log2(math.e)
HI = lax.Precision.HIGHEST
VMEM_LIMIT = 56 * 1024 * 1024
IN_PROJ_VMEM_LIMIT = 60 * 1024 * 1024

NT_DIMS = (((1,), (1,)), ((), ()))


def _cparams(n_axes):
    return pltpu.CompilerParams(dimension_semantics=("arbitrary",) * n_axes,
                                vmem_limit_bytes=VMEM_LIMIT)


def _rms(xf, g):
    return xf * lax.rsqrt(jnp.mean(xf * xf, axis=-1, keepdims=True) + EPS) * g


def _dot(a, b):
    return jnp.dot(a, b, preferred_element_type=F32)


def _dot_nt(a, b):
    return lax.dot_general(a, b, NT_DIMS, preferred_element_type=F32)


def _head64_norm(x):
    lo = lax.broadcasted_iota(jnp.int32, (x.shape[0], LANES), 1) < HEAD_DIM
    outs = []
    for j in range(x.shape[1] // LANES):
        blk = x[:, j * LANES:(j + 1) * LANES]
        sq = blk * blk
        s_lo = jnp.sum(jnp.where(lo, sq, 0.0), axis=-1, keepdims=True)
        s_hi = jnp.sum(jnp.where(lo, 0.0, sq), axis=-1, keepdims=True)
        inv = jnp.where(lo, lax.rsqrt(s_lo * (1.0 / HEAD_DIM) + EPS),
                        lax.rsqrt(s_hi * (1.0 / HEAD_DIM) + EPS))
        outs.append(blk * inv)
    return jnp.concatenate(outs, axis=-1)


def _head256_norm(x):
    outs = []
    for h in range(x.shape[1] // X_HEAD_DIM):
        blk = x[:, h * X_HEAD_DIM:(h + 1) * X_HEAD_DIM]
        outs.append(blk * lax.rsqrt(jnp.mean(blk * blk, axis=-1, keepdims=True) + EPS))
    return jnp.concatenate(outs, axis=-1)


def _row_halves(rows):
    return [slice(0, rows // 2), slice(rows // 2, rows)]


def _sigmoid(z):
    return 0.5 * jnp.tanh(0.5 * z) + 0.5


def _log_sigmoid(z):
    return jnp.minimum(z, 0.0) - jnp.log(1.0 + jnp.exp(-jnp.abs(z)))


def _gelu_tanh(x):
    k = math.sqrt(2.0 / math.pi)
    half = 0.5 * x
    return half + half * jnp.tanh(x * (k + (k * 0.044715) * (x * x)))


def _s5_params_kernel(*refs):
    for gi in range(refs[0].shape[0]):
        _s5_group_operators(gi, *refs)


def _s5_group_operators(gi, ldt_ref, arc_ref, aic_ref, arr_ref, air_ref, bre_ref, bim_ref, cre2_ref, cim2_ref,
                        d_ref, tr_ref, o_ref, a16r_ref, a16i_ref):
    dt = jnp.exp(ldt_ref[gi])
    ar, ai = arc_ref[gi], aic_ref[gi]
    mag = jnp.exp(ar * dt)
    lr, li = mag * jnp.cos(ai * dt), mag * jnp.sin(ai * dt)
    den = ar * ar + ai * ai
    nr = lr - 1.0
    coef_r = (nr * ar + li * ai) / den
    coef_i = (li * ar - nr * ai) / den
    bb_r = coef_r * bre_ref[gi] - coef_i * bim_ref[gi]
    bb_i = coef_r * bim_ref[gi] + coef_i * bre_ref[gi]
    arr, air = arr_ref[gi], air_ref[gi]
    magr = jnp.exp(arr * dt)
    lrr, lir = magr * jnp.cos(air * dt), magr * jnp.sin(air * dt)

    lane_t = lax.broadcasted_iota(jnp.int32, (S5_STATE, CHUNK_W), 1) // S5_GROUP_CH
    lane16 = lax.broadcasted_iota(jnp.int32, (S5_GROUP_CH, CHUNK_W), 1)
    row16 = lax.broadcasted_iota(jnp.int32, (S5_GROUP_CH, CHUNK_W), 0)
    lo128 = lax.broadcasted_iota(jnp.int32, (S5_GROUP_CH, 2 * S5_STATE), 1) < S5_STATE
    c_cat = jnp.where(lo128, cre2_ref[gi], -cim2_ref[gi])
    d_col = d_ref[gi]

    expo = (CHUNK - 1) - lane_t
    pw_r = jnp.ones((S5_STATE, CHUNK_W), F32)
    pw_i = jnp.zeros((S5_STATE, CHUNK_W), F32)
    sq_r, sq_i = lr, li
    for bit in range(CHUNK.bit_length() - 1):
        take = ((expo >> bit) & 1) == 1
        pw_r, pw_i = (jnp.where(take, pw_r * sq_r - pw_i * sq_i, pw_r),
                      jnp.where(take, pw_r * sq_i + pw_i * sq_r, pw_i))
        sq_r, sq_i = sq_r * sq_r - sq_i * sq_i, 2.0 * sq_r * sq_i
    a16r_ref[gi] = jnp.broadcast_to(sq_r, (S5_STATE, LANES))
    a16i_ref[gi] = jnp.broadcast_to(sq_i, (S5_STATE, LANES))
    m_r = pw_r * bb_r - pw_i * bb_i
    m_i = pw_r * bb_i + pw_i * bb_r
    tr_ref[gi, CHUNK_W:CHUNK_W + S5_STATE, :] = m_r.astype(BF16)
    tr_ref[gi, CHUNK_W + S5_STATE:CHUNK_W + 2 * S5_STATE, :] = m_i.astype(BF16)
    k_all = jnp.dot(c_cat, jnp.concatenate([m_r, m_i], axis=0), preferred_element_type=F32, precision=HI)
    w_r, w_i = cre2_ref[gi], cim2_ref[gi]
    for tq in range(CHUNK):
        shift = (CHUNK - 1 - tq) * S5_GROUP_CH
        t_rows = k_all if shift == 0 else pltpu.roll(k_all, CHUNK_W - shift, axis=1)
        t_rows = (jnp.where(lane16 < (tq + 1) * S5_GROUP_CH, t_rows, 0.0)
                  + jnp.where(lane16 == tq * S5_GROUP_CH + row16, d_col, 0.0))
        tr_ref[gi, tq * S5_GROUP_CH:(tq + 1) * S5_GROUP_CH, :] = t_rows.astype(BF16)
        w_r, w_i = w_r * lrr - w_i * lir, w_r * lir + w_i * lrr
        o_ref[gi, tq * S5_GROUP_CH:(tq + 1) * S5_GROUP_CH, :] = jnp.where(lo128, w_r, -w_i).astype(BF16)


def _s5_params(s5_log_dt, s5_a_re, s5_a_im, s5_b_re, s5_b_im, s5_c_re, s5_c_im, s5_d):
    g, p, c = S5_GROUPS, S5_STATE, S5_GROUP_CH
    args = (
        s5_log_dt.reshape(g, 1, 1),
        s5_a_re.reshape(g, p, 1), s5_a_im.reshape(g, p, 1),
        jnp.tile(s5_a_re.reshape(g, 1, p), (1, 1, 2)), jnp.tile(s5_a_im.reshape(g, 1, p), (1, 1, 2)),
        jnp.tile(s5_b_re, (1, 1, CHUNK)), jnp.tile(s5_b_im, (1, 1, CHUNK)),
        jnp.tile(s5_c_re, (1, 1, 2)), jnp.tile(s5_c_im, (1, 1, 2)),
        s5_d.reshape(g, c, 1),
    )
    per_step = 4
    in_specs = [pl.BlockSpec((per_step,) + a.shape[1:], lambda i: (i, 0, 0)) for a in args]
    out_shape = (
        jax.ShapeDtypeStruct((g, CHUNK_W + 2 * p, CHUNK_W), BF16),
        jax.ShapeDtypeStruct((g, CHUNK_W, 2 * p), BF16),
        jax.ShapeDtypeStruct((g, p, LANES), F32),
        jax.ShapeDtypeStruct((g, p, LANES), F32),
    )
    out_specs = tuple(pl.BlockSpec((per_step,) + s.shape[1:], lambda i: (i, 0, 0)) for s in out_shape)
    return pl.pallas_call(
        _s5_params_kernel, grid=(g // per_step,), in_specs=in_specs, out_specs=out_specs, out_shape=out_shape,
        compiler_params=_cparams(1), name="s5_params")(*args)


def _pack3(v):
    hi = v.astype(BF16).astype(F32)
    rest = v - hi
    mid = rest.astype(BF16).astype(F32)
    lo = rest - mid
    return (hi + pltpu.roll(mid, N_FOX_HEADS, axis=1) + pltpu.roll(lo, 2 * N_FOX_HEADS, axis=1)).astype(BF16)


def _in_proj_kernel(x_ref, g_ref, wuf_ref, wqk_ref, wvt_ref, qg_ref, kg_ref, bias_ref, place_ref,
                    q_ref, k_ref, vt_ref, xt_ref, cs_ref, z_ref, f_ref):
    seq = x_ref.shape[1]
    nck = seq // CHUNK
    n_slab = S5_WIDTH // LANES
    grp_per_slab = LANES // S5_GROUP_CH
    tiles = [slice(r * TOK_TILE, (r + 1) * TOK_TILE) for r in range(seq // TOK_TILE)]
    gate_blocks = seq // LANES
    head_lane = lax.broadcasted_iota(jnp.int32, (LANES, LANES), 1) < N_FOX_HEADS
    lower = (lax.broadcasted_iota(jnp.int32, (LANES, LANES), 0)
             >= lax.broadcasted_iota(jnp.int32, (LANES, LANES), 1)).astype(BF16)

    def normed(rows):
        return _rms(x_ref[0, rows, :], g_ref[...]).astype(BF16)

    for rows in tiles:
        z = _dot(normed(rows), wuf_ref[...])
        for j in range(n_slab):
            z_ref[j, rows, :] = z[:, j * LANES:(j + 1) * LANES]
        f_ref[rows, :] = z[:, S5_WIDTH:]

    local = []
    for n, rows in enumerate(tiles):
        hn = normed(rows)
        qk = _dot(hn, wqk_ref[...])
        vt = _dot_nt(wvt_ref[...], hn)
        for t in range(n * CHUNK // len(tiles), (n + 1) * CHUNK // len(tiles)):
            for j in range(n_slab):
                blk = z_ref[j, pl.ds(t, nck, stride=CHUNK), :]
                xt_ref[j * grp_per_slab:(j + 1) * grp_per_slab, t * S5_GROUP_CH:(t + 1) * S5_GROUP_CH, :] = (
                    blk.T.reshape(grp_per_slab, S5_GROUP_CH, nck).astype(BF16))
        if n == 0:
            for blk in range(gate_blocks):
                log2f = jnp.where(head_lane, _log_sigmoid(f_ref[blk * LANES:(blk + 1) * LANES, :] + bias_ref[...])
                                  * LOG2E, 0.0)
                part = _dot(lower, _pack3(log2f))
                c = (part + pltpu.roll(part, LANES - N_FOX_HEADS, axis=1)
                     + pltpu.roll(part, LANES - 2 * N_FOX_HEADS, axis=1))
                local.append(jnp.where(head_lane, c, 0.0))
        if n == min(1, len(tiles) - 1):
            carry = jnp.zeros((1, LANES), F32)
            for blk in range(gate_blocks):
                c = local[blk] + carry
                carry = c[LANES - 1:LANES, :]
                cs_ref[0, blk * LANES:(blk + 1) * LANES, :] = _dot(_pack3(c), place_ref[...]).astype(BF16)
        q = _head64_norm(qk[:, :FOX_WIDTH]) * (qg_ref[...] * (HEAD_DIM ** -0.5 * LOG2E))
        k = _head64_norm(qk[:, FOX_WIDTH:]) * kg_ref[...]
        q_ref[0, rows, :] = q.astype(BF16)
        k_ref[0, rows, :] = k.astype(BF16)
        vt_ref[0, :, rows] = vt.astype(BF16)


def _in_proj(x, norm_mix, w_uf, w_qk, w_vt, q_gain, k_gain, bias_row, place):
    b, l, d = x.shape
    nck = l // CHUNK
    full = lambda a: pl.BlockSpec(a.shape, lambda i: (0,) * a.ndim, pipeline_mode=pl.Buffered(1))
    tok = pl.BlockSpec((1, l, FOX_WIDTH), lambda i: (i, 0, 0))
    return pl.pallas_call(
        _in_proj_kernel, grid=(b,),
        in_specs=[pl.BlockSpec((1, l, d), lambda i: (i, 0, 0)), full(norm_mix), full(w_uf), full(w_qk),
                  full(w_vt), full(q_gain), full(k_gain), full(bias_row), full(place)],
        out_specs=(tok, tok, pl.BlockSpec((1, FOX_WIDTH, l), lambda i: (i, 0, 0)),
                   pl.BlockSpec((S5_GROUPS, CHUNK_W, nck), lambda i: (0, 0, i)), tok),
        out_shape=(jax.ShapeDtypeStruct((b, l, FOX_WIDTH), BF16),) * 2
        + (jax.ShapeDtypeStruct((b, FOX_WIDTH, l), BF16),
           jax.ShapeDtypeStruct((S5_GROUPS, CHUNK_W, b * nck), BF16),
           jax.ShapeDtypeStruct((b, l, FOX_WIDTH), BF16)),
        scratch_shapes=[pltpu.VMEM((S5_WIDTH // LANES, l, LANES), F32), pltpu.VMEM((l, LANES), F32)],
        compiler_params=pltpu.CompilerParams(dimension_semantics=("arbitrary",),
                                             vmem_limit_bytes=IN_PROJ_VMEM_LIMIT),
        name="in_proj")(x, norm_mix, w_uf, w_qk, w_vt, q_gain, k_gain, bias_row, place)


def _fox_kernel(q_ref, k_ref, cs_ref, vt_ref, o_ref, kaug_ref, s_ref):
    blk = ATT_BLOCK
    seq = q_ref.shape[1]
    lane_k = lax.broadcasted_iota(jnp.int32, (seq, LANES), 1)
    k2 = k_ref[0]
    cs = cs_ref[0]
    kaug_ref[0] = jnp.where(lane_k < HEAD_DIM, k2, cs)
    kaug_ref[1] = jnp.where(lane_k >= HEAD_DIM, k2, cs)

    lane_q = lax.broadcasted_iota(jnp.int32, (blk, LANES), 1)
    minus1 = [jnp.where((lane_q >= HEAD_DIM) & (lane_q < HEAD_DIM + 3), -1.0, 0.0).astype(BF16),
              jnp.where(lane_q < 3, -1.0, 0.0).astype(BF16)]
    own = [lane_q < HEAD_DIM, lane_q >= HEAD_DIM]
    key_pos = lax.broadcasted_iota(jnp.int32, (blk, blk), 0)
    qry_pos = lax.broadcasted_iota(jnp.int32, (blk, blk), 1)

    def scores(i):
        q2 = q_ref[0, i * blk:(i + 1) * blk, :]
        for hh in range(2):
            qa = jnp.where(own[hh], q2, minus1[hh])
            st = _dot_nt(kaug_ref[hh, 0:(i + 1) * blk, :], qa)
            if i > 0:
                s_ref[i % 2, hh, 0:i * blk, :] = st[:i * blk]
            s_ref[i % 2, hh, i * blk:(i + 1) * blk, :] = jnp.where(key_pos > qry_pos, NEG_BIG, st[i * blk:])

    def attend(i):
        outs = []
        for hh in range(2):
            tile = lambda j: s_ref[i % 2, hh, j * blk:(j + 1) * blk, :]
            m = jnp.max(tile(0), axis=0, keepdims=True)
            for j in range(1, i + 1):
                m = jnp.maximum(m, jnp.max(tile(j), axis=0, keepdims=True))
            l = jnp.zeros((1, blk), F32)
            acc = jnp.zeros((HEAD_DIM, blk), F32)
            for j in range(i + 1):
                p = jnp.exp2(tile(j) - m)
                l = l + jnp.sum(p, axis=0, keepdims=True)
                acc = acc + _dot(vt_ref[0, hh * HEAD_DIM:(hh + 1) * HEAD_DIM, j * blk:(j + 1) * blk],
                                 p.astype(BF16))
            outs.append(acc / l)
        o_ref[0, i * blk:(i + 1) * blk, :] = jnp.concatenate(outs, axis=0).T.astype(BF16)

    n_blk = seq // blk
    scores(0)
    for i in range(n_blk):
        if i + 1 < n_blk:
            scores(i + 1)
        attend(i)


def _fox_attn(q, k, cs, vt):
    b, l, _ = q.shape
    n_pairs = FOX_WIDTH // LANES
    tok = pl.BlockSpec((1, l, LANES), lambda i, j: (i, 0, j))
    return pl.pallas_call(
        _fox_kernel, grid=(b, n_pairs),
        in_specs=[tok, tok, tok, pl.BlockSpec((1, LANES, l), lambda i, j: (i, j, 0))],
        out_specs=tok,
        out_shape=jax.ShapeDtypeStruct((b, l, FOX_WIDTH), BF16),
        scratch_shapes=[pltpu.VMEM((2, l, LANES), BF16), pltpu.VMEM((2, 2, l, ATT_BLOCK), F32)],
        compiler_params=_cparams(2), name="fox_attn")(q, k, cs, vt)


def _s5_chunk_kernel(xt_ref, tr_ref, o_ref, a16r_ref, a16i_ref, yg_ref, *, chunks_per_seq):
    lane_tile = LANES
    n_tiles = xt_ref.shape[2] // lane_tile
    n_levels = chunks_per_seq.bit_length() - 1
    reps = lane_tile // LANES
    kpos = lax.broadcasted_iota(jnp.int32, (S5_STATE, LANES), 1) % chunks_per_seq
    a_r, a_i = a16r_ref[0], a16i_ref[0]
    coef = []
    for lvl in range(n_levels):
        keep = kpos >= (1 << lvl)
        coef.append((jnp.concatenate([jnp.where(keep, a_r, 0.0)] * reps, axis=1),
                     jnp.concatenate([jnp.where(keep, a_i, 0.0)] * reps, axis=1)))
        a_r, a_i = a_r * a_r - a_i * a_i, 2.0 * a_r * a_i
    first = jnp.concatenate([kpos] * reps, axis=1) == 0
    stage1 = [_dot(tr_ref[0], xt_ref[0, :, n * lane_tile:(n + 1) * lane_tile]) for n in range(n_tiles)]
    states = [(yr[CHUNK_W:CHUNK_W + S5_STATE], yr[CHUNK_W + S5_STATE:]) for yr in stage1]
    for lvl in range(n_levels):
        c_r, c_i = coef[lvl]
        rolled = [(pltpu.roll(s_r, 1 << lvl, axis=1), pltpu.roll(s_i, 1 << lvl, axis=1)) for s_r, s_i in states]
        states = [(s_r + (c_r * p_r - c_i * p_i), s_i + (c_r * p_i + c_i * p_r))
                  for (s_r, s_i), (p_r, p_i) in zip(states, rolled)]
    entering = [jnp.concatenate([jnp.where(first, 0.0, pltpu.roll(s_r, 1, axis=1)),
                                 jnp.where(first, 0.0, pltpu.roll(s_i, 1, axis=1))], axis=0).astype(BF16)
                for s_r, s_i in states]
    carried = [_dot(o_ref[0], s_in) for s_in in entering]
    for n in range(n_tiles):
        yg = _gelu_tanh(stage1[n][:CHUNK_W] + carried[n]).astype(BF16)
        yg_ref[:, :, n * lane_tile:(n + 1) * lane_tile] = yg.reshape(CHUNK, S5_GROUP_CH, lane_tile)


def _s5_chunk(xt, tr_op, o_op, a16r, a16i, chunks_per_seq):
    g, _, nb = xt.shape
    per_g = lambda a: pl.BlockSpec((1,) + a.shape[1:], lambda i: (i, 0, 0))
    return pl.pallas_call(
        functools.partial(_s5_chunk_kernel, chunks_per_seq=chunks_per_seq), grid=(g,),
        in_specs=[per_g(xt), per_g(tr_op), per_g(o_op), per_g(a16r), per_g(a16i)],
        out_specs=pl.BlockSpec((CHUNK, S5_GROUP_CH, nb), lambda i: (0, i, 0)),
        out_shape=jax.ShapeDtypeStruct((CHUNK, S5_WIDTH, nb), BF16),
        compiler_params=_cparams(1), name="s5_chunk")(xt, tr_op, o_op, a16r, a16i)


def _s5_post_kernel(yg_ref, wgt_ref, bg_ref, gn_ref, o_ref):
    nck = yg_ref.shape[2]
    bias = jnp.concatenate([bg_ref[...]] * 2, axis=1)
    gain = jnp.concatenate([gn_ref[...]] * 2, axis=1)
    for t in range(0, CHUNK, 2):
        y = jnp.concatenate([yg_ref[t], yg_ref[t + 1]], axis=1)
        z = y.astype(F32) * _sigmoid(_dot(wgt_ref[...], y) + bias)
        zn = z * lax.rsqrt(jnp.mean(z * z, axis=0, keepdims=True) + EPS) * gain
        o_ref[0, :, t, :] = zn[:, :nck].T
        o_ref[0, :, t + 1, :] = zn[:, nck:].T


def _s5_post(yg, w_glu_t, b_glu_col, gain_col, batch):
    _, _, nb = yg.shape
    nck = nb // batch
    full = lambda a: pl.BlockSpec(a.shape, lambda i: (0,) * a.ndim)
    return pl.pallas_call(
        _s5_post_kernel, grid=(batch,),
        in_specs=[pl.BlockSpec((CHUNK, S5_WIDTH, nck), lambda i: (0, 0, i)),
                  full(w_glu_t), full(b_glu_col), full(gain_col)],
        out_specs=pl.BlockSpec((1, nck, CHUNK, S5_WIDTH), lambda i: (i, 0, 0, 0)),
        out_shape=jax.ShapeDtypeStruct((batch, nck, CHUNK, S5_WIDTH), F32),
        compiler_params=_cparams(1), name="s5_post")(yg, w_glu_t, b_glu_col, gain_col)


def _mix_cross_kernel(x_ref, fox_ref, s5_ref, gf_ref, wo_ref, gc_ref, wxq_ref, gq_ref, xk_ref, xv_ref, wxo_ref,
                      o_ref):
    halves = _row_halves(x_ref.shape[1])
    heads = [slice(hd * X_HEAD_DIM, (hd + 1) * X_HEAD_DIM) for hd in range(N_X_HEADS)]
    mixed = [jnp.concatenate([_rms(fox_ref[0, r, :].astype(F32), gf_ref[...]).astype(BF16),
                              s5_ref[0, r, :].astype(BF16)], axis=-1) for r in halves]
    proj = [_dot(m, wo_ref[...]) for m in mixed]
    h1 = [x_ref[0, r, :] + pr for r, pr in zip(halves, proj)]
    xq_raw = [_dot(_rms(h, gc_ref[...]).astype(BF16), wxq_ref[...]) for h in h1]
    xq = [(_head256_norm(v) * (gq_ref[...] * X_HEAD_DIM ** -0.5)).astype(BF16) for v in xq_raw]
    scores = [[_dot_nt(q[:, sl], xk_ref[0, :, sl]) for sl in heads] for q in xq]
    xo = []
    for sc in scores:
        outs = []
        for s, sl in zip(sc, heads):
            p = jnp.exp(s - jnp.max(s, axis=-1, keepdims=True))
            p = p / jnp.sum(p, axis=-1, keepdims=True)
            outs.append(_dot(p.astype(BF16), xv_ref[0, :, sl]).astype(BF16))
        xo.append(jnp.concatenate(outs, axis=-1))
    for r, h, v in zip(halves, h1, xo):
        o_ref[0, r, :] = h + _dot(v, wxo_ref[...])


def _mix_cross(x, fox, s5n, g_fox, w_out, g_cross, w_xq, g_xq, xk, xv, w_xo):
    b, l, d = x.shape
    tm = 2 * TOK_TILE
    tok = lambda w: pl.BlockSpec((1, tm, w), lambda i, j: (i, j, 0))
    full = lambda a: pl.BlockSpec(a.shape, lambda i, j: (0,) * a.ndim)
    mem = pl.BlockSpec((1, N_MEM, d), lambda i, j: (i, 0, 0))
    return pl.pallas_call(
        _mix_cross_kernel, grid=(b, l // tm),
        in_specs=[tok(d), tok(FOX_WIDTH), tok(S5_WIDTH), full(g_fox), full(w_out), full(g_cross),
                  full(w_xq), full(g_xq), mem, mem, full(w_xo)],
        out_specs=tok(d),
        out_shape=jax.ShapeDtypeStruct((b, l, d), F32),
        compiler_params=_cparams(2), name="mix_cross")(x, fox, s5n, g_fox, w_out, g_cross, w_xq, g_xq, xk, xv,
                                                        w_xo)


def _mem_kv_kernel(mem_ref, g_ref, wkv_ref, gk_ref, xk_ref, xv_ref):
    mn = _rms(mem_ref[0], g_ref[...]).astype(BF16)
    kv = _dot(mn, wkv_ref[...])
    xk_ref[0] = (_head256_norm(kv[:, :D_MODEL]) * gk_ref[...]).astype(BF16)
    xv_ref[0] = kv[:, D_MODEL:].astype(BF16)


def _mem_kv(mem, g_mem, w_xkv, g_xk):
    b, n, d = mem.shape
    full = lambda a: pl.BlockSpec(a.shape, lambda i: (0,) * a.ndim)
    blk = pl.BlockSpec((1, n, d), lambda i: (i, 0, 0))
    return pl.pallas_call(
        _mem_kv_kernel, grid=(b,),
        in_specs=[blk, full(g_mem), full(w_xkv), full(g_xk)],
        out_specs=(blk, blk),
        out_shape=(jax.ShapeDtypeStruct((b, n, d), BF16),) * 2,
        compiler_params=_cparams(1), name="mem_kv")(mem, g_mem, w_xkv, g_xk)


def _conv_ffn_kernel(h_ref, g_ref, wg_ref, wu_ref, cw_ref, cb_ref, wd_ref, o_ref, tail_ref, act_ref):
    @pl.when(pl.program_id(1) == 0)
    def _():
        tail_ref[...] = jnp.zeros_like(tail_ref)

    h = h_ref[0]
    tm = h.shape[0]
    hn = _rms(h, g_ref[...]).astype(BF16)
    row = lax.broadcasted_iota(jnp.int32, (tm, FF_CHUNK), 0)
    for c in range(D_FF // FF_CHUNK):
        sl = slice(c * FF_CHUNK, (c + 1) * FF_CHUNK)
        gate = _dot(hn, wg_ref[:, sl])
        up = _dot(hn, wu_ref[:, sl])
        prev1 = tail_ref[1:2, sl]
        prev2 = tail_ref[0:1, sl]
        g1 = jnp.where(row == 0, prev1, pltpu.roll(gate, 1, axis=0))
        g2 = jnp.where(row == 0, prev2, jnp.where(row == 1, prev1, pltpu.roll(gate, 2, axis=0)))
        conv = cb_ref[:, sl] + cw_ref[0:1, sl] * g2 + cw_ref[1:2, sl] * g1 + cw_ref[2:3, sl] * gate
        tail_ref[0:2, sl] = gate[tm - 2:tm, :]
        act_ref[:, sl] = (conv * _sigmoid(conv) * up).astype(BF16)
    o_ref[0] = h + _dot(act_ref[...], wd_ref[...])


def _conv_ffn(h, g_ffn, w_gate, w_up, conv_w, conv_b, w_down):
    b, l, d = h.shape
    tm = 2 * TOK_TILE
    tok = pl.BlockSpec((1, tm, d), lambda i, j: (i, j, 0))
    full = lambda a: pl.BlockSpec(a.shape, lambda i, j: (0,) * a.ndim, pipeline_mode=pl.Buffered(1))
    return pl.pallas_call(
        _conv_ffn_kernel, grid=(b, l // tm),
        in_specs=[tok, full(g_ffn), full(w_gate), full(w_up), full(conv_w), full(conv_b), full(w_down)],
        out_specs=tok,
        out_shape=jax.ShapeDtypeStruct((b, l, d), F32),
        scratch_shapes=[pltpu.VMEM((8, D_FF), F32), pltpu.VMEM((tm, D_FF), BF16)],
        compiler_params=_cparams(2), name="conv_ffn")(h, g_ffn, w_gate, w_up, conv_w, conv_b, w_down)


def _gate_placement():
    place = np.zeros((LANES, FOX_WIDTH), np.float32)
    for h in range(N_FOX_HEADS):
        base = (h // 2) * LANES + (HEAD_DIM if h % 2 == 0 else 0)
        for term in range(3):
            place[term * N_FOX_HEADS + h, base + term] = 1.0
    return jnp.asarray(place, dtype=BF16)


def _layer(h, mem, p):
    b, l, d = h.shape
    nck = l // CHUNK
    row = lambda a: a.reshape(1, -1)
    col = lambda a: a.reshape(-1, 1)
    w_in = p["w_in"]
    qk_end, v_end = 2 * FOX_WIDTH, 3 * FOX_WIDTH
    w_qk = w_in[:, :qk_end].astype(BF16)
    w_vt = w_in[:, qk_end:v_end].T.astype(BF16)
    w_uf = jnp.concatenate(
        [w_in[:, v_end + N_FOX_HEADS:],
         jnp.pad(w_in[:, v_end:v_end + N_FOX_HEADS], ((0, 0), (0, LANES - N_FOX_HEADS)))], axis=1).astype(BF16)
    f_bias = jnp.pad(p["fox_f_bias"], (0, LANES - N_FOX_HEADS)).reshape(1, LANES)

    tr_op, o_op, a16r, a16i = _s5_params(
        p["s5_log_dt"], p["s5_a_re"], p["s5_a_im"], p["s5_b_re"], p["s5_b_im"],
        p["s5_c_re"], p["s5_c_im"], p["s5_d"])

    q, k, vt, xt, cs = _in_proj(h, row(p["norm_mix"]), w_uf, w_qk, w_vt,
                                row(jnp.tile(p["fox_q_norm"], N_FOX_HEADS)),
                                row(jnp.tile(p["fox_k_norm"], N_FOX_HEADS)), f_bias, _gate_placement())
    fox = _fox_attn(q, k, cs, vt)

    yg = _s5_chunk(xt, tr_op, o_op, a16r, a16i, nck)
    lane_bcast = lambda a: jnp.broadcast_to(col(a), (a.shape[0], LANES))
    s5n = _s5_post(yg, p["s5_w_glu"].T.astype(BF16), lane_bcast(p["s5_b_glu"]), lane_bcast(p["out_norm_s5"]), b)
    s5n = s5n.reshape(b, l, S5_WIDTH)

    xk, xv = _mem_kv(mem, row(p["norm_mem"]), p["w_xkv"].astype(BF16),
                     row(jnp.tile(p["xk_norm"], N_X_HEADS)))
    h2 = _mix_cross(h, fox, s5n, row(p["out_norm_fox"]), p["w_out"].astype(BF16),
                    row(p["norm_cross"]), p["w_xq"].astype(BF16),
                    row(jnp.tile(p["xq_norm"], N_X_HEADS)), xk, xv, p["w_xo"].astype(BF16))

    w_up = p["w_ffn_up"]
    return _conv_ffn(h2, row(p["norm_ffn"]), w_up[:, :D_FF].astype(BF16), w_up[:, D_FF:].astype(BF16),
                     p["ffn_conv_w"], row(p["ffn_conv_b"]), p["w_ffn_down"].astype(BF16))


_PARAM_NAMES = (
    "norm_mix", "w_in", "fox_q_norm", "fox_k_norm", "fox_f_bias", "s5_a_re", "s5_a_im", "s5_log_dt",
    "s5_b_re", "s5_b_im", "s5_c_re", "s5_c_im", "s5_d", "s5_w_glu", "s5_b_glu", "out_norm_fox",
    "out_norm_s5", "w_out", "norm_cross", "norm_mem", "w_xq", "w_xkv", "xq_norm", "xk_norm", "w_xo",
    "norm_ffn", "w_ffn_up", "ffn_conv_w", "ffn_conv_b", "w_ffn_down")


def kernel(x, mem, norm_mix, w_in, fox_q_norm, fox_k_norm, fox_f_bias, s5_a_re, s5_a_im, s5_log_dt, s5_b_re, s5_b_im, s5_c_re, s5_c_im, s5_d, s5_w_glu, s5_b_glu, out_norm_fox, out_norm_s5, w_out, norm_cross, norm_mem, w_xq, w_xkv, xq_norm, xk_norm, w_xo, norm_ffn, w_ffn_up, ffn_conv_w, ffn_conv_b, w_ffn_down):
    stacked = (norm_mix, w_in, fox_q_norm, fox_k_norm, fox_f_bias, s5_a_re, s5_a_im, s5_log_dt,
               s5_b_re, s5_b_im, s5_c_re, s5_c_im, s5_d, s5_w_glu, s5_b_glu, out_norm_fox,
               out_norm_s5, w_out, norm_cross, norm_mem, w_xq, w_xkv, xq_norm, xk_norm, w_xo,
               norm_ffn, w_ffn_up, ffn_conv_w, ffn_conv_b, w_ffn_down)
    h = x
    for layer in range(norm_mix.shape[0]):
        h = _layer(h, mem, {n: a[layer] for n, a in zip(_PARAM_NAMES, stacked)})
    return h
```

```python
import functools
import math

import jax
import jax.numpy as jnp
import numpy as np
from jax import lax
from jax.experimental import pallas as pl
from jax.experimental.pallas import tpu as pltpu

F32 = jnp.float32
BF16 = jnp.bfloat16

D_MODEL = 1024
N_MEM = 256
FOX_WIDTH = 512
HEAD_DIM = 64
N_FOX_HEADS = 8
S5_WIDTH = 512
S5_GROUP_CH = 16
S5_GROUPS = 32
S5_STATE = 64
N_X_HEADS = 4
X_HEAD_DIM = 256
D_FF = 2816
CONV_W = 3
EPS = 1e-6

LANES = 128
CHUNK = 16
CHUNK_W = CHUNK * S5_GROUP_CH
TOK_TILE = 512
ATT_BLOCK = 256
FF_CHUNK = 256
NEG_BIG = -1e30
LOG2E = math.log2(math.e)
HI = lax.Precision.HIGHEST
VMEM_LIMIT = 56 * 1024 * 1024
IN_PROJ_VMEM_LIMIT = 60 * 1024 * 1024

NT_DIMS = (((1,), (1,)), ((), ()))


def _cparams(n_axes):
    return pltpu.CompilerParams(dimension_semantics=("arbitrary",) * n_axes,
                                vmem_limit_bytes=VMEM_LIMIT)


def _rms(xf, g):
    return xf * lax.rsqrt(jnp.mean(xf * xf, axis=-1, keepdims=True) + EPS) * g


def _dot(a, b):
    return jnp.dot(a, b, preferred_element_type=F32)


def _dot_nt(a, b):
    return lax.dot_general(a, b, NT_DIMS, preferred_element_type=F32)


def _head64_norm(x):
    lo = lax.broadcasted_iota(jnp.int32, (x.shape[0], LANES), 1) < HEAD_DIM
    outs = []
    for j in range(x.shape[1] // LANES):
        blk = x[:, j * LANES:(j + 1) * LANES]
        sq = blk * blk
        s_lo = jnp.sum(jnp.where(lo, sq, 0.0), axis=-1, keepdims=True)
        s_hi = jnp.sum(jnp.where(lo, 0.0, sq), axis=-1, keepdims=True)
        inv = jnp.where(lo, lax.rsqrt(s_lo * (1.0 / HEAD_DIM) + EPS),
                        lax.rsqrt(s_hi * (1.0 / HEAD_DIM) + EPS))
        outs.append(blk * inv)
    return jnp.concatenate(outs, axis=-1)


def _head256_norm(x):
    outs = []
    for h in range(x.shape[1] // X_HEAD_DIM):
        blk = x[:, h * X_HEAD_DIM:(h + 1) * X_HEAD_DIM]
        outs.append(blk * lax.rsqrt(jnp.mean(blk * blk, axis=-1, keepdims=True) + EPS))
    return jnp.concatenate(outs, axis=-1)


def _row_halves(rows):
    return [slice(0, rows // 2), slice(rows // 2, rows)]


def _sigmoid(z):
    return 0.5 * jnp.tanh(0.5 * z) + 0.5


def _log_sigmoid(z):
    return jnp.minimum(z, 0.0) - jnp.log(1.0 + jnp.exp(-jnp.abs(z)))


def _gelu_tanh(x):
    k = math.sqrt(2.0 / math.pi)
    half = 0.5 * x
    return half + half * jnp.tanh(x * (k + (k * 0.044715) * (x * x)))


def _s5_params_kernel(*refs):
    for gi in range(refs[0].shape[0]):
        _s5_group_operators(gi, *refs)


def _s5_group_operators(gi, ldt_ref, arc_ref, aic_ref, arr_ref, air_ref, bre_ref, bim_ref, cre2_ref, cim2_ref,
                        d_ref, tr_ref, o_ref, a16r_ref, a16i_ref):
    dt = jnp.exp(ldt_ref[gi])
    ar, ai = arc_ref[gi], aic_ref[gi]
    mag = jnp.exp(ar * dt)
    lr, li = mag * jnp.cos(ai * dt), mag * jnp.sin(ai * dt)
    den = ar * ar + ai * ai
    nr = lr - 1.0
    coef_r = (nr * ar + li * ai) / den
    coef_i = (li * ar - nr * ai) / den
    bb_r = coef_r * bre_ref[gi] - coef_i * bim_ref[gi]
    bb_i = coef_r * bim_ref[gi] + coef_i * bre_ref[gi]
    arr, air = arr_ref[gi], air_ref[gi]
    magr = jnp.exp(arr * dt)
    lrr, lir = magr * jnp.cos(air * dt), magr * jnp.sin(air * dt)

    lane_t = lax.broadcasted_iota(jnp.int32, (S5_STATE, CHUNK_W), 1) // S5_GROUP_CH
    lane16 = lax.broadcasted_iota(jnp.int32, (S5_GROUP_CH, CHUNK_W), 1)
    row16 = lax.broadcasted_iota(jnp.int32, (S5_GROUP_CH, CHUNK_W), 0)
    lo128 = lax.broadcasted_iota(jnp.int32, (S5_GROUP_CH, 2 * S5_STATE), 1) < S5_STATE
    c_cat = jnp.where(lo128, cre2_ref[gi], -cim2_ref[gi])
    d_col = d_ref[gi]

    expo = (CHUNK - 1) - lane_t
    pw_r = jnp.ones((S5_STATE, CHUNK_W), F32)
    pw_i = jnp.zeros((S5_STATE, CHUNK_W), F32)
    sq_r, sq_i = lr, li
    for bit in range(CHUNK.bit_length() - 1):
        take = ((expo >> bit) & 1) == 1
        pw_r, pw_i = (jnp.where(take, pw_r * sq_r - pw_i * sq_i, pw_r),
                      jnp.where(take, pw_r * sq_i + pw_i * sq_r, pw_i))
        sq_r, sq_i = sq_r * sq_r - sq_i * sq_i, 2.0 * sq_r * sq_i
    a16r_ref[gi] = jnp.broadcast_to(sq_r, (S5_STATE, LANES))
    a16i_ref[gi] = jnp.broadcast_to(sq_i, (S5_STATE, LANES))
    m_r = pw_r * bb_r - pw_i * bb_i
    m_i = pw_r * bb_i + pw_i * bb_r
    tr_ref[gi, CHUNK_W:CHUNK_W + S5_STATE, :] = m_r.astype(BF16)
    tr_ref[gi, CHUNK_W + S5_STATE:CHUNK_W + 2 * S5_STATE, :] = m_i.astype(BF16)
    k_all = jnp.dot(c_cat, jnp.concatenate([m_r, m_i], axis=0), preferred_element_type=F32, precision=HI)
    w_r, w_i = cre2_ref[gi], cim2_ref[gi]
    for tq in range(CHUNK):
        shift = (CHUNK - 1 - tq) * S5_GROUP_CH
        t_rows = k_all if shift == 0 else pltpu.roll(k_all, CHUNK_W - shift, axis=1)
        t_rows = (jnp.where(lane16 < (tq + 1) * S5_GROUP_CH, t_rows, 0.0)
                  + jnp.where(lane16 == tq * S5_GROUP_CH + row16, d_col, 0.0))
        tr_ref[gi, tq * S5_GROUP_CH:(tq + 1) * S5_GROUP_CH, :] = t_rows.astype(BF16)
        w_r, w_i = w_r * lrr - w_i * lir, w_r * lir + w_i * lrr
        o_ref[gi, tq * S5_GROUP_CH:(tq + 1) * S5_GROUP_CH, :] = jnp.where(lo128, w_r, -w_i).astype(BF16)


def _s5_params(s5_log_dt, s5_a_re, s5_a_im, s5_b_re, s5_b_im, s5_c_re, s5_c_im, s5_d):
    g, p, c = S5_GROUPS, S5_STATE, S5_GROUP_CH
    args = (
        s5_log_dt.reshape(g, 1, 1),
        s5_a_re.reshape(g, p, 1), s5_a_im.reshape(g, p, 1),
        jnp.tile(s5_a_re.reshape(g, 1, p), (1, 1, 2)), jnp.tile(s5_a_im.reshape(g, 1, p), (1, 1, 2)),
        jnp.tile(s5_b_re, (1, 1, CHUNK)), jnp.tile(s5_b_im, (1, 1, CHUNK)),
        jnp.tile(s5_c_re, (1, 1, 2)), jnp.tile(s5_c_im, (1, 1, 2)),
        s5_d.reshape(g, c, 1),
    )
    per_step = 4
    in_specs = [pl.BlockSpec((per_step,) + a.shape[1:], lambda i: (i, 0, 0)) for a in args]
    out_shape = (
        jax.ShapeDtypeStruct((g, CHUNK_W + 2 * p, CHUNK_W), BF16),
        jax.ShapeDtypeStruct((g, CHUNK_W, 2 * p), BF16),
        jax.ShapeDtypeStruct((g, p, LANES), F32),
        jax.ShapeDtypeStruct((g, p, LANES), F32),
    )
    out_specs = tuple(pl.BlockSpec((per_step,) + s.shape[1:], lambda i: (i, 0, 0)) for s in out_shape)
    return pl.pallas_call(
        _s5_params_kernel, grid=(g // per_step,), in_specs=in_specs, out_specs=out_specs, out_shape=out_shape,
        compiler_params=_cparams(1), name="s5_params")(*args)


def _pack3(v):
    hi = v.astype(BF16).astype(F32)
    rest = v - hi
    mid = rest.astype(BF16).astype(F32)
    lo = rest - mid
    return (hi + pltpu.roll(mid, N_FOX_HEADS, axis=1) + pltpu.roll(lo, 2 * N_FOX_HEADS, axis=1)).astype(BF16)


def _in_proj_kernel(x_ref, g_ref, wuf_ref, wqk_ref, wvt_ref, qg_ref, kg_ref, bias_ref, place_ref,
                    q_ref, k_ref, vt_ref, xt_ref, cs_ref, z_ref, f_ref):
    seq = x_ref.shape[1]
    nck = seq // CHUNK
    n_slab = S5_WIDTH // LANES
    grp_per_slab = LANES // S5_GROUP_CH
    tiles = [slice(r * TOK_TILE, (r + 1) * TOK_TILE) for r in range(seq // TOK_TILE)]
    gate_blocks = seq // LANES
    head_lane = lax.broadcasted_iota(jnp.int32, (LANES, LANES), 1) < N_FOX_HEADS
    lower = (lax.broadcasted_iota(jnp.int32, (LANES, LANES), 0)
             >= lax.broadcasted_iota(jnp.int32, (LANES, LANES), 1)).astype(BF16)

    def normed(rows):
        return _rms(x_ref[0, rows, :], g_ref[...]).astype(BF16)

    for rows in tiles:
        z = _dot(normed(rows), wuf_ref[...])
        for j in range(n_slab):
            z_ref[j, rows, :] = z[:, j * LANES:(j + 1) * LANES]
        f_ref[rows, :] = z[:, S5_WIDTH:]

    local = []
    for n, rows in enumerate(tiles):
        hn = normed(rows)
        qk = _dot(hn, wqk_ref[...])
        vt = _dot_nt(wvt_ref[...], hn)
        for t in range(n * CHUNK // len(tiles), (n + 1) * CHUNK // len(tiles)):
            for j in range(n_slab):
                blk = z_ref[j, pl.ds(t, nck, stride=CHUNK), :]
                xt_ref[j * grp_per_slab:(j + 1) * grp_per_slab, t * S5_GROUP_CH:(t + 1) * S5_GROUP_CH, :] = (
                    blk.T.reshape(grp_per_slab, S5_GROUP_CH, nck).astype(BF16))
        if n == 0:
            for blk in range(gate_blocks):
                log2f = jnp.where(head_lane, _log_sigmoid(f_ref[blk * LANES:(blk + 1) * LANES, :] + bias_ref[...])
                                  * LOG2E, 0.0)
                part = _dot(lower, _pack3(log2f))
                c = (part + pltpu.roll(part, LANES - N_FOX_HEADS, axis=1)
                     + pltpu.roll(part, LANES - 2 * N_FOX_HEADS, axis=1))
                local.append(jnp.where(head_lane, c, 0.0))
        if n == min(1, len(tiles) - 1):
            carry = jnp.zeros((1, LANES), F32)
            for blk in range(gate_blocks):
                c = local[blk] + carry
                carry = c[LANES - 1:LANES, :]
                cs_ref[0, blk * LANES:(blk + 1) * LANES, :] = _dot(_pack3(c), place_ref[...]).astype(BF16)
        q = _head64_norm(qk[:, :FOX_WIDTH]) * (qg_ref[...] * (HEAD_DIM ** -0.5 * LOG2E))
        k = _head64_norm(qk[:, FOX_WIDTH:]) * kg_ref[...]
        q_ref[0, rows, :] = q.astype(BF16)
        k_ref[0, rows, :] = k.astype(BF16)
        vt_ref[0, :, rows] = vt.astype(BF16)


def _in_proj(x, norm_mix, w_uf, w_qk, w_vt, q_gain, k_gain, bias_row, place):
    b, l, d = x.shape
    nck = l // CHUNK
    full = lambda a: pl.BlockSpec(a.shape, lambda i: (0,) * a.ndim, pipeline_mode=pl.Buffered(1))
    tok = pl.BlockSpec((1, l, FOX_WIDTH), lambda i: (i, 0, 0))
    return pl.pallas_call(
        _in_proj_kernel, grid=(b,),
        in_specs=[pl.BlockSpec((1, l, d), lambda i: (i, 0, 0)), full(norm_mix), full(w_uf), full(w_qk),
                  full(w_vt), full(q_gain), full(k_gain), full(bias_row), full(place)],
        out_specs=(tok, tok, pl.BlockSpec((1, FOX_WIDTH, l), lambda i: (i, 0, 0)),
                   pl.BlockSpec((S5_GROUPS, CHUNK_W, nck), lambda i: (0, 0, i)), tok),
        out_shape=(jax.ShapeDtypeStruct((b, l, FOX_WIDTH), BF16),) * 2
        + (jax.ShapeDtypeStruct((b, FOX_WIDTH, l), BF16),
           jax.ShapeDtypeStruct((S5_GROUPS, CHUNK_W, b * nck), BF16),
           jax.ShapeDtypeStruct((b, l, FOX_WIDTH), BF16)),
        scratch_shapes=[pltpu.VMEM((S5_WIDTH // LANES, l, LANES), F32), pltpu.VMEM((l, LANES), F32)],
        compiler_params=pltpu.CompilerParams(dimension_semantics=("arbitrary",),
                                             vmem_limit_bytes=IN_PROJ_VMEM_LIMIT),
        name="in_proj")(x, norm_mix, w_uf, w_qk, w_vt, q_gain, k_gain, bias_row, place)


def _fox_kernel(q_ref, k_ref, cs_ref, vt_ref, *rest, n_side):
    side_in, o_ref, side_out = rest[:n_side], rest[n_side], rest[n_side + 1:2 * n_side + 1]
    kaug_ref, s_ref = rest[2 * n_side + 1:]
    for w_ref, wb_ref in zip(side_in, side_out):
        wb_ref[...] = w_ref[...].astype(BF16)

    blk = ATT_BLOCK
    seq = q_ref.shape[1]
    lane_k = lax.broadcasted_iota(jnp.int32, (seq, LANES), 1)
    k2 = k_ref[0]
    cs = cs_ref[0]
    kaug_ref[0] = jnp.where(lane_k < HEAD_DIM, k2, cs)
    kaug_ref[1] = jnp.where(lane_k >= HEAD_DIM, k2, cs)

    lane_q = lax.broadcasted_iota(jnp.int32, (blk, LANES), 1)
    minus1 = [jnp.where((lane_q >= HEAD_DIM) & (lane_q < HEAD_DIM + 3), -1.0, 0.0).astype(BF16),
              jnp.where(lane_q < 3, -1.0, 0.0).astype(BF16)]
    own = [lane_q < HEAD_DIM, lane_q >= HEAD_DIM]
    key_pos = lax.broadcasted_iota(jnp.int32, (blk, blk), 0)
    qry_pos = lax.broadcasted_iota(jnp.int32, (blk, blk), 1)

    def scores(i):
        q2 = q_ref[0, i * blk:(i + 1) * blk, :]
        for hh in range(2):
            qa = jnp.where(own[hh], q2, minus1[hh])
            st = _dot_nt(kaug_ref[hh, 0:(i + 1) * blk, :], qa)
            if i > 0:
                s_ref[i % 2, hh, 0:i * blk, :] = st[:i * blk]
            s_ref[i % 2, hh, i * blk:(i + 1) * blk, :] = jnp.where(key_pos > qry_pos, NEG_BIG, st[i * blk:])

    def attend(i):
        outs = []
        for hh in range(2):
            tile = lambda j: s_ref[i % 2, hh, j * blk:(j + 1) * blk, :]
            m = jnp.max(tile(0), axis=0, keepdims=True)
            for j in range(1, i + 1):
                m = jnp.maximum(m, jnp.max(tile(j), axis=0, keepdims=True))
            l = jnp.zeros((1, blk), F32)
            acc = jnp.zeros((HEAD_DIM, blk), F32)
            for j in range(i + 1):
                p = jnp.exp2(tile(j) - m)
                l = l + jnp.sum(p, axis=0, keepdims=True)
                acc = acc + _dot(vt_ref[0, hh * HEAD_DIM:(hh + 1) * HEAD_DIM, j * blk:(j + 1) * blk],
                                 p.astype(BF16))
            outs.append(acc / l)
        o_ref[0, i * blk:(i + 1) * blk, :] = jnp.concatenate(outs, axis=0).T.astype(BF16)

    n_blk = seq // blk
    scores(0)
    for i in range(n_blk):
        if i + 1 < n_blk:
            scores(i + 1)
        attend(i)


def _cast_row_block(rows, steps):
    block = 16 * pl.cdiv(pl.cdiv(rows, steps), 16)
    while rows % block:
        block += 16
    return block


def _fox_attn(q, k, cs, vt, side_weights):
    b, l, _ = q.shape
    n_pairs = FOX_WIDTH // LANES
    tok = pl.BlockSpec((1, l, LANES), lambda i, j: (i, 0, j))

    def side_spec(w):
        block = _cast_row_block(w.shape[0], b * n_pairs)
        last = w.shape[0] // block - 1
        return pl.BlockSpec((block, w.shape[1]), lambda i, j: (jnp.minimum(i * n_pairs + j, last), 0))

    side_specs = [side_spec(w) for w in side_weights]
    outs = pl.pallas_call(
        functools.partial(_fox_kernel, n_side=len(side_weights)), grid=(b, n_pairs),
        in_specs=[tok, tok, tok, pl.BlockSpec((1, LANES, l), lambda i, j: (i, j, 0))] + side_specs,
        out_specs=[tok] + side_specs,
        out_shape=[jax.ShapeDtypeStruct((b, l, FOX_WIDTH), BF16)]
        + [jax.ShapeDtypeStruct(w.shape, BF16) for w in side_weights],
        scratch_shapes=[pltpu.VMEM((2, l, LANES), BF16), pltpu.VMEM((2, 2, l, ATT_BLOCK), F32)],
        compiler_params=_cparams(2), name="fox_attn")(q, k, cs, vt, *side_weights)
    return outs[0], outs[1:]


def _s5_chunk_kernel(xt_ref, tr_ref, o_ref, a16r_ref, a16i_ref, yg_ref, *, chunks_per_seq):
    lane_tile = LANES
    n_tiles = xt_ref.shape[2] // lane_tile
    n_levels = chunks_per_seq.bit_length() - 1
    reps = lane_tile // LANES
    kpos = lax.broadcasted_iota(jnp.int32, (S5_STATE, LANES), 1) % chunks_per_seq
    a_r, a_i = a16r_ref[0], a16i_ref[0]
    coef = []
    for lvl in range(n_levels):
        keep = kpos >= (1 << lvl)
        coef.append((jnp.concatenate([jnp.where(keep, a_r, 0.0)] * reps, axis=1),
                     jnp.concatenate([jnp.where(keep, a_i, 0.0)] * reps, axis=1)))
        a_r, a_i = a_r * a_r - a_i * a_i, 2.0 * a_r * a_i
    first = jnp.concatenate([kpos] * reps, axis=1) == 0
    stage1 = [_dot(tr_ref[0], xt_ref[0, :, n * lane_tile:(n + 1) * lane_tile]) for n in range(n_tiles)]
    states = [(yr[CHUNK_W:CHUNK_W + S5_STATE], yr[CHUNK_W + S5_STATE:]) for yr in stage1]
    for lvl in range(n_levels):
        c_r, c_i = coef[lvl]
        rolled = [(pltpu.roll(s_r, 1 << lvl, axis=1), pltpu.roll(s_i, 1 << lvl, axis=1)) for s_r, s_i in states]
        states = [(s_r + (c_r * p_r - c_i * p_i), s_i + (c_r * p_i + c_i * p_r))
                  for (s_r, s_i), (p_r, p_i) in zip(states, rolled)]
    entering = [jnp.concatenate([jnp.where(first, 0.0, pltpu.roll(s_r, 1, axis=1)),
                                 jnp.where(first, 0.0, pltpu.roll(s_i, 1, axis=1))], axis=0).astype(BF16)
                for s_r, s_i in states]
    carried = [_dot(o_ref[0], s_in) for s_in in entering]
    for n in range(n_tiles):
        yg = _gelu_tanh(stage1[n][:CHUNK_W] + carried[n]).astype(BF16)
        yg_ref[:, :, n * lane_tile:(n + 1) * lane_tile] = yg.reshape(CHUNK, S5_GROUP_CH, lane_tile)


def _s5_chunk(xt, tr_op, o_op, a16r, a16i, chunks_per_seq):
    g, _, nb = xt.shape
    per_g = lambda a: pl.BlockSpec((1,) + a.shape[1:], lambda i: (i, 0, 0))
    return pl.pallas_call(
        functools.partial(_s5_chunk_kernel, chunks_per_seq=chunks_per_seq), grid=(g,),
        in_specs=[per_g(xt), per_g(tr_op), per_g(o_op), per_g(a16r), per_g(a16i)],
        out_specs=pl.BlockSpec((CHUNK, S5_GROUP_CH, nb), lambda i: (0, i, 0)),
        out_shape=jax.ShapeDtypeStruct((CHUNK, S5_WIDTH, nb), BF16),
        compiler_params=_cparams(1), name="s5_chunk")(xt, tr_op, o_op, a16r, a16i)


def _s5_post_kernel(yg_ref, wgt_ref, bg_ref, gn_ref, o_ref):
    nck = yg_ref.shape[2]
    bias = jnp.concatenate([bg_ref[...]] * 2, axis=1)
    gain = jnp.concatenate([gn_ref[...]] * 2, axis=1)
    for t in range(0, CHUNK, 2):
        y = jnp.concatenate([yg_ref[t], yg_ref[t + 1]], axis=1)
        z = y.astype(F32) * _sigmoid(_dot(wgt_ref[...], y) + bias)
        zn = z * lax.rsqrt(jnp.mean(z * z, axis=0, keepdims=True) + EPS) * gain
        o_ref[0, :, t, :] = zn[:, :nck].T
        o_ref[0, :, t + 1, :] = zn[:, nck:].T


def _s5_post(yg, w_glu_t, b_glu_col, gain_col, batch):
    _, _, nb = yg.shape
    nck = nb // batch
    full = lambda a: pl.BlockSpec(a.shape, lambda i: (0,) * a.ndim)
    return pl.pallas_call(
        _s5_post_kernel, grid=(batch,),
        in_specs=[pl.BlockSpec((CHUNK, S5_WIDTH, nck), lambda i: (0, 0, i)),
                  full(w_glu_t), full(b_glu_col), full(gain_col)],
        out_specs=pl.BlockSpec((1, nck, CHUNK, S5_WIDTH), lambda i: (i, 0, 0, 0)),
        out_shape=jax.ShapeDtypeStruct((batch, nck, CHUNK, S5_WIDTH), F32),
        compiler_params=_cparams(1), name="s5_post")(yg, w_glu_t, b_glu_col, gain_col)


def _mix_cross_kernel(x_ref, fox_ref, s5_ref, gf_ref, wo_ref, gc_ref, wxq_ref, gq_ref, xk_ref, xv_ref, wxo_ref,
                      o_ref):
    halves = _row_halves(x_ref.shape[1])
    heads = [slice(hd * X_HEAD_DIM, (hd + 1) * X_HEAD_DIM) for hd in range(N_X_HEADS)]
    mixed = [jnp.concatenate([_rms(fox_ref[0, r, :].astype(F32), gf_ref[...]).astype(BF16),
                              s5_ref[0, r, :].astype(BF16)], axis=-1) for r in halves]
    proj = [_dot(m, wo_ref[...]) for m in mixed]
    h1 = [x_ref[0, r, :] + pr for r, pr in zip(halves, proj)]
    xq_raw = [_dot(_rms(h, gc_ref[...]).astype(BF16), wxq_ref[...]) for h in h1]
    xq = [(_head256_norm(v) * (gq_ref[...] * X_HEAD_DIM ** -0.5)).astype(BF16) for v in xq_raw]
    scores = [[_dot_nt(q[:, sl], xk_ref[0, :, sl]) for sl in heads] for q in xq]
    xo = []
    for sc in scores:
        outs = []
        for s, sl in zip(sc, heads):
            p = jnp.exp(s - jnp.max(s, axis=-1, keepdims=True))
            p = p / jnp.sum(p, axis=-1, keepdims=True)
            outs.append(_dot(p.astype(BF16), xv_ref[0, :, sl]).astype(BF16))
        xo.append(jnp.concatenate(outs, axis=-1))
    for r, h, v in zip(halves, h1, xo):
        o_ref[0, r, :] = h + _dot(v, wxo_ref[...])


def _mix_cross(x, fox, s5n, g_fox, w_out, g_cross, w_xq, g_xq, xk, xv, w_xo):
    b, l, d = x.shape
    tm = 2 * TOK_TILE
    tok = lambda w: pl.BlockSpec((1, tm, w), lambda i, j: (i, j, 0))
    full = lambda a: pl.BlockSpec(a.shape, lambda i, j: (0,) * a.ndim)
    mem = pl.BlockSpec((1, N_MEM, d), lambda i, j: (i, 0, 0))
    return pl.pallas_call(
        _mix_cross_kernel, grid=(b, l // tm),
        in_specs=[tok(d), tok(FOX_WIDTH), tok(S5_WIDTH), full(g_fox), full(w_out), full(g_cross),
                  full(w_xq), full(g_xq), mem, mem, full(w_xo)],
        out_specs=tok(d),
        out_shape=jax.ShapeDtypeStruct((b, l, d), F32),
        compiler_params=_cparams(2), name="mix_cross")(x, fox, s5n, g_fox, w_out, g_cross, w_xq, g_xq, xk, xv,
                                                        w_xo)


def _mem_kv_kernel(mem_ref, g_ref, wkv_ref, gk_ref, xk_ref, xv_ref):
    mn = _rms(mem_ref[0], g_ref[...]).astype(BF16)
    kv = _dot(mn, wkv_ref[...])
    xk_ref[0] = (_head256_norm(kv[:, :D_MODEL]) * gk_ref[...]).astype(BF16)
    xv_ref[0] = kv[:, D_MODEL:].astype(BF16)


def _mem_kv(mem, g_mem, w_xkv, g_xk):
    b, n, d = mem.shape
    full = lambda a: pl.BlockSpec(a.shape, lambda i: (0,) * a.ndim)
    blk = pl.BlockSpec((1, n, d), lambda i: (i, 0, 0))
    return pl.pallas_call(
        _mem_kv_kernel, grid=(b,),
        in_specs=[blk, full(g_mem), full(w_xkv), full(g_xk)],
        out_specs=(blk, blk),
        out_shape=(jax.ShapeDtypeStruct((b, n, d), BF16),) * 2,
        compiler_params=_cparams(1), name="mem_kv")(mem, g_mem, w_xkv, g_xk)


def _conv_ffn_kernel(h_ref, g_ref, wg_ref, wu_ref, cw_ref, cb_ref, wd_ref, o_ref, tail_ref, act_ref):
    @pl.when(pl.program_id(1) == 0)
    def _():
        tail_ref[...] = jnp.zeros_like(tail_ref)

    h = h_ref[0]
    tm = h.shape[0]
    hn = _rms(h, g_ref[...]).astype(BF16)
    row = lax.broadcasted_iota(jnp.int32, (tm, FF_CHUNK), 0)
    for c in range(D_FF // FF_CHUNK):
        sl = slice(c * FF_CHUNK, (c + 1) * FF_CHUNK)
        gate = _dot(hn, wg_ref[:, sl])
        up = _dot(hn, wu_ref[:, sl])
        prev1 = tail_ref[1:2, sl]
        prev2 = tail_ref[0:1, sl]
        g1 = jnp.where(row == 0, prev1, pltpu.roll(gate, 1, axis=0))
        g2 = jnp.where(row == 0, prev2, jnp.where(row == 1, prev1, pltpu.roll(gate, 2, axis=0)))
        conv = cb_ref[:, sl] + cw_ref[0:1, sl] * g2 + cw_ref[1:2, sl] * g1 + cw_ref[2:3, sl] * gate
        tail_ref[0:2, sl] = gate[tm - 2:tm, :]
        act_ref[:, sl] = (conv * _sigmoid(conv) * up).astype(BF16)
    o_ref[0] = h + _dot(act_ref[...], wd_ref[...])


def _conv_ffn(h, g_ffn, w_gate_up, conv_w, conv_b, w_down):
    b, l, d = h.shape
    assert conv_w.shape == (CONV_W, D_FF) and w_gate_up.shape == (d, 2 * D_FF)
    tm = 2 * TOK_TILE
    tok = pl.BlockSpec((1, tm, d), lambda i, j: (i, j, 0))
    full = lambda a: pl.BlockSpec(a.shape, lambda i, j: (0,) * a.ndim, pipeline_mode=pl.Buffered(1))
    half = lambda c: pl.BlockSpec((d, D_FF), lambda i, j: (0, c), pipeline_mode=pl.Buffered(1))
    return pl.pallas_call(
        _conv_ffn_kernel, grid=(b, l // tm),
        in_specs=[tok, full(g_ffn), half(0), half(1), full(conv_w), full(conv_b), full(w_down)],
        out_specs=tok,
        out_shape=jax.ShapeDtypeStruct((b, l, d), F32),
        scratch_shapes=[pltpu.VMEM((8, D_FF), F32), pltpu.VMEM((tm, D_FF), BF16)],
        compiler_params=_cparams(2), name="conv_ffn")(h, g_ffn, w_gate_up, w_gate_up, conv_w, conv_b, w_down)


def _gate_placement():
    place = np.zeros((LANES, FOX_WIDTH), np.float32)
    for h in range(N_FOX_HEADS):
        base = (h // 2) * LANES + (HEAD_DIM if h % 2 == 0 else 0)
        for term in range(3):
            place[term * N_FOX_HEADS + h, base + term] = 1.0
    return jnp.asarray(place, dtype=BF16)


def _layer(h, mem, p):
    b, l, d = h.shape
    nck = l // CHUNK
    assert nck == LANES and d == D_MODEL, "one sequence's chunks must fill exactly one lane tile"
    row = lambda a: a.reshape(1, -1)
    col = lambda a: a.reshape(-1, 1)
    w_in = p["w_in"]
    qk_end, v_end = 2 * FOX_WIDTH, 3 * FOX_WIDTH
    w_qk = w_in[:, :qk_end].astype(BF16)
    w_vt = w_in[:, qk_end:v_end].T.astype(BF16)
    w_uf = jnp.concatenate(
        [w_in[:, v_end + N_FOX_HEADS:],
         jnp.pad(w_in[:, v_end:v_end + N_FOX_HEADS], ((0, 0), (0, LANES - N_FOX_HEADS)))], axis=1).astype(BF16)
    f_bias = jnp.pad(p["fox_f_bias"], (0, LANES - N_FOX_HEADS)).reshape(1, LANES)

    tr_op, o_op, a16r, a16i = _s5_params(
        p["s5_log_dt"], p["s5_a_re"], p["s5_a_im"], p["s5_b_re"], p["s5_b_im"],
        p["s5_c_re"], p["s5_c_im"], p["s5_d"])

    q, k, vt, xt, cs = _in_proj(h, row(p["norm_mix"]), w_uf, w_qk, w_vt,
                                row(jnp.tile(p["fox_q_norm"], N_FOX_HEADS)),
                                row(jnp.tile(p["fox_k_norm"], N_FOX_HEADS)), f_bias, _gate_placement())
    fox, (w_ffn_up, w_ffn_down, w_out, w_xq, w_xo, w_xkv) = _fox_attn(
        q, k, cs, vt, [p["w_ffn_up"], p["w_ffn_down"], p["w_out"], p["w_xq"], p["w_xo"], p["w_xkv"]])

    yg = _s5_chunk(xt, tr_op, o_op, a16r, a16i, nck)
    lane_bcast = lambda a: jnp.broadcast_to(col(a), (a.shape[0], LANES))
    s5n = _s5_post(yg, p["s5_w_glu"].T.astype(BF16), lane_bcast(p["s5_b_glu"]), lane_bcast(p["out_norm_s5"]), b)
    s5n = s5n.reshape(b, l, S5_WIDTH)

    xk, xv = _mem_kv(mem, row(p["norm_mem"]), w_xkv, row(jnp.tile(p["xk_norm"], N_X_HEADS)))
    h2 = _mix_cross(h, fox, s5n, row(p["out_norm_fox"]), w_out, row(p["norm_cross"]), w_xq,
                    row(jnp.tile(p["xq_norm"], N_X_HEADS)), xk, xv, w_xo)
    return _conv_ffn(h2, row(p["norm_ffn"]), w_ffn_up, p["ffn_conv_w"], row(p["ffn_conv_b"]), w_ffn_down)


_PARAM_NAMES = (
    "norm_mix", "w_in", "fox_q_norm", "fox_k_norm", "fox_f_bias", "s5_a_re", "s5_a_im", "s5_log_dt",
    "s5_b_re", "s5_b_im", "s5_c_re", "s5_c_im", "s5_d", "s5_w_glu", "s5_b_glu", "out_norm_fox",
    "out_norm_s5", "w_out", "norm_cross", "norm_mem", "w_xq", "w_xkv", "xq_norm", "xk_norm", "w_xo",
    "norm_ffn", "w_ffn_up", "ffn_conv_w", "ffn_conv_b", "w_ffn_down")


def kernel(x, mem, norm_mix, w_in, fox_q_norm, fox_k_norm, fox_f_bias, s5_a_re, s5_a_im, s5_log_dt, s5_b_re, s5_b_im, s5_c_re, s5_c_im, s5_d, s5_w_glu, s5_b_glu, out_norm_fox, out_norm_s5, w_out, norm_cross, norm_mem, w_xq, w_xkv, xq_norm, xk_norm, w_xo, norm_ffn, w_ffn_up, ffn_conv_w, ffn_conv_b, w_ffn_down):
    stacked = (norm_mix, w_in, fox_q_norm, fox_k_norm, fox_f_bias, s5_a_re, s5_a_im, s5_log_dt,
               s5_b_re, s5_b_im, s5_c_re, s5_c_im, s5_d, s5_w_glu, s5_b_glu, out_norm_fox,
               out_norm_s5, w_out, norm_cross, norm_mem, w_xq, w_xkv, xq_norm, xk_norm, w_xo,
               norm_ffn, w_ffn_up, ffn_conv_w, ffn_conv_b, w_ffn_down)
    h = x
    for layer in range(norm_mix.shape[0]):
        h = _layer(h, mem, {n: a[layer] for n, a in zip(_PARAM_NAMES, stacked)})
    return h
```

```python
import functools
import math

import jax
import jax.numpy as jnp
import numpy as np
from jax import lax
from jax.experimental import pallas as pl
from jax.experimental.pallas import tpu as pltpu

F32 = jnp.float32
BF16 = jnp.bfloat16

D_MODEL = 1024
N_MEM = 256
FOX_WIDTH = 512
HEAD_DIM = 64
N_FOX_HEADS = 8
S5_WIDTH = 512
S5_GROUP_CH = 16
S5_GROUPS = 32
S5_STATE = 64
N_X_HEADS = 4
X_HEAD_DIM = 256
D_FF = 2816
CONV_W = 3
EPS = 1e-6

LANES = 128
CHUNK = 16
CHUNK_W = CHUNK * S5_GROUP_CH
TOK_TILE = 512
ATT_BLOCK = 256
FF_CHUNK = 256
NEG_BIG = -1e30
LOG2E = math.log2(math.e)
HI = lax.Precision.HIGHEST
VMEM_LIMIT = 56 * 1024 * 1024
IN_PROJ_VMEM_LIMIT = 60 * 1024 * 1024

NT_DIMS = (((1,), (1,)), ((), ()))


def _cparams(n_axes):
    return pltpu.CompilerParams(dimension_semantics=("arbitrary",) * n_axes,
                                vmem_limit_bytes=VMEM_LIMIT)


def _rms(xf, g):
    return xf * lax.rsqrt(jnp.mean(xf * xf, axis=-1, keepdims=True) + EPS) * g


def _dot(a, b):
    return jnp.dot(a, b, preferred_element_type=F32)


def _dot_nt(a, b):
    return lax.dot_general(a, b, NT_DIMS, preferred_element_type=F32)


def _head64_norm(x):
    lo = lax.broadcasted_iota(jnp.int32, (x.shape[0], LANES), 1) < HEAD_DIM
    outs = []
    for j in range(x.shape[1] // LANES):
        blk = x[:, j * LANES:(j + 1) * LANES]
        sq = blk * blk
        s_lo = jnp.sum(jnp.where(lo, sq, 0.0), axis=-1, keepdims=True)
        s_hi = jnp.sum(jnp.where(lo, 0.0, sq), axis=-1, keepdims=True)
        inv = jnp.where(lo, lax.rsqrt(s_lo * (1.0 / HEAD_DIM) + EPS),
                        lax.rsqrt(s_hi * (1.0 / HEAD_DIM) + EPS))
        outs.append(blk * inv)
    return jnp.concatenate(outs, axis=-1)


def _head256_norm(x):
    outs = []
    for h in range(x.shape[1] // X_HEAD_DIM):
        blk = x[:, h * X_HEAD_DIM:(h + 1) * X_HEAD_DIM]
        outs.append(blk * lax.rsqrt(jnp.mean(blk * blk, axis=-1, keepdims=True) + EPS))
    return jnp.concatenate(outs, axis=-1)


def _row_halves(rows):
    return [slice(0, rows // 2), slice(rows // 2, rows)]


def _sigmoid(z):
    return 0.5 * jnp.tanh(0.5 * z) + 0.5


def _log_sigmoid(z):
    return jnp.minimum(z, 0.0) - jnp.log(1.0 + jnp.exp(-jnp.abs(z)))


def _gelu_tanh(x):
    k = math.sqrt(2.0 / math.pi)
    half = 0.5 * x
    return half + half * jnp.tanh(x * (k + (k * 0.044715) * (x * x)))


def _s5_params_kernel(*refs):
    for gi in range(refs[0].shape[0]):
        _s5_group_operators(gi, *refs)


def _s5_group_operators(gi, ldt_ref, arc_ref, aic_ref, arr_ref, air_ref, bre_ref, bim_ref, cre2_ref, cim2_ref,
                        d_ref, tr_ref, o_ref, a16r_ref, a16i_ref):
    dt = jnp.exp(ldt_ref[gi])
    ar, ai = arc_ref[gi], aic_ref[gi]
    mag = jnp.exp(ar * dt)
    lr, li = mag * jnp.cos(ai * dt), mag * jnp.sin(ai * dt)
    den = ar * ar + ai * ai
    nr = lr - 1.0
    coef_r = (nr * ar + li * ai) / den
    coef_i = (li * ar - nr * ai) / den
    bb_r = coef_r * bre_ref[gi] - coef_i * bim_ref[gi]
    bb_i = coef_r * bim_ref[gi] + coef_i * bre_ref[gi]
    arr, air = arr_ref[gi], air_ref[gi]
    magr = jnp.exp(arr * dt)
    lrr, lir = magr * jnp.cos(air * dt), magr * jnp.sin(air * dt)

    lane_t = lax.broadcasted_iota(jnp.int32, (S5_STATE, CHUNK_W), 1) // S5_GROUP_CH
    lane16 = lax.broadcasted_iota(jnp.int32, (S5_GROUP_CH, CHUNK_W), 1)
    row16 = lax.broadcasted_iota(jnp.int32, (S5_GROUP_CH, CHUNK_W), 0)
    lo128 = lax.broadcasted_iota(jnp.int32, (S5_GROUP_CH, 2 * S5_STATE), 1) < S5_STATE
    c_cat = jnp.where(lo128, cre2_ref[gi], -cim2_ref[gi])
    d_col = d_ref[gi]

    expo = (CHUNK - 1) - lane_t
    pw_r = jnp.ones((S5_STATE, CHUNK_W), F32)
    pw_i = jnp.zeros((S5_STATE, CHUNK_W), F32)
    sq_r, sq_i = lr, li
    for bit in range(CHUNK.bit_length() - 1):
        take = ((expo >> bit) & 1) == 1
        pw_r, pw_i = (jnp.where(take, pw_r * sq_r - pw_i * sq_i, pw_r),
                      jnp.where(take, pw_r * sq_i + pw_i * sq_r, pw_i))
        sq_r, sq_i = sq_r * sq_r - sq_i * sq_i, 2.0 * sq_r * sq_i
    a16r_ref[gi] = jnp.broadcast_to(sq_r, (S5_STATE, LANES))
    a16i_ref[gi] = jnp.broadcast_to(sq_i, (S5_STATE, LANES))
    m_r = pw_r * bb_r - pw_i * bb_i
    m_i = pw_r * bb_i + pw_i * bb_r
    tr_ref[gi, CHUNK_W:CHUNK_W + S5_STATE, :] = m_r.astype(BF16)
    tr_ref[gi, CHUNK_W + S5_STATE:CHUNK_W + 2 * S5_STATE, :] = m_i.astype(BF16)
    k_all = jnp.dot(c_cat, jnp.concatenate([m_r, m_i], axis=0), preferred_element_type=F32, precision=HI)
    w_r, w_i = cre2_ref[gi], cim2_ref[gi]
    for tq in range(CHUNK):
        shift = (CHUNK - 1 - tq) * S5_GROUP_CH
        t_rows = k_all if shift == 0 else pltpu.roll(k_all, CHUNK_W - shift, axis=1)
        t_rows = (jnp.where(lane16 < (tq + 1) * S5_GROUP_CH, t_rows, 0.0)
                  + jnp.where(lane16 == tq * S5_GROUP_CH + row16, d_col, 0.0))
        tr_ref[gi, tq * S5_GROUP_CH:(tq + 1) * S5_GROUP_CH, :] = t_rows.astype(BF16)
        w_r, w_i = w_r * lrr - w_i * lir, w_r * lir + w_i * lrr
        o_ref[gi, tq * S5_GROUP_CH:(tq + 1) * S5_GROUP_CH, :] = jnp.where(lo128, w_r, -w_i).astype(BF16)


def _s5_params(s5_log_dt, s5_a_re, s5_a_im, s5_b_re, s5_b_im, s5_c_re, s5_c_im, s5_d):
    g, p, c = S5_GROUPS, S5_STATE, S5_GROUP_CH
    args = (
        s5_log_dt.reshape(g, 1, 1),
        s5_a_re.reshape(g, p, 1), s5_a_im.reshape(g, p, 1),
        jnp.tile(s5_a_re.reshape(g, 1, p), (1, 1, 2)), jnp.tile(s5_a_im.reshape(g, 1, p), (1, 1, 2)),
        jnp.tile(s5_b_re, (1, 1, CHUNK)), jnp.tile(s5_b_im, (1, 1, CHUNK)),
        jnp.tile(s5_c_re, (1, 1, 2)), jnp.tile(s5_c_im, (1, 1, 2)),
        s5_d.reshape(g, c, 1),
    )
    per_step = 4
    in_specs = [pl.BlockSpec((per_step,) + a.shape[1:], lambda i: (i, 0, 0)) for a in args]
    out_shape = (
        jax.ShapeDtypeStruct((g, CHUNK_W + 2 * p, CHUNK_W), BF16),
        jax.ShapeDtypeStruct((g, CHUNK_W, 2 * p), BF16),
        jax.ShapeDtypeStruct((g, p, LANES), F32),
        jax.ShapeDtypeStruct((g, p, LANES), F32),
    )
    out_specs = tuple(pl.BlockSpec((per_step,) + s.shape[1:], lambda i: (i, 0, 0)) for s in out_shape)
    return pl.pallas_call(
        _s5_params_kernel, grid=(g // per_step,), in_specs=in_specs, out_specs=out_specs, out_shape=out_shape,
        compiler_params=_cparams(1), name="s5_params")(*args)


def _pack3(v):
    hi = v.astype(BF16).astype(F32)
    rest = v - hi
    mid = rest.astype(BF16).astype(F32)
    lo = rest - mid
    return (hi + pltpu.roll(mid, N_FOX_HEADS, axis=1) + pltpu.roll(lo, 2 * N_FOX_HEADS, axis=1)).astype(BF16)


def _in_proj_kernel(x_ref, g_ref, wuf_ref, wqk_ref, wvt_ref, qg_ref, kg_ref, bias_ref, place_ref,
                    q_ref, k_ref, vt_ref, xt_ref, cs_ref, z_ref, f_ref):
    seq = x_ref.shape[1]
    nck = seq // CHUNK
    n_slab = S5_WIDTH // LANES
    grp_per_slab = LANES // S5_GROUP_CH
    tiles = [slice(r * TOK_TILE, (r + 1) * TOK_TILE) for r in range(seq // TOK_TILE)]
    gate_blocks = seq // LANES
    head_lane = lax.broadcasted_iota(jnp.int32, (LANES, LANES), 1) < N_FOX_HEADS
    lower = (lax.broadcasted_iota(jnp.int32, (LANES, LANES), 0)
             >= lax.broadcasted_iota(jnp.int32, (LANES, LANES), 1)).astype(BF16)

    def normed(rows):
        return _rms(x_ref[0, rows, :], g_ref[...]).astype(BF16)

    for rows in tiles:
        z = _dot(normed(rows), wuf_ref[...])
        for j in range(n_slab):
            z_ref[j, rows, :] = z[:, j * LANES:(j + 1) * LANES]
        f_ref[rows, :] = z[:, S5_WIDTH:]

    local = []
    for n, rows in enumerate(tiles):
        hn = normed(rows)
        qk = _dot(hn, wqk_ref[...])
        vt = _dot_nt(wvt_ref[...], hn)
        for t in range(n * CHUNK // len(tiles), (n + 1) * CHUNK // len(tiles)):
            for j in range(n_slab):
                blk = z_ref[j, pl.ds(t, nck, stride=CHUNK), :]
                xt_ref[0, j * grp_per_slab:(j + 1) * grp_per_slab, t * S5_GROUP_CH:(t + 1) * S5_GROUP_CH, :] = (
                    blk.T.reshape(grp_per_slab, S5_GROUP_CH, nck).astype(BF16))
        if n == 0:
            for blk in range(gate_blocks):
                log2f = jnp.where(head_lane, _log_sigmoid(f_ref[blk * LANES:(blk + 1) * LANES, :] + bias_ref[...])
                                  * LOG2E, 0.0)
                part = _dot(lower, _pack3(log2f))
                c = (part + pltpu.roll(part, LANES - N_FOX_HEADS, axis=1)
                     + pltpu.roll(part, LANES - 2 * N_FOX_HEADS, axis=1))
                local.append(jnp.where(head_lane, c, 0.0))
        if n == min(1, len(tiles) - 1):
            carry = jnp.zeros((1, LANES), F32)
            for blk in range(gate_blocks):
                c = local[blk] + carry
                carry = c[LANES - 1:LANES, :]
                cs_ref[0, blk * LANES:(blk + 1) * LANES, :] = _dot(_pack3(c), place_ref[...]).astype(BF16)
        q = _head64_norm(qk[:, :FOX_WIDTH]) * (qg_ref[...] * (HEAD_DIM ** -0.5 * LOG2E))
        k = _head64_norm(qk[:, FOX_WIDTH:]) * kg_ref[...]
        q_ref[0, rows, :] = q.astype(BF16)
        k_ref[0, rows, :] = k.astype(BF16)
        vt_ref[0, :, rows] = vt.astype(BF16)


def _in_proj(x, norm_mix, w_uf, w_qk, w_vt, q_gain, k_gain, bias_row, place):
    b, l, d = x.shape
    nck = l // CHUNK
    full = lambda a: pl.BlockSpec(a.shape, lambda i: (0,) * a.ndim, pipeline_mode=pl.Buffered(1))
    tok = pl.BlockSpec((1, l, FOX_WIDTH), lambda i: (i, 0, 0))
    return pl.pallas_call(
        _in_proj_kernel, grid=(b,),
        in_specs=[pl.BlockSpec((1, l, d), lambda i: (i, 0, 0)), full(norm_mix), full(w_uf), full(w_qk),
                  full(w_vt), full(q_gain), full(k_gain), full(bias_row), full(place)],
        out_specs=(tok, tok, pl.BlockSpec((1, FOX_WIDTH, l), lambda i: (i, 0, 0)),
                   pl.BlockSpec((1, S5_GROUPS, CHUNK_W, nck), lambda i: (i, 0, 0, 0)), tok),
        out_shape=(jax.ShapeDtypeStruct((b, l, FOX_WIDTH), BF16),) * 2
        + (jax.ShapeDtypeStruct((b, FOX_WIDTH, l), BF16),
           jax.ShapeDtypeStruct((b, S5_GROUPS, CHUNK_W, nck), BF16),
           jax.ShapeDtypeStruct((b, l, FOX_WIDTH), BF16)),
        scratch_shapes=[pltpu.VMEM((S5_WIDTH // LANES, l, LANES), F32), pltpu.VMEM((l, LANES), F32)],
        compiler_params=pltpu.CompilerParams(dimension_semantics=("arbitrary",),
                                             vmem_limit_bytes=IN_PROJ_VMEM_LIMIT),
        name="in_proj")(x, norm_mix, w_uf, w_qk, w_vt, q_gain, k_gain, bias_row, place)


def _fox_kernel(q_ref, k_ref, cs_ref, vt_ref, *rest, n_side):
    side_in, o_ref, side_out = rest[:n_side], rest[n_side], rest[n_side + 1:2 * n_side + 1]
    kaug_ref, s_ref = rest[2 * n_side + 1:]
    for w_ref, wb_ref in zip(side_in, side_out):
        wb_ref[...] = w_ref[...].astype(BF16)

    blk = ATT_BLOCK
    seq = q_ref.shape[1]
    lane_k = lax.broadcasted_iota(jnp.int32, (seq, LANES), 1)
    k2 = k_ref[0]
    cs = cs_ref[0]
    kaug_ref[0] = jnp.where(lane_k < HEAD_DIM, k2, cs)
    kaug_ref[1] = jnp.where(lane_k >= HEAD_DIM, k2, cs)

    lane_q = lax.broadcasted_iota(jnp.int32, (blk, LANES), 1)
    minus1 = [jnp.where((lane_q >= HEAD_DIM) & (lane_q < HEAD_DIM + 3), -1.0, 0.0).astype(BF16),
              jnp.where(lane_q < 3, -1.0, 0.0).astype(BF16)]
    own = [lane_q < HEAD_DIM, lane_q >= HEAD_DIM]
    key_pos = lax.broadcasted_iota(jnp.int32, (blk, blk), 0)
    qry_pos = lax.broadcasted_iota(jnp.int32, (blk, blk), 1)

    def scores(i):
        q2 = q_ref[0, i * blk:(i + 1) * blk, :]
        for hh in range(2):
            qa = jnp.where(own[hh], q2, minus1[hh])
            st = _dot_nt(kaug_ref[hh, 0:(i + 1) * blk, :], qa)
            if i > 0:
                s_ref[i % 2, hh, 0:i * blk, :] = st[:i * blk]
            s_ref[i % 2, hh, i * blk:(i + 1) * blk, :] = jnp.where(key_pos > qry_pos, NEG_BIG, st[i * blk:])

    def attend(i):
        outs = []
        for hh in range(2):
            tile = lambda j: s_ref[i % 2, hh, j * blk:(j + 1) * blk, :]
            m = jnp.max(tile(0), axis=0, keepdims=True)
            for j in range(1, i + 1):
                m = jnp.maximum(m, jnp.max(tile(j), axis=0, keepdims=True))
            l = jnp.zeros((1, blk), F32)
            acc = jnp.zeros((HEAD_DIM, blk), F32)
            for j in range(i + 1):
                p = jnp.exp2(tile(j) - m)
                l = l + jnp.sum(p, axis=0, keepdims=True)
                acc = acc + _dot(vt_ref[0, hh * HEAD_DIM:(hh + 1) * HEAD_DIM, j * blk:(j + 1) * blk],
                                 p.astype(BF16))
            outs.append(acc / l)
        o_ref[0, i * blk:(i + 1) * blk, :] = jnp.concatenate(outs, axis=0).T.astype(BF16)

    n_blk = seq // blk
    scores(0)
    for i in range(n_blk):
        if i + 1 < n_blk:
            scores(i + 1)
        attend(i)


def _cast_row_block(rows, steps):
    block = 16 * pl.cdiv(pl.cdiv(rows, steps), 16)
    while rows % block:
        block += 16
    return block


def _fox_attn(q, k, cs, vt, side_weights):
    b, l, _ = q.shape
    n_pairs = FOX_WIDTH // LANES
    tok = pl.BlockSpec((1, l, LANES), lambda i, j: (i, 0, j))

    def side_spec(w):
        block = _cast_row_block(w.shape[0], b * n_pairs)
        last = w.shape[0] // block - 1
        return pl.BlockSpec((block, w.shape[1]), lambda i, j: (jnp.minimum(i * n_pairs + j, last), 0))

    side_specs = [side_spec(w) for w in side_weights]
    outs = pl.pallas_call(
        functools.partial(_fox_kernel, n_side=len(side_weights)), grid=(b, n_pairs),
        in_specs=[tok, tok, tok, pl.BlockSpec((1, LANES, l), lambda i, j: (i, j, 0))] + side_specs,
        out_specs=[tok] + side_specs,
        out_shape=[jax.ShapeDtypeStruct((b, l, FOX_WIDTH), BF16)]
        + [jax.ShapeDtypeStruct(w.shape, BF16) for w in side_weights],
        scratch_shapes=[pltpu.VMEM((2, l, LANES), BF16), pltpu.VMEM((2, 2, l, ATT_BLOCK), F32)],
        compiler_params=_cparams(2), name="fox_attn")(q, k, cs, vt, *side_weights)
    return outs[0], outs[1:]


def _s5_chunk_kernel(xt_ref, tr_ref, o_ref, a16r_ref, a16i_ref, yg_ref):
    n_seq, _, _, nck = xt_ref.shape
    n_levels = nck.bit_length() - 1
    kpos = lax.broadcasted_iota(jnp.int32, (S5_STATE, nck), 1)
    a_r, a_i = a16r_ref[0], a16i_ref[0]
    coef = []
    for lvl in range(n_levels):
        keep = kpos >= (1 << lvl)
        coef.append((jnp.where(keep, a_r, 0.0), jnp.where(keep, a_i, 0.0)))
        a_r, a_i = a_r * a_r - a_i * a_i, 2.0 * a_r * a_i
    first = kpos == 0
    stage1 = [_dot(tr_ref[0], xt_ref[n, 0]) for n in range(n_seq)]
    states = [(yr[CHUNK_W:CHUNK_W + S5_STATE], yr[CHUNK_W + S5_STATE:]) for yr in stage1]
    for lvl in range(n_levels):
        c_r, c_i = coef[lvl]
        rolled = [(pltpu.roll(s_r, 1 << lvl, axis=1), pltpu.roll(s_i, 1 << lvl, axis=1)) for s_r, s_i in states]
        states = [(s_r + (c_r * p_r - c_i * p_i), s_i + (c_r * p_i + c_i * p_r))
                  for (s_r, s_i), (p_r, p_i) in zip(states, rolled)]
    entering = [jnp.concatenate([jnp.where(first, 0.0, pltpu.roll(s_r, 1, axis=1)),
                                 jnp.where(first, 0.0, pltpu.roll(s_i, 1, axis=1))], axis=0).astype(BF16)
                for s_r, s_i in states]
    carried = [_dot(o_ref[0], s_in) for s_in in entering]
    for n in range(n_seq):
        yg = _gelu_tanh(stage1[n][:CHUNK_W] + carried[n]).astype(BF16)
        yg_ref[n] = yg.reshape(CHUNK, S5_GROUP_CH, nck)


def _s5_chunk(xt, tr_op, o_op, a16r, a16i):
    b, g, _, nck = xt.shape
    per_g = lambda a: pl.BlockSpec((1,) + a.shape[1:], lambda i: (i, 0, 0))
    return pl.pallas_call(
        _s5_chunk_kernel, grid=(g,),
        in_specs=[pl.BlockSpec((b, 1, CHUNK_W, nck), lambda i: (0, i, 0, 0)),
                  per_g(tr_op), per_g(o_op), per_g(a16r), per_g(a16i)],
        out_specs=pl.BlockSpec((b, CHUNK, S5_GROUP_CH, nck), lambda i: (0, 0, i, 0)),
        out_shape=jax.ShapeDtypeStruct((b, CHUNK, S5_WIDTH, nck), BF16),
        compiler_params=_cparams(1), name="s5_chunk")(xt, tr_op, o_op, a16r, a16i)


def _s5_post_kernel(yg_ref, wgt_ref, bg_ref, gn_ref, o_ref):
    nck = yg_ref.shape[3]
    bias = jnp.concatenate([bg_ref[...]] * 2, axis=1)
    gain = jnp.concatenate([gn_ref[...]] * 2, axis=1)
    for t in range(0, CHUNK, 2):
        y = jnp.concatenate([yg_ref[0, t], yg_ref[0, t + 1]], axis=1)
        z = y.astype(F32) * _sigmoid(_dot(wgt_ref[...], y) + bias)
        zn = z * lax.rsqrt(jnp.mean(z * z, axis=0, keepdims=True) + EPS) * gain
        o_ref[0, :, t, :] = zn[:, :nck].T
        o_ref[0, :, t + 1, :] = zn[:, nck:].T


def _s5_post(yg, w_glu_t, b_glu_col, gain_col):
    batch, _, _, nck = yg.shape
    full = lambda a: pl.BlockSpec(a.shape, lambda i: (0,) * a.ndim)
    return pl.pallas_call(
        _s5_post_kernel, grid=(batch,),
        in_specs=[pl.BlockSpec((1, CHUNK, S5_WIDTH, nck), lambda i: (i, 0, 0, 0)),
                  full(w_glu_t), full(b_glu_col), full(gain_col)],
        out_specs=pl.BlockSpec((1, nck, CHUNK, S5_WIDTH), lambda i: (i, 0, 0, 0)),
        out_shape=jax.ShapeDtypeStruct((batch, nck, CHUNK, S5_WIDTH), F32),
        compiler_params=_cparams(1), name="s5_post")(yg, w_glu_t, b_glu_col, gain_col)


def _mix_cross_kernel(x_ref, fox_ref, s5_ref, gf_ref, wo_ref, gc_ref, wxq_ref, gq_ref, xk_ref, xv_ref, wxo_ref,
                      o_ref):
    halves = _row_halves(x_ref.shape[1])
    heads = [slice(hd * X_HEAD_DIM, (hd + 1) * X_HEAD_DIM) for hd in range(N_X_HEADS)]
    mixed = [jnp.concatenate([_rms(fox_ref[0, r, :].astype(F32), gf_ref[...]).astype(BF16),
                              s5_ref[0, r, :].astype(BF16)], axis=-1) for r in halves]
    proj = [_dot(m, wo_ref[...]) for m in mixed]
    h1 = [x_ref[0, r, :] + pr for r, pr in zip(halves, proj)]
    xq_raw = [_dot(_rms(h, gc_ref[...]).astype(BF16), wxq_ref[...]) for h in h1]
    xq = [(_head256_norm(v) * (gq_ref[...] * X_HEAD_DIM ** -0.5)).astype(BF16) for v in xq_raw]
    scores = [[_dot_nt(q[:, sl], xk_ref[0, :, sl]) for sl in heads] for q in xq]
    xo = []
    for sc in scores:
        outs = []
        for s, sl in zip(sc, heads):
            p = jnp.exp(s - jnp.max(s, axis=-1, keepdims=True))
            p = p / jnp.sum(p, axis=-1, keepdims=True)
            outs.append(_dot(p.astype(BF16), xv_ref[0, :, sl]).astype(BF16))
        xo.append(jnp.concatenate(outs, axis=-1))
    for r, h, v in zip(halves, h1, xo):
        o_ref[0, r, :] = h + _dot(v, wxo_ref[...])


def _mix_cross(x, fox, s5n, g_fox, w_out, g_cross, w_xq, g_xq, xk, xv, w_xo):
    b, l, d = x.shape
    tm = 2 * TOK_TILE
    tok = lambda w: pl.BlockSpec((1, tm, w), lambda i, j: (i, j, 0))
    full = lambda a: pl.BlockSpec(a.shape, lambda i, j: (0,) * a.ndim)
    mem = pl.BlockSpec((1, N_MEM, d), lambda i, j: (i, 0, 0))
    return pl.pallas_call(
        _mix_cross_kernel, grid=(b, l // tm),
        in_specs=[tok(d), tok(FOX_WIDTH), tok(S5_WIDTH), full(g_fox), full(w_out), full(g_cross),
                  full(w_xq), full(g_xq), mem, mem, full(w_xo)],
        out_specs=tok(d),
        out_shape=jax.ShapeDtypeStruct((b, l, d), F32),
        compiler_params=_cparams(2), name="mix_cross")(x, fox, s5n, g_fox, w_out, g_cross, w_xq, g_xq, xk, xv,
                                                        w_xo)


def _mem_kv_kernel(mem_ref, g_ref, wkv_ref, gk_ref, xk_ref, xv_ref):
    mn = _rms(mem_ref[0], g_ref[...]).astype(BF16)
    kv = _dot(mn, wkv_ref[...])
    xk_ref[0] = (_head256_norm(kv[:, :D_MODEL]) * gk_ref[...]).astype(BF16)
    xv_ref[0] = kv[:, D_MODEL:].astype(BF16)


def _mem_kv(mem, g_mem, w_xkv, g_xk):
    b, n, d = mem.shape
    full = lambda a: pl.BlockSpec(a.shape, lambda i: (0,) * a.ndim)
    blk = pl.BlockSpec((1, n, d), lambda i: (i, 0, 0))
    return pl.pallas_call(
        _mem_kv_kernel, grid=(b,),
        in_specs=[blk, full(g_mem), full(w_xkv), full(g_xk)],
        out_specs=(blk, blk),
        out_shape=(jax.ShapeDtypeStruct((b, n, d), BF16),) * 2,
        compiler_params=_cparams(1), name="mem_kv")(mem, g_mem, w_xkv, g_xk)


def _conv_ffn_kernel(h_ref, g_ref, wg_ref, wu_ref, cw_ref, cb_ref, wd_ref, o_ref, tail_ref, act_ref):
    @pl.when(pl.program_id(1) == 0)
    def _():
        tail_ref[...] = jnp.zeros_like(tail_ref)

    h = h_ref[0]
    tm = h.shape[0]
    hn = _rms(h, g_ref[...]).astype(BF16)
    row = lax.broadcasted_iota(jnp.int32, (tm, FF_CHUNK), 0)
    for c in range(D_FF // FF_CHUNK):
        sl = slice(c * FF_CHUNK, (c + 1) * FF_CHUNK)
        gate = _dot(hn, wg_ref[:, sl])
        up = _dot(hn, wu_ref[:, sl])
        prev1 = tail_ref[1:2, sl]
        prev2 = tail_ref[0:1, sl]
        g1 = jnp.where(row == 0, prev1, pltpu.roll(gate, 1, axis=0))
        g2 = jnp.where(row == 0, prev2, jnp.where(row == 1, prev1, pltpu.roll(gate, 2, axis=0)))
        conv = cb_ref[:, sl] + cw_ref[0:1, sl] * g2 + cw_ref[1:2, sl] * g1 + cw_ref[2:3, sl] * gate
        tail_ref[0:2, sl] = gate[tm - 2:tm, :]
        act_ref[:, sl] = (conv * _sigmoid(conv) * up).astype(BF16)
    o_ref[0] = h + _dot(act_ref[...], wd_ref[...])


def _conv_ffn(h, g_ffn, w_gate_up, conv_w, conv_b, w_down):
    b, l, d = h.shape
    assert conv_w.shape == (CONV_W, D_FF) and w_gate_up.shape == (d, 2 * D_FF)
    tm = 2 * TOK_TILE
    tok = pl.BlockSpec((1, tm, d), lambda i, j: (i, j, 0))
    full = lambda a: pl.BlockSpec(a.shape, lambda i, j: (0,) * a.ndim, pipeline_mode=pl.Buffered(1))
    half = lambda c: pl.BlockSpec((d, D_FF), lambda i, j: (0, c), pipeline_mode=pl.Buffered(1))
    return pl.pallas_call(
        _conv_ffn_kernel, grid=(b, l // tm),
        in_specs=[tok, full(g_ffn), half(0), half(1), full(conv_w), full(conv_b), full(w_down)],
        out_specs=tok,
        out_shape=jax.ShapeDtypeStruct((b, l, d), F32),
        scratch_shapes=[pltpu.VMEM((8, D_FF), F32), pltpu.VMEM((tm, D_FF), BF16)],
        compiler_params=_cparams(2), name="conv_ffn")(h, g_ffn, w_gate_up, w_gate_up, conv_w, conv_b, w_down)


def _gate_placement():
    place = np.zeros((LANES, FOX_WIDTH), np.float32)
    for h in range(N_FOX_HEADS):
        base = (h // 2) * LANES + (HEAD_DIM if h % 2 == 0 else 0)
        for term in range(3):
            place[term * N_FOX_HEADS + h, base + term] = 1.0
    return jnp.asarray(place, dtype=BF16)


def _layer(h, mem, p):
    b, l, d = h.shape
    nck = l // CHUNK
    assert nck == LANES and d == D_MODEL, "one sequence's chunks must fill exactly one lane tile"
    row = lambda a: a.reshape(1, -1)
    col = lambda a: a.reshape(-1, 1)
    w_in = p["w_in"]
    qk_end, v_end = 2 * FOX_WIDTH, 3 * FOX_WIDTH
    w_qk = w_in[:, :qk_end].astype(BF16)
    w_vt = w_in[:, qk_end:v_end].T.astype(BF16)
    w_uf = jnp.concatenate(
        [w_in[:, v_end + N_FOX_HEADS:],
         jnp.pad(w_in[:, v_end:v_end + N_FOX_HEADS], ((0, 0), (0, LANES - N_FOX_HEADS)))], axis=1).astype(BF16)
    f_bias = jnp.pad(p["fox_f_bias"], (0, LANES - N_FOX_HEADS)).reshape(1, LANES)

    tr_op, o_op, a16r, a16i = _s5_params(
        p["s5_log_dt"], p["s5_a_re"], p["s5_a_im"], p["s5_b_re"], p["s5_b_im"],
        p["s5_c_re"], p["s5_c_im"], p["s5_d"])

    q, k, vt, xt, cs = _in_proj(h, row(p["norm_mix"]), w_uf, w_qk, w_vt,
                                row(jnp.tile(p["fox_q_norm"], N_FOX_HEADS)),
                                row(jnp.tile(p["fox_k_norm"], N_FOX_HEADS)), f_bias, _gate_placement())
    fox, (w_ffn_up, w_ffn_down, w_out, w_xq, w_xo, w_xkv) = _fox_attn(
        q, k, cs, vt, [p["w_ffn_up"], p["w_ffn_down"], p["w_out"], p["w_xq"], p["w_xo"], p["w_xkv"]])

    yg = _s5_chunk(xt, tr_op, o_op, a16r, a16i)
    lane_bcast = lambda a: jnp.broadcast_to(col(a), (a.shape[0], LANES))
    s5n = _s5_post(yg, p["s5_w_glu"].T.astype(BF16), lane_bcast(p["s5_b_glu"]), lane_bcast(p["out_norm_s5"]))
    s5n = s5n.reshape(b, l, S5_WIDTH)

    xk, xv = _mem_kv(mem, row(p["norm_mem"]), w_xkv, row(jnp.tile(p["xk_norm"], N_X_HEADS)))
    h2 = _mix_cross(h, fox, s5n, row(p["out_norm_fox"]), w_out, row(p["norm_cross"]), w_xq,
                    row(jnp.tile(p["xq_norm"], N_X_HEADS)), xk, xv, w_xo)
    return _conv_ffn(h2, row(p["norm_ffn"]), w_ffn_up, p["ffn_conv_w"], row(p["ffn_conv_b"]), w_ffn_down)


_PARAM_NAMES = (
    "norm_mix", "w_in", "fox_q_norm", "fox_k_norm", "fox_f_bias", "s5_a_re", "s5_a_im", "s5_log_dt",
    "s5_b_re", "s5_b_im", "s5_c_re", "s5_c_im", "s5_d", "s5_w_glu", "s5_b_glu", "out_norm_fox",
    "out_norm_s5", "w_out", "norm_cross", "norm_mem", "w_xq", "w_xkv", "xq_norm", "xk_norm", "w_xo",
    "norm_ffn", "w_ffn_up", "ffn_conv_w", "ffn_conv_b", "w_ffn_down")


def kernel(x, mem, norm_mix, w_in, fox_q_norm, fox_k_norm, fox_f_bias, s5_a_re, s5_a_im, s5_log_dt, s5_b_re, s5_b_im, s5_c_re, s5_c_im, s5_d, s5_w_glu, s5_b_glu, out_norm_fox, out_norm_s5, w_out, norm_cross, norm_mem, w_xq, w_xkv, xq_norm, xk_norm, w_xo, norm_ffn, w_ffn_up, ffn_conv_w, ffn_conv_b, w_ffn_down):
    stacked = (norm_mix, w_in, fox_q_norm, fox_k_norm, fox_f_bias, s5_a_re, s5_a_im, s5_log_dt,
               s5_b_re, s5_b_im, s5_c_re, s5_c_im, s5_d, s5_w_glu, s5_b_glu, out_norm_fox,
               out_norm_s5, w_out, norm_cross, norm_mem, w_xq, w_xkv, xq_norm, xk_norm, w_xo,
               norm_ffn, w_ffn_up, ffn_conv_w, ffn_conv_b, w_ffn_down)
    h = x
    for layer in range(norm_mix.shape[0]):
        h = _layer(h, mem, {n: a[layer] for n, a in zip(_PARAM_NAMES, stacked)})
    return h
```

```python
import functools
import math

import jax
import jax.numpy as jnp
import numpy as np
from jax import lax
from jax.experimental import pallas as pl
from jax.experimental.pallas import tpu as pltpu

F32 = jnp.float32
BF16 = jnp.bfloat16

D_MODEL = 1024
N_MEM = 256
FOX_WIDTH = 512
HEAD_DIM = 64
N_FOX_HEADS = 8
S5_WIDTH = 512
S5_GROUP_CH = 16
S5_GROUPS = 32
S5_STATE = 64
N_X_HEADS = 4
X_HEAD_DIM = 256
D_FF = 2816
CONV_W = 3
EPS = 1e-6

LANES = 128
CHUNK = 16
CHUNK_W = CHUNK * S5_GROUP_CH
TOK_TILE = 512
ATT_BLOCK = 256
FOX_PAIRS_PER_STEP = 2
FF_CHUNK = 256
NEG_BIG = -1e30
LOG2E = math.log2(math.e)
HI = lax.Precision.HIGHEST
V7X_VMEM_BYTES = 64 * 1024 * 1024
VMEM_LIMIT = V7X_VMEM_BYTES * 7 // 8
IN_PROJ_VMEM_LIMIT = V7X_VMEM_BYTES * 15 // 16

NT_DIMS = (((1,), (1,)), ((), ()))


def _cparams(n_axes):
    return pltpu.CompilerParams(dimension_semantics=("arbitrary",) * n_axes,
                                vmem_limit_bytes=VMEM_LIMIT)


def _rms(xf, g):
    return xf * lax.rsqrt(jnp.mean(xf * xf, axis=-1, keepdims=True) + EPS) * g


def _dot(a, b):
    return jnp.dot(a, b, preferred_element_type=F32)


def _dot_nt(a, b):
    return lax.dot_general(a, b, NT_DIMS, preferred_element_type=F32)


def _head64_norm(x):
    lo = lax.broadcasted_iota(jnp.int32, (x.shape[0], LANES), 1) < HEAD_DIM
    outs = []
    for j in range(x.shape[1] // LANES):
        blk = x[:, j * LANES:(j + 1) * LANES]
        sq = blk * blk
        s_lo = jnp.sum(jnp.where(lo, sq, 0.0), axis=-1, keepdims=True)
        s_hi = jnp.sum(jnp.where(lo, 0.0, sq), axis=-1, keepdims=True)
        inv = jnp.where(lo, lax.rsqrt(s_lo * (1.0 / HEAD_DIM) + EPS),
                        lax.rsqrt(s_hi * (1.0 / HEAD_DIM) + EPS))
        outs.append(blk * inv)
    return jnp.concatenate(outs, axis=-1)


def _head256_norm(x):
    outs = []
    for h in range(x.shape[1] // X_HEAD_DIM):
        blk = x[:, h * X_HEAD_DIM:(h + 1) * X_HEAD_DIM]
        outs.append(blk * lax.rsqrt(jnp.mean(blk * blk, axis=-1, keepdims=True) + EPS))
    return jnp.concatenate(outs, axis=-1)


def _row_halves(rows):
    return [slice(0, rows // 2), slice(rows // 2, rows)]


def _sigmoid(z):
    return 0.5 * jnp.tanh(0.5 * z) + 0.5


def _log_sigmoid(z):
    return jnp.minimum(z, 0.0) - jnp.log(1.0 + jnp.exp(-jnp.abs(z)))


def _gelu_tanh(x):
    k = math.sqrt(2.0 / math.pi)
    half = 0.5 * x
    return half + half * jnp.tanh(x * (k + (k * 0.044715) * (x * x)))


def _s5_params_kernel(*refs):
    for gi in range(refs[0].shape[0]):
        _s5_group_operators(gi, *refs)


def _s5_group_operators(gi, ldt_ref, arc_ref, aic_ref, arr_ref, air_ref, bre_ref, bim_ref, cre2_ref, cim2_ref,
                        d_ref, tr_ref, o_ref, a16r_ref, a16i_ref):
    dt = jnp.exp(ldt_ref[gi])
    ar, ai = arc_ref[gi], aic_ref[gi]
    mag = jnp.exp(ar * dt)
    lr, li = mag * jnp.cos(ai * dt), mag * jnp.sin(ai * dt)
    den = ar * ar + ai * ai
    nr = lr - 1.0
    coef_r = (nr * ar + li * ai) / den
    coef_i = (li * ar - nr * ai) / den
    bb_r = coef_r * bre_ref[gi] - coef_i * bim_ref[gi]
    bb_i = coef_r * bim_ref[gi] + coef_i * bre_ref[gi]
    arr, air = arr_ref[gi], air_ref[gi]
    magr = jnp.exp(arr * dt)
    lrr, lir = magr * jnp.cos(air * dt), magr * jnp.sin(air * dt)

    lane_t = lax.broadcasted_iota(jnp.int32, (S5_STATE, CHUNK_W), 1) // S5_GROUP_CH
    lane16 = lax.broadcasted_iota(jnp.int32, (S5_GROUP_CH, CHUNK_W), 1)
    row16 = lax.broadcasted_iota(jnp.int32, (S5_GROUP_CH, CHUNK_W), 0)
    lo128 = lax.broadcasted_iota(jnp.int32, (S5_GROUP_CH, 2 * S5_STATE), 1) < S5_STATE
    c_cat = jnp.where(lo128, cre2_ref[gi], -cim2_ref[gi])
    d_col = d_ref[gi]

    expo = (CHUNK - 1) - lane_t
    pw_r = jnp.ones((S5_STATE, CHUNK_W), F32)
    pw_i = jnp.zeros((S5_STATE, CHUNK_W), F32)
    sq_r, sq_i = lr, li
    for bit in range(CHUNK.bit_length() - 1):
        take = ((expo >> bit) & 1) == 1
        pw_r, pw_i = (jnp.where(take, pw_r * sq_r - pw_i * sq_i, pw_r),
                      jnp.where(take, pw_r * sq_i + pw_i * sq_r, pw_i))
        sq_r, sq_i = sq_r * sq_r - sq_i * sq_i, 2.0 * sq_r * sq_i
    a16r_ref[gi] = jnp.broadcast_to(sq_r, (S5_STATE, LANES))
    a16i_ref[gi] = jnp.broadcast_to(sq_i, (S5_STATE, LANES))
    m_r = pw_r * bb_r - pw_i * bb_i
    m_i = pw_r * bb_i + pw_i * bb_r
    tr_ref[gi, CHUNK_W:CHUNK_W + S5_STATE, :] = m_r.astype(BF16)
    tr_ref[gi, CHUNK_W + S5_STATE:CHUNK_W + 2 * S5_STATE, :] = m_i.astype(BF16)
    k_all = jnp.dot(c_cat, jnp.concatenate([m_r, m_i], axis=0), preferred_element_type=F32, precision=HI)
    w_r, w_i = cre2_ref[gi], cim2_ref[gi]
    for tq in range(CHUNK):
        shift = (CHUNK - 1 - tq) * S5_GROUP_CH
        t_rows = k_all if shift == 0 else pltpu.roll(k_all, CHUNK_W - shift, axis=1)
        t_rows = (jnp.where(lane16 < (tq + 1) * S5_GROUP_CH, t_rows, 0.0)
                  + jnp.where(lane16 == tq * S5_GROUP_CH + row16, d_col, 0.0))
        tr_ref[gi, tq * S5_GROUP_CH:(tq + 1) * S5_GROUP_CH, :] = t_rows.astype(BF16)
        w_r, w_i = w_r * lrr - w_i * lir, w_r * lir + w_i * lrr
        o_ref[gi, tq * S5_GROUP_CH:(tq + 1) * S5_GROUP_CH, :] = jnp.where(lo128, w_r, -w_i).astype(BF16)


def _s5_params(s5_log_dt, s5_a_re, s5_a_im, s5_b_re, s5_b_im, s5_c_re, s5_c_im, s5_d):
    g, p, c = S5_GROUPS, S5_STATE, S5_GROUP_CH
    args = (
        s5_log_dt.reshape(g, 1, 1),
        s5_a_re.reshape(g, p, 1), s5_a_im.reshape(g, p, 1),
        jnp.tile(s5_a_re.reshape(g, 1, p), (1, 1, 2)), jnp.tile(s5_a_im.reshape(g, 1, p), (1, 1, 2)),
        jnp.tile(s5_b_re, (1, 1, CHUNK)), jnp.tile(s5_b_im, (1, 1, CHUNK)),
        jnp.tile(s5_c_re, (1, 1, 2)), jnp.tile(s5_c_im, (1, 1, 2)),
        s5_d.reshape(g, c, 1),
    )
    per_step = 4
    in_specs = [pl.BlockSpec((per_step,) + a.shape[1:], lambda i: (i, 0, 0)) for a in args]
    out_shape = (
        jax.ShapeDtypeStruct((g, CHUNK_W + 2 * p, CHUNK_W), BF16),
        jax.ShapeDtypeStruct((g, CHUNK_W, 2 * p), BF16),
        jax.ShapeDtypeStruct((g, p, LANES), F32),
        jax.ShapeDtypeStruct((g, p, LANES), F32),
    )
    out_specs = tuple(pl.BlockSpec((per_step,) + s.shape[1:], lambda i: (i, 0, 0)) for s in out_shape)
    return pl.pallas_call(
        _s5_params_kernel, grid=(g // per_step,), in_specs=in_specs, out_specs=out_specs, out_shape=out_shape,
        compiler_params=_cparams(1), name="s5_params")(*args)


def _pack3(v):
    hi = v.astype(BF16).astype(F32)
    rest = v - hi
    mid = rest.astype(BF16).astype(F32)
    lo = rest - mid
    return (hi + pltpu.roll(mid, N_FOX_HEADS, axis=1) + pltpu.roll(lo, 2 * N_FOX_HEADS, axis=1)).astype(BF16)


def _in_proj_kernel(x_ref, g_ref, wuf_ref, wqk_ref, wvt_ref, qg_ref, kg_ref, bias_ref, place_ref,
                    q_ref, k_ref, vt_ref, xt_ref, cs_ref, z_ref, f_ref):
    seq = x_ref.shape[1]
    nck = seq // CHUNK
    n_slab = S5_WIDTH // LANES
    grp_per_slab = LANES // S5_GROUP_CH
    tiles = [slice(r * TOK_TILE, (r + 1) * TOK_TILE) for r in range(seq // TOK_TILE)]
    gate_blocks = seq // LANES
    head_lane = lax.broadcasted_iota(jnp.int32, (LANES, LANES), 1) < N_FOX_HEADS
    lower = (lax.broadcasted_iota(jnp.int32, (LANES, LANES), 0)
             >= lax.broadcasted_iota(jnp.int32, (LANES, LANES), 1)).astype(BF16)

    def normed(rows):
        return _rms(x_ref[0, rows, :], g_ref[...]).astype(BF16)

    for rows in tiles:
        z = _dot(normed(rows), wuf_ref[...])
        for j in range(n_slab):
            z_ref[j, rows, :] = z[:, j * LANES:(j + 1) * LANES]
        f_ref[rows, :] = z[:, S5_WIDTH:]

    local = []
    for n, rows in enumerate(tiles):
        hn = normed(rows)
        qk = _dot(hn, wqk_ref[...])
        vt = _dot_nt(wvt_ref[...], hn)
        for t in range(n * CHUNK // len(tiles), (n + 1) * CHUNK // len(tiles)):
            for j in range(n_slab):
                blk = z_ref[j, pl.ds(t, nck, stride=CHUNK), :]
                xt_ref[0, j * grp_per_slab:(j + 1) * grp_per_slab, t * S5_GROUP_CH:(t + 1) * S5_GROUP_CH, :] = (
                    blk.T.reshape(grp_per_slab, S5_GROUP_CH, nck).astype(BF16))
        if n == 0:
            for blk in range(gate_blocks):
                log2f = jnp.where(head_lane, _log_sigmoid(f_ref[blk * LANES:(blk + 1) * LANES, :] + bias_ref[...])
                                  * LOG2E, 0.0)
                part = _dot(lower, _pack3(log2f))
                c = (part + pltpu.roll(part, LANES - N_FOX_HEADS, axis=1)
                     + pltpu.roll(part, LANES - 2 * N_FOX_HEADS, axis=1))
                local.append(jnp.where(head_lane, c, 0.0))
        if n == min(1, len(tiles) - 1):
            carry = jnp.zeros((1, LANES), F32)
            for blk in range(gate_blocks):
                c = local[blk] + carry
                carry = c[LANES - 1:LANES, :]
                cs_ref[0, blk * LANES:(blk + 1) * LANES, :] = _dot(_pack3(c), place_ref[...]).astype(BF16)
        q = _head64_norm(qk[:, :FOX_WIDTH]) * (qg_ref[...] * (HEAD_DIM ** -0.5 * LOG2E))
        k = _head64_norm(qk[:, FOX_WIDTH:]) * kg_ref[...]
        q_ref[0, rows, :] = q.astype(BF16)
        k_ref[0, rows, :] = k.astype(BF16)
        vt_ref[0, :, rows] = vt.astype(BF16)


def _in_proj(x, norm_mix, w_uf, w_qk, w_vt, q_gain, k_gain, bias_row, place):
    b, l, d = x.shape
    nck = l // CHUNK
    full = lambda a: pl.BlockSpec(a.shape, lambda i: (0,) * a.ndim, pipeline_mode=pl.Buffered(1))
    tok = pl.BlockSpec((1, l, FOX_WIDTH), lambda i: (i, 0, 0))
    return pl.pallas_call(
        _in_proj_kernel, grid=(b,),
        in_specs=[pl.BlockSpec((1, l, d), lambda i: (i, 0, 0)), full(norm_mix), full(w_uf), full(w_qk),
                  full(w_vt), full(q_gain), full(k_gain), full(bias_row), full(place)],
        out_specs=(tok, tok, pl.BlockSpec((1, FOX_WIDTH, l), lambda i: (i, 0, 0)),
                   pl.BlockSpec((1, S5_GROUPS, CHUNK_W, nck), lambda i: (i, 0, 0, 0)), tok),
        out_shape=(jax.ShapeDtypeStruct((b, l, FOX_WIDTH), BF16),) * 2
        + (jax.ShapeDtypeStruct((b, FOX_WIDTH, l), BF16),
           jax.ShapeDtypeStruct((b, S5_GROUPS, CHUNK_W, nck), BF16),
           jax.ShapeDtypeStruct((b, l, FOX_WIDTH), BF16)),
        scratch_shapes=[pltpu.VMEM((S5_WIDTH // LANES, l, LANES), F32), pltpu.VMEM((l, LANES), F32)],
        compiler_params=pltpu.CompilerParams(dimension_semantics=("arbitrary",),
                                             vmem_limit_bytes=IN_PROJ_VMEM_LIMIT),
        name="in_proj")(x, norm_mix, w_uf, w_qk, w_vt, q_gain, k_gain, bias_row, place)


def _fox_kernel(q_ref, k_ref, cs_ref, vt_ref, *rest, n_side):
    side_in, o_ref, side_out = rest[:n_side], rest[n_side], rest[n_side + 1:2 * n_side + 1]
    kaug_ref, s_ref = rest[2 * n_side + 1:]
    for w_ref, wb_ref in zip(side_in, side_out):
        wb_ref[...] = w_ref[...].astype(BF16)

    blk = ATT_BLOCK
    seq = q_ref.shape[1]
    n_pairs = q_ref.shape[2] // LANES
    lane_k = lax.broadcasted_iota(jnp.int32, (seq, LANES), 1)
    for pp in range(n_pairs):
        lanes = slice(pp * LANES, (pp + 1) * LANES)
        k2 = k_ref[0, :, lanes]
        cs = cs_ref[0, :, lanes]
        kaug_ref[2 * pp] = jnp.where(lane_k < HEAD_DIM, k2, cs)
        kaug_ref[2 * pp + 1] = jnp.where(lane_k >= HEAD_DIM, k2, cs)

    lane_q = lax.broadcasted_iota(jnp.int32, (blk, LANES), 1)
    minus1 = [jnp.where((lane_q >= HEAD_DIM) & (lane_q < HEAD_DIM + 3), -1.0, 0.0).astype(BF16),
              jnp.where(lane_q < 3, -1.0, 0.0).astype(BF16)]
    own = [lane_q < HEAD_DIM, lane_q >= HEAD_DIM]
    key_pos = lax.broadcasted_iota(jnp.int32, (blk, blk), 0)
    qry_pos = lax.broadcasted_iota(jnp.int32, (blk, blk), 1)

    def scores(pp, i, slot):
        q2 = q_ref[0, i * blk:(i + 1) * blk, pp * LANES:(pp + 1) * LANES]
        for hh in range(2):
            qa = jnp.where(own[hh], q2, minus1[hh])
            st = _dot_nt(kaug_ref[2 * pp + hh, 0:(i + 1) * blk, :], qa)
            if i > 0:
                s_ref[slot, hh, 0:i * blk, :] = st[:i * blk]
            s_ref[slot, hh, i * blk:(i + 1) * blk, :] = jnp.where(key_pos > qry_pos, NEG_BIG, st[i * blk:])

    def attend(pp, i, slot):
        outs = []
        for hh in range(2):
            tile = lambda j: s_ref[slot, hh, j * blk:(j + 1) * blk, :]
            m = jnp.max(tile(0), axis=0, keepdims=True)
            for j in range(1, i + 1):
                m = jnp.maximum(m, jnp.max(tile(j), axis=0, keepdims=True))
            l = jnp.zeros((1, blk), F32)
            acc = jnp.zeros((HEAD_DIM, blk), F32)
            rows = slice((2 * pp + hh) * HEAD_DIM, (2 * pp + hh + 1) * HEAD_DIM)
            for j in range(i + 1):
                p = jnp.exp2(tile(j) - m)
                l = l + jnp.sum(p, axis=0, keepdims=True)
                acc = acc + _dot(vt_ref[0, rows, j * blk:(j + 1) * blk], p.astype(BF16))
            outs.append(acc / l)
        o_ref[0, i * blk:(i + 1) * blk, pp * LANES:(pp + 1) * LANES] = (
            jnp.concatenate(outs, axis=0).T.astype(BF16))

    items = [(pp, i) for pp in range(n_pairs) for i in range(seq // blk)]
    scores(*items[0], 0)
    for n, item in enumerate(items):
        if n + 1 < len(items):
            scores(*items[n + 1], (n + 1) % 2)
        attend(*item, n % 2)


def _cast_row_block(rows, steps):
    block = 16 * pl.cdiv(pl.cdiv(rows, steps), 16)
    while rows % block:
        block += 16
    return block


def _fox_attn(q, k, cs, vt, side_weights):
    b, l, _ = q.shape
    width = FOX_PAIRS_PER_STEP * LANES
    n_pairs = FOX_WIDTH // width
    tok = pl.BlockSpec((1, l, width), lambda i, j: (i, 0, j))

    def side_spec(w):
        block = _cast_row_block(w.shape[0], b * n_pairs)
        last = w.shape[0] // block - 1
        return pl.BlockSpec((block, w.shape[1]), lambda i, j: (jnp.minimum(i * n_pairs + j, last), 0))

    side_specs = [side_spec(w) for w in side_weights]
    outs = pl.pallas_call(
        functools.partial(_fox_kernel, n_side=len(side_weights)), grid=(b, n_pairs),
        in_specs=[tok, tok, tok, pl.BlockSpec((1, width, l), lambda i, j: (i, j, 0))] + side_specs,
        out_specs=[tok] + side_specs,
        out_shape=[jax.ShapeDtypeStruct((b, l, FOX_WIDTH), BF16)]
        + [jax.ShapeDtypeStruct(w.shape, BF16) for w in side_weights],
        scratch_shapes=[pltpu.VMEM((2 * FOX_PAIRS_PER_STEP, l, LANES), BF16),
                        pltpu.VMEM((2, 2, l, ATT_BLOCK), F32)],
        compiler_params=_cparams(2), name="fox_attn")(q, k, cs, vt, *side_weights)
    return outs[0], outs[1:]


def _s5_chunk_kernel(xt_ref, tr_ref, o_ref, a16r_ref, a16i_ref, yg_ref):
    n_seq, _, _, nck = xt_ref.shape
    n_levels = nck.bit_length() - 1
    kpos = lax.broadcasted_iota(jnp.int32, (S5_STATE, nck), 1)
    a_r, a_i = a16r_ref[0], a16i_ref[0]
    coef = []
    for lvl in range(n_levels):
        keep = kpos >= (1 << lvl)
        coef.append((jnp.where(keep, a_r, 0.0), jnp.where(keep, a_i, 0.0)))
        a_r, a_i = a_r * a_r - a_i * a_i, 2.0 * a_r * a_i
    first = kpos == 0
    stage1 = [_dot(tr_ref[0], xt_ref[n, 0]) for n in range(n_seq)]
    states = [(yr[CHUNK_W:CHUNK_W + S5_STATE], yr[CHUNK_W + S5_STATE:]) for yr in stage1]
    for lvl in range(n_levels):
        c_r, c_i = coef[lvl]
        rolled = [(pltpu.roll(s_r, 1 << lvl, axis=1), pltpu.roll(s_i, 1 << lvl, axis=1)) for s_r, s_i in states]
        states = [(s_r + (c_r * p_r - c_i * p_i), s_i + (c_r * p_i + c_i * p_r))
                  for (s_r, s_i), (p_r, p_i) in zip(states, rolled)]
    entering = [jnp.concatenate([jnp.where(first, 0.0, pltpu.roll(s_r, 1, axis=1)),
                                 jnp.where(first, 0.0, pltpu.roll(s_i, 1, axis=1))], axis=0).astype(BF16)
                for s_r, s_i in states]
    carried = [_dot(o_ref[0], s_in) for s_in in entering]
    for n in range(n_seq):
        yg = _gelu_tanh(stage1[n][:CHUNK_W] + carried[n]).astype(BF16)
        yg_ref[n] = yg.reshape(CHUNK, S5_GROUP_CH, nck)


def _s5_chunk(xt, tr_op, o_op, a16r, a16i):
    b, g, _, nck = xt.shape
    per_g = lambda a: pl.BlockSpec((1,) + a.shape[1:], lambda i: (i, 0, 0))
    return pl.pallas_call(
        _s5_chunk_kernel, grid=(g,),
        in_specs=[pl.BlockSpec((b, 1, CHUNK_W, nck), lambda i: (0, i, 0, 0)),
                  per_g(tr_op), per_g(o_op), per_g(a16r), per_g(a16i)],
        out_specs=pl.BlockSpec((b, CHUNK, S5_GROUP_CH, nck), lambda i: (0, 0, i, 0)),
        out_shape=jax.ShapeDtypeStruct((b, CHUNK, S5_WIDTH, nck), BF16),
        compiler_params=_cparams(1), name="s5_chunk")(xt, tr_op, o_op, a16r, a16i)


def _s5_post_kernel(yg_ref, wgt_ref, bg_ref, gn_ref, o_ref):
    nck = yg_ref.shape[3]
    bias = jnp.concatenate([bg_ref[...]] * 2, axis=1)
    gain = jnp.concatenate([gn_ref[...]] * 2, axis=1)
    for t in range(0, CHUNK, 2):
        y = jnp.concatenate([yg_ref[0, t], yg_ref[0, t + 1]], axis=1)
        z = y.astype(F32) * _sigmoid(_dot(wgt_ref[...], y) + bias)
        zn = z * lax.rsqrt(jnp.mean(z * z, axis=0, keepdims=True) + EPS) * gain
        o_ref[0, :, t, :] = zn[:, :nck].T
        o_ref[0, :, t + 1, :] = zn[:, nck:].T


def _s5_post(yg, w_glu_t, b_glu_col, gain_col):
    batch, _, _, nck = yg.shape
    full = lambda a: pl.BlockSpec(a.shape, lambda i: (0,) * a.ndim)
    return pl.pallas_call(
        _s5_post_kernel, grid=(batch,),
        in_specs=[pl.BlockSpec((1, CHUNK, S5_WIDTH, nck), lambda i: (i, 0, 0, 0)),
                  full(w_glu_t), full(b_glu_col), full(gain_col)],
        out_specs=pl.BlockSpec((1, nck, CHUNK, S5_WIDTH), lambda i: (i, 0, 0, 0)),
        out_shape=jax.ShapeDtypeStruct((batch, nck, CHUNK, S5_WIDTH), F32),
        compiler_params=_cparams(1), name="s5_post")(yg, w_glu_t, b_glu_col, gain_col)


def _mix_cross_kernel(x_ref, fox_ref, s5_ref, gf_ref, wo_ref, gc_ref, wxq_ref, gq_ref, xk_ref, xv_ref, wxo_ref,
                      o_ref):
    halves = _row_halves(x_ref.shape[1])
    heads = [slice(hd * X_HEAD_DIM, (hd + 1) * X_HEAD_DIM) for hd in range(N_X_HEADS)]
    mixed = [jnp.concatenate([_rms(fox_ref[0, r, :].astype(F32), gf_ref[...]).astype(BF16),
                              s5_ref[0, r, :].astype(BF16)], axis=-1) for r in halves]
    proj = [_dot(m, wo_ref[...]) for m in mixed]
    h1 = [x_ref[0, r, :] + pr for r, pr in zip(halves, proj)]
    xq_raw = [_dot(_rms(h, gc_ref[...]).astype(BF16), wxq_ref[...]) for h in h1]
    xq = [(_head256_norm(v) * (gq_ref[...] * X_HEAD_DIM ** -0.5)).astype(BF16) for v in xq_raw]
    scores = [[_dot_nt(q[:, sl], xk_ref[0, :, sl]) for sl in heads] for q in xq]
    xo = []
    for sc in scores:
        outs = []
        for s, sl in zip(sc, heads):
            p = jnp.exp(s - jnp.max(s, axis=-1, keepdims=True))
            p = p / jnp.sum(p, axis=-1, keepdims=True)
            outs.append(_dot(p.astype(BF16), xv_ref[0, :, sl]).astype(BF16))
        xo.append(jnp.concatenate(outs, axis=-1))
    for r, h, v in zip(halves, h1, xo):
        o_ref[0, r, :] = h + _dot(v, wxo_ref[...])


def _mix_cross(x, fox, s5n, g_fox, w_out, g_cross, w_xq, g_xq, xk, xv, w_xo):
    b, l, d = x.shape
    tm = 2 * TOK_TILE
    tok = lambda w: pl.BlockSpec((1, tm, w), lambda i, j: (i, j, 0))
    full = lambda a: pl.BlockSpec(a.shape, lambda i, j: (0,) * a.ndim)
    mem = pl.BlockSpec((1, N_MEM, d), lambda i, j: (i, 0, 0))
    return pl.pallas_call(
        _mix_cross_kernel, grid=(b, l // tm),
        in_specs=[tok(d), tok(FOX_WIDTH), tok(S5_WIDTH), full(g_fox), full(w_out), full(g_cross),
                  full(w_xq), full(g_xq), mem, mem, full(w_xo)],
        out_specs=tok(d),
        out_shape=jax.ShapeDtypeStruct((b, l, d), F32),
        compiler_params=_cparams(2), name="mix_cross")(x, fox, s5n, g_fox, w_out, g_cross, w_xq, g_xq, xk, xv,
                                                        w_xo)


def _mem_kv_kernel(mem_ref, g_ref, wkv_ref, gk_ref, xk_ref, xv_ref):
    mn = _rms(mem_ref[0], g_ref[...]).astype(BF16)
    kv = _dot(mn, wkv_ref[...])
    xk_ref[0] = (_head256_norm(kv[:, :D_MODEL]) * gk_ref[...]).astype(BF16)
    xv_ref[0] = kv[:, D_MODEL:].astype(BF16)


def _mem_kv(mem, g_mem, w_xkv, g_xk):
    b, n, d = mem.shape
    full = lambda a: pl.BlockSpec(a.shape, lambda i: (0,) * a.ndim)
    blk = pl.BlockSpec((1, n, d), lambda i: (i, 0, 0))
    return pl.pallas_call(
        _mem_kv_kernel, grid=(b,),
        in_specs=[blk, full(g_mem), full(w_xkv), full(g_xk)],
        out_specs=(blk, blk),
        out_shape=(jax.ShapeDtypeStruct((b, n, d), BF16),) * 2,
        compiler_params=_cparams(1), name="mem_kv")(mem, g_mem, w_xkv, g_xk)


def _conv_ffn_kernel(h_ref, g_ref, wg_ref, wu_ref, cw_ref, cb_ref, wd_ref, o_ref, tail_ref, act_ref):
    @pl.when(pl.program_id(1) == 0)
    def _():
        tail_ref[...] = jnp.zeros_like(tail_ref)

    halves = _row_halves(h_ref.shape[1])
    rows_half = h_ref.shape[1] // 2
    hn = [_rms(h_ref[0, r, :], g_ref[...]).astype(BF16) for r in halves]
    row = lax.broadcasted_iota(jnp.int32, (rows_half, FF_CHUNK), 0)
    for c in range(D_FF // FF_CHUNK):
        sl = slice(c * FF_CHUNK, (c + 1) * FF_CHUNK)
        gates = [_dot(v, wg_ref[:, sl]) for v in hn]
        ups = [_dot(v, wu_ref[:, sl]) for v in hn]
        prev2, prev1 = tail_ref[0:1, sl], tail_ref[1:2, sl]
        for r, gate, up in zip(halves, gates, ups):
            g1 = jnp.where(row == 0, prev1, pltpu.roll(gate, 1, axis=0))
            g2 = jnp.where(row == 0, prev2, jnp.where(row == 1, prev1, pltpu.roll(gate, 2, axis=0)))
            conv = cb_ref[:, sl] + cw_ref[0:1, sl] * g2 + cw_ref[1:2, sl] * g1 + cw_ref[2:3, sl] * gate
            prev2, prev1 = gate[rows_half - 2:rows_half - 1, :], gate[rows_half - 1:rows_half, :]
            act_ref[r, sl] = (conv * _sigmoid(conv) * up).astype(BF16)
        tail_ref[0:1, sl] = prev2
        tail_ref[1:2, sl] = prev1
    for r in halves:
        o_ref[0, r, :] = h_ref[0, r, :] + _dot(act_ref[r, :], wd_ref[...])


def _conv_ffn(h, g_ffn, w_gate_up, conv_w, conv_b, w_down):
    b, l, d = h.shape
    assert conv_w.shape == (CONV_W, D_FF) and w_gate_up.shape == (d, 2 * D_FF)
    tm = 2 * TOK_TILE
    tok = pl.BlockSpec((1, tm, d), lambda i, j: (i, j, 0))
    full = lambda a: pl.BlockSpec(a.shape, lambda i, j: (0,) * a.ndim, pipeline_mode=pl.Buffered(1))
    half = lambda c: pl.BlockSpec((d, D_FF), lambda i, j: (0, c), pipeline_mode=pl.Buffered(1))
    return pl.pallas_call(
        _conv_ffn_kernel, grid=(b, l // tm),
        in_specs=[tok, full(g_ffn), half(0), half(1), full(conv_w), full(conv_b), full(w_down)],
        out_specs=tok,
        out_shape=jax.ShapeDtypeStruct((b, l, d), F32),
        scratch_shapes=[pltpu.VMEM((8, D_FF), F32), pltpu.VMEM((tm, D_FF), BF16)],
        compiler_params=_cparams(2), name="conv_ffn")(h, g_ffn, w_gate_up, w_gate_up, conv_w, conv_b, w_down)


def _gate_placement():
    place = np.zeros((LANES, FOX_WIDTH), np.float32)
    for h in range(N_FOX_HEADS):
        base = (h // 2) * LANES + (HEAD_DIM if h % 2 == 0 else 0)
        for term in range(3):
            place[term * N_FOX_HEADS + h, base + term] = 1.0
    return jnp.asarray(place, dtype=BF16)


def _layer(h, mem, p):
    b, l, d = h.shape
    nck = l // CHUNK
    assert nck == LANES and d == D_MODEL, "one sequence's chunks must fill exactly one lane tile"
    row = lambda a: a.reshape(1, -1)
    col = lambda a: a.reshape(-1, 1)
    w_in = p["w_in"]
    qk_end, v_end = 2 * FOX_WIDTH, 3 * FOX_WIDTH
    w_qk = w_in[:, :qk_end].astype(BF16)
    w_vt = w_in[:, qk_end:v_end].T.astype(BF16)
    w_uf = jnp.concatenate(
        [w_in[:, v_end + N_FOX_HEADS:],
         jnp.pad(w_in[:, v_end:v_end + N_FOX_HEADS], ((0, 0), (0, LANES - N_FOX_HEADS)))], axis=1).astype(BF16)
    f_bias = jnp.pad(p["fox_f_bias"], (0, LANES - N_FOX_HEADS)).reshape(1, LANES)

    tr_op, o_op, a16r, a16i = _s5_params(
        p["s5_log_dt"], p["s5_a_re"], p["s5_a_im"], p["s5_b_re"], p["s5_b_im"],
        p["s5_c_re"], p["s5_c_im"], p["s5_d"])

    q, k, vt, xt, cs = _in_proj(h, row(p["norm_mix"]), w_uf, w_qk, w_vt,
                                row(jnp.tile(p["fox_q_norm"], N_FOX_HEADS)),
                                row(jnp.tile(p["fox_k_norm"], N_FOX_HEADS)), f_bias, _gate_placement())
    fox, (w_ffn_up, w_ffn_down, w_out, w_xq, w_xo, w_xkv) = _fox_attn(
        q, k, cs, vt, [p["w_ffn_up"], p["w_ffn_down"], p["w_out"], p["w_xq"], p["w_xo"], p["w_xkv"]])

    yg = _s5_chunk(xt, tr_op, o_op, a16r, a16i)
    lane_bcast = lambda a: jnp.broadcast_to(col(a), (a.shape[0], LANES))
    s5n = _s5_post(yg, p["s5_w_glu"].T.astype(BF16), lane_bcast(p["s5_b_glu"]), lane_bcast(p["out_norm_s5"]))
    s5n = s5n.reshape(b, l, S5_WIDTH)

    xk, xv = _mem_kv(mem, row(p["norm_mem"]), w_xkv, row(jnp.tile(p["xk_norm"], N_X_HEADS)))
    h2 = _mix_cross(h, fox, s5n, row(p["out_norm_fox"]), w_out, row(p["norm_cross"]), w_xq,
                    row(jnp.tile(p["xq_norm"], N_X_HEADS)), xk, xv, w_xo)
    return _conv_ffn(h2, row(p["norm_ffn"]), w_ffn_up, p["ffn_conv_w"], row(p["ffn_conv_b"]), w_ffn_down)


_PARAM_NAMES = (
    "norm_mix", "w_in", "fox_q_norm", "fox_k_norm", "fox_f_bias", "s5_a_re", "s5_a_im", "s5_log_dt",
    "s5_b_re", "s5_b_im", "s5_c_re", "s5_c_im", "s5_d", "s5_w_glu", "s5_b_glu", "out_norm_fox",
    "out_norm_s5", "w_out", "norm_cross", "norm_mem", "w_xq", "w_xkv", "xq_norm", "xk_norm", "w_xo",
    "norm_ffn", "w_ffn_up", "ffn_conv_w", "ffn_conv_b", "w_ffn_down")


def kernel(x, mem, norm_mix, w_in, fox_q_norm, fox_k_norm, fox_f_bias, s5_a_re, s5_a_im, s5_log_dt, s5_b_re, s5_b_im, s5_c_re, s5_c_im, s5_d, s5_w_glu, s5_b_glu, out_norm_fox, out_norm_s5, w_out, norm_cross, norm_mem, w_xq, w_xkv, xq_norm, xk_norm, w_xo, norm_ffn, w_ffn_up, ffn_conv_w, ffn_conv_b, w_ffn_down):
    stacked = (norm_mix, w_in, fox_q_norm, fox_k_norm, fox_f_bias, s5_a_re, s5_a_im, s5_log_dt,
               s5_b_re, s5_b_im, s5_c_re, s5_c_im, s5_d, s5_w_glu, s5_b_glu, out_norm_fox,
               out_norm_s5, w_out, norm_cross, norm_mem, w_xq, w_xkv, xq_norm, xk_norm, w_xo,
               norm_ffn, w_ffn_up, ffn_conv_w, ffn_conv_b, w_ffn_down)
    h = x
    for layer in range(norm_mix.shape[0]):
        h = _layer(h, mem, {n: a[layer] for n, a in zip(_PARAM_NAMES, stacked)})
    return h
```

```python
import functools
import math

import jax
import jax.numpy as jnp
import numpy as np
from jax import lax
from jax.experimental import pallas as pl
from jax.experimental.pallas import tpu as pltpu

F32 = jnp.float32
BF16 = jnp.bfloat16

D_MODEL = 1024
N_MEM = 256
FOX_WIDTH = 512
HEAD_DIM = 64
N_FOX_HEADS = 8
S5_WIDTH = 512
S5_GROUP_CH = 16
S5_GROUPS = 32
S5_STATE = 64
N_X_HEADS = 4
X_HEAD_DIM = 256
D_FF = 2816
CONV_W = 3
EPS = 1e-6

LANES = 128
CHUNK = 16
CHUNK_W = CHUNK * S5_GROUP_CH
TOK_TILE = 512
ATT_BLOCK = 256
FOX_PAIRS_PER_STEP = 2
S5_GROUPS_PER_STEP = 4
FF_CHUNK = 256
NEG_BIG = -1e30
LOG2E = math.log2(math.e)
HI = lax.Precision.HIGHEST
V7X_VMEM_BYTES = 64 * 1024 * 1024
VMEM_LIMIT = V7X_VMEM_BYTES * 7 // 8
IN_PROJ_VMEM_LIMIT = V7X_VMEM_BYTES * 15 // 16

NT_DIMS = (((1,), (1,)), ((), ()))


def _cparams(n_axes):
    return pltpu.CompilerParams(dimension_semantics=("arbitrary",) * n_axes,
                                vmem_limit_bytes=VMEM_LIMIT)


def _rms(xf, g):
    return xf * lax.rsqrt(jnp.mean(xf * xf, axis=-1, keepdims=True) + EPS) * g


def _dot(a, b):
    return jnp.dot(a, b, preferred_element_type=F32)


def _dot_nt(a, b):
    return lax.dot_general(a, b, NT_DIMS, preferred_element_type=F32)


def _head64_norm(x):
    lo = lax.broadcasted_iota(jnp.int32, (x.shape[0], LANES), 1) < HEAD_DIM
    outs = []
    for j in range(x.shape[1] // LANES):
        blk = x[:, j * LANES:(j + 1) * LANES]
        sq = blk * blk
        s_lo = jnp.sum(jnp.where(lo, sq, 0.0), axis=-1, keepdims=True)
        s_hi = jnp.sum(jnp.where(lo, 0.0, sq), axis=-1, keepdims=True)
        inv = jnp.where(lo, lax.rsqrt(s_lo * (1.0 / HEAD_DIM) + EPS),
                        lax.rsqrt(s_hi * (1.0 / HEAD_DIM) + EPS))
        outs.append(blk * inv)
    return jnp.concatenate(outs, axis=-1)


def _head256_norm(x):
    outs = []
    for h in range(x.shape[1] // X_HEAD_DIM):
        blk = x[:, h * X_HEAD_DIM:(h + 1) * X_HEAD_DIM]
        outs.append(blk * lax.rsqrt(jnp.mean(blk * blk, axis=-1, keepdims=True) + EPS))
    return jnp.concatenate(outs, axis=-1)


def _row_halves(rows):
    return [slice(0, rows // 2), slice(rows // 2, rows)]


def _sigmoid(z):
    return 0.5 * jnp.tanh(0.5 * z) + 0.5


def _log_sigmoid(z):
    return jnp.minimum(z, 0.0) - jnp.log(1.0 + jnp.exp(-jnp.abs(z)))


def _gelu_tanh(x):
    k = math.sqrt(2.0 / math.pi)
    half = 0.5 * x
    return half + half * jnp.tanh(x * (k + (k * 0.044715) * (x * x)))


def _s5_params_kernel(*refs):
    for gi in range(refs[0].shape[0]):
        _s5_group_operators(gi, *refs)


def _s5_group_operators(gi, ldt_ref, arc_ref, aic_ref, arr_ref, air_ref, bre_ref, bim_ref, cre2_ref, cim2_ref,
                        d_ref, tr_ref, o_ref, a16r_ref, a16i_ref):
    dt = jnp.exp(ldt_ref[gi])
    ar, ai = arc_ref[gi], aic_ref[gi]
    mag = jnp.exp(ar * dt)
    lr, li = mag * jnp.cos(ai * dt), mag * jnp.sin(ai * dt)
    den = ar * ar + ai * ai
    nr = lr - 1.0
    coef_r = (nr * ar + li * ai) / den
    coef_i = (li * ar - nr * ai) / den
    bb_r = coef_r * bre_ref[gi] - coef_i * bim_ref[gi]
    bb_i = coef_r * bim_ref[gi] + coef_i * bre_ref[gi]
    arr, air = arr_ref[gi], air_ref[gi]
    magr = jnp.exp(arr * dt)
    lrr, lir = magr * jnp.cos(air * dt), magr * jnp.sin(air * dt)

    lane_t = lax.broadcasted_iota(jnp.int32, (S5_STATE, CHUNK_W), 1) // S5_GROUP_CH
    lane16 = lax.broadcasted_iota(jnp.int32, (S5_GROUP_CH, CHUNK_W), 1)
    row16 = lax.broadcasted_iota(jnp.int32, (S5_GROUP_CH, CHUNK_W), 0)
    lo128 = lax.broadcasted_iota(jnp.int32, (S5_GROUP_CH, 2 * S5_STATE), 1) < S5_STATE
    c_cat = jnp.where(lo128, cre2_ref[gi], -cim2_ref[gi])
    d_col = d_ref[gi]

    expo = (CHUNK - 1) - lane_t
    pw_r = jnp.ones((S5_STATE, CHUNK_W), F32)
    pw_i = jnp.zeros((S5_STATE, CHUNK_W), F32)
    sq_r, sq_i = lr, li
    for bit in range(CHUNK.bit_length() - 1):
        take = ((expo >> bit) & 1) == 1
        pw_r, pw_i = (jnp.where(take, pw_r * sq_r - pw_i * sq_i, pw_r),
                      jnp.where(take, pw_r * sq_i + pw_i * sq_r, pw_i))
        sq_r, sq_i = sq_r * sq_r - sq_i * sq_i, 2.0 * sq_r * sq_i
    a16r_ref[gi] = jnp.broadcast_to(sq_r, (S5_STATE, LANES))
    a16i_ref[gi] = jnp.broadcast_to(sq_i, (S5_STATE, LANES))
    m_r = pw_r * bb_r - pw_i * bb_i
    m_i = pw_r * bb_i + pw_i * bb_r
    tr_ref[gi, CHUNK_W:CHUNK_W + S5_STATE, :] = m_r.astype(BF16)
    tr_ref[gi, CHUNK_W + S5_STATE:CHUNK_W + 2 * S5_STATE, :] = m_i.astype(BF16)
    k_all = jnp.dot(c_cat, jnp.concatenate([m_r, m_i], axis=0), preferred_element_type=F32, precision=HI)
    w_r, w_i = cre2_ref[gi], cim2_ref[gi]
    for tq in range(CHUNK):
        shift = (CHUNK - 1 - tq) * S5_GROUP_CH
        t_rows = k_all if shift == 0 else pltpu.roll(k_all, CHUNK_W - shift, axis=1)
        t_rows = (jnp.where(lane16 < (tq + 1) * S5_GROUP_CH, t_rows, 0.0)
                  + jnp.where(lane16 == tq * S5_GROUP_CH + row16, d_col, 0.0))
        tr_ref[gi, tq * S5_GROUP_CH:(tq + 1) * S5_GROUP_CH, :] = t_rows.astype(BF16)
        w_r, w_i = w_r * lrr - w_i * lir, w_r * lir + w_i * lrr
        o_ref[gi, tq * S5_GROUP_CH:(tq + 1) * S5_GROUP_CH, :] = jnp.where(lo128, w_r, -w_i).astype(BF16)


def _s5_params(s5_log_dt, s5_a_re, s5_a_im, s5_b_re, s5_b_im, s5_c_re, s5_c_im, s5_d):
    g, p, c = S5_GROUPS, S5_STATE, S5_GROUP_CH
    args = (
        s5_log_dt.reshape(g, 1, 1),
        s5_a_re.reshape(g, p, 1), s5_a_im.reshape(g, p, 1),
        jnp.tile(s5_a_re.reshape(g, 1, p), (1, 1, 2)), jnp.tile(s5_a_im.reshape(g, 1, p), (1, 1, 2)),
        jnp.tile(s5_b_re, (1, 1, CHUNK)), jnp.tile(s5_b_im, (1, 1, CHUNK)),
        jnp.tile(s5_c_re, (1, 1, 2)), jnp.tile(s5_c_im, (1, 1, 2)),
        s5_d.reshape(g, c, 1),
    )
    per_step = 8
    in_specs = [pl.BlockSpec((per_step,) + a.shape[1:], lambda i: (i, 0, 0)) for a in args]
    out_shape = (
        jax.ShapeDtypeStruct((g, CHUNK_W + 2 * p, CHUNK_W), BF16),
        jax.ShapeDtypeStruct((g, CHUNK_W, 2 * p), BF16),
        jax.ShapeDtypeStruct((g, p, LANES), F32),
        jax.ShapeDtypeStruct((g, p, LANES), F32),
    )
    out_specs = tuple(pl.BlockSpec((per_step,) + s.shape[1:], lambda i: (i, 0, 0)) for s in out_shape)
    return pl.pallas_call(
        _s5_params_kernel, grid=(g // per_step,), in_specs=in_specs, out_specs=out_specs, out_shape=out_shape,
        compiler_params=_cparams(1), name="s5_params")(*args)


def _pack3(v):
    hi = v.astype(BF16).astype(F32)
    rest = v - hi
    mid = rest.astype(BF16).astype(F32)
    lo = rest - mid
    return (hi + pltpu.roll(mid, N_FOX_HEADS, axis=1) + pltpu.roll(lo, 2 * N_FOX_HEADS, axis=1)).astype(BF16)


def _in_proj_kernel(x_ref, g_ref, wuf_ref, wqk_ref, wvt_ref, qg_ref, kg_ref, bias_ref, place_ref,
                    q_ref, k_ref, vt_ref, xt_ref, cs_ref, z_ref, f_ref):
    seq = x_ref.shape[1]
    nck = seq // CHUNK
    n_slab = S5_WIDTH // LANES
    grp_per_slab = LANES // S5_GROUP_CH
    tiles = [slice(r * TOK_TILE, (r + 1) * TOK_TILE) for r in range(seq // TOK_TILE)]
    gate_blocks = seq // LANES
    head_lane = lax.broadcasted_iota(jnp.int32, (LANES, LANES), 1) < N_FOX_HEADS
    lower = (lax.broadcasted_iota(jnp.int32, (LANES, LANES), 0)
             >= lax.broadcasted_iota(jnp.int32, (LANES, LANES), 1)).astype(BF16)

    def normed(rows):
        return _rms(x_ref[0, rows, :], g_ref[...]).astype(BF16)

    for rows in tiles:
        z = _dot(normed(rows), wuf_ref[...])
        for j in range(n_slab):
            z_ref[j, rows, :] = z[:, j * LANES:(j + 1) * LANES]
        f_ref[rows, :] = z[:, S5_WIDTH:]

    local = []
    for n, rows in enumerate(tiles):
        hn = normed(rows)
        qk = _dot(hn, wqk_ref[...])
        vt = _dot_nt(wvt_ref[...], hn)
        for t in range(n * CHUNK // len(tiles), (n + 1) * CHUNK // len(tiles)):
            for j in range(n_slab):
                blk = z_ref[j, pl.ds(t, nck, stride=CHUNK), :]
                xt_ref[0, j * grp_per_slab:(j + 1) * grp_per_slab, t * S5_GROUP_CH:(t + 1) * S5_GROUP_CH, :] = (
                    blk.T.reshape(grp_per_slab, S5_GROUP_CH, nck).astype(BF16))
        if n == 0:
            for blk in range(gate_blocks):
                log2f = jnp.where(head_lane, _log_sigmoid(f_ref[blk * LANES:(blk + 1) * LANES, :] + bias_ref[...])
                                  * LOG2E, 0.0)
                part = _dot(lower, _pack3(log2f))
                c = (part + pltpu.roll(part, LANES - N_FOX_HEADS, axis=1)
                     + pltpu.roll(part, LANES - 2 * N_FOX_HEADS, axis=1))
                local.append(jnp.where(head_lane, c, 0.0))
        if n == min(1, len(tiles) - 1):
            carry = jnp.zeros((1, LANES), F32)
            for blk in range(gate_blocks):
                c = local[blk] + carry
                carry = c[LANES - 1:LANES, :]
                cs_ref[0, blk * LANES:(blk + 1) * LANES, :] = _dot(_pack3(c), place_ref[...]).astype(BF16)
        q = _head64_norm(qk[:, :FOX_WIDTH]) * (qg_ref[...] * (HEAD_DIM ** -0.5 * LOG2E))
        k = _head64_norm(qk[:, FOX_WIDTH:]) * kg_ref[...]
        q_ref[0, rows, :] = q.astype(BF16)
        k_ref[0, rows, :] = k.astype(BF16)
        vt_ref[0, :, rows] = vt.astype(BF16)


def _in_proj(x, norm_mix, w_uf, w_qk, w_vt, q_gain, k_gain, bias_row, place):
    b, l, d = x.shape
    nck = l // CHUNK
    full = lambda a: pl.BlockSpec(a.shape, lambda i: (0,) * a.ndim, pipeline_mode=pl.Buffered(1))
    tok = pl.BlockSpec((1, l, FOX_WIDTH), lambda i: (i, 0, 0))
    return pl.pallas_call(
        _in_proj_kernel, grid=(b,),
        in_specs=[pl.BlockSpec((1, l, d), lambda i: (i, 0, 0)), full(norm_mix), full(w_uf), full(w_qk),
                  full(w_vt), full(q_gain), full(k_gain), full(bias_row), full(place)],
        out_specs=(tok, tok, pl.BlockSpec((1, FOX_WIDTH, l), lambda i: (i, 0, 0)),
                   pl.BlockSpec((1, S5_GROUPS, CHUNK_W, nck), lambda i: (i, 0, 0, 0)), tok),
        out_shape=(jax.ShapeDtypeStruct((b, l, FOX_WIDTH), BF16),) * 2
        + (jax.ShapeDtypeStruct((b, FOX_WIDTH, l), BF16),
           jax.ShapeDtypeStruct((b, S5_GROUPS, CHUNK_W, nck), BF16),
           jax.ShapeDtypeStruct((b, l, FOX_WIDTH), BF16)),
        scratch_shapes=[pltpu.VMEM((S5_WIDTH // LANES, l, LANES), F32), pltpu.VMEM((l, LANES), F32)],
        compiler_params=pltpu.CompilerParams(dimension_semantics=("arbitrary",),
                                             vmem_limit_bytes=IN_PROJ_VMEM_LIMIT),
        name="in_proj")(x, norm_mix, w_uf, w_qk, w_vt, q_gain, k_gain, bias_row, place)


def _fox_kernel(q_ref, k_ref, cs_ref, vt_ref, *rest, n_side):
    side_in, o_ref, side_out = rest[:n_side], rest[n_side], rest[n_side + 1:2 * n_side + 1]
    kaug_ref, s_ref = rest[2 * n_side + 1:]
    for w_ref, wb_ref in zip(side_in, side_out):
        wb_ref[...] = w_ref[...].astype(BF16)

    blk = ATT_BLOCK
    seq = q_ref.shape[1]
    n_pairs = q_ref.shape[2] // LANES
    lane_k = lax.broadcasted_iota(jnp.int32, (seq, LANES), 1)
    for pp in range(n_pairs):
        lanes = slice(pp * LANES, (pp + 1) * LANES)
        k2 = k_ref[0, :, lanes]
        cs = cs_ref[0, :, lanes]
        kaug_ref[2 * pp] = jnp.where(lane_k < HEAD_DIM, k2, cs)
        kaug_ref[2 * pp + 1] = jnp.where(lane_k >= HEAD_DIM, k2, cs)

    lane_q = lax.broadcasted_iota(jnp.int32, (blk, LANES), 1)
    minus1 = [jnp.where((lane_q >= HEAD_DIM) & (lane_q < HEAD_DIM + 3), -1.0, 0.0).astype(BF16),
              jnp.where(lane_q < 3, -1.0, 0.0).astype(BF16)]
    own = [lane_q < HEAD_DIM, lane_q >= HEAD_DIM]
    key_pos = lax.broadcasted_iota(jnp.int32, (blk, blk), 0)
    qry_pos = lax.broadcasted_iota(jnp.int32, (blk, blk), 1)

    def scores(pp, i, slot):
        q2 = q_ref[0, i * blk:(i + 1) * blk, pp * LANES:(pp + 1) * LANES]
        for hh in range(2):
            qa = jnp.where(own[hh], q2, minus1[hh])
            st = _dot_nt(kaug_ref[2 * pp + hh, 0:(i + 1) * blk, :], qa)
            if i > 0:
                s_ref[slot, hh, 0:i * blk, :] = st[:i * blk]
            s_ref[slot, hh, i * blk:(i + 1) * blk, :] = jnp.where(key_pos > qry_pos, NEG_BIG, st[i * blk:])

    def attend(pp, i, slot):
        outs = []
        for hh in range(2):
            tile = lambda j: s_ref[slot, hh, j * blk:(j + 1) * blk, :]
            m = jnp.max(tile(0), axis=0, keepdims=True)
            for j in range(1, i + 1):
                m = jnp.maximum(m, jnp.max(tile(j), axis=0, keepdims=True))
            l = jnp.zeros((1, blk), F32)
            acc = jnp.zeros((HEAD_DIM, blk), F32)
            rows = slice((2 * pp + hh) * HEAD_DIM, (2 * pp + hh + 1) * HEAD_DIM)
            for j in range(i + 1):
                p = jnp.exp2(tile(j) - m)
                l = l + jnp.sum(p, axis=0, keepdims=True)
                acc = acc + _dot(vt_ref[0, rows, j * blk:(j + 1) * blk], p.astype(BF16))
            outs.append(acc / l)
        o_ref[0, i * blk:(i + 1) * blk, pp * LANES:(pp + 1) * LANES] = (
            jnp.concatenate(outs, axis=0).T.astype(BF16))

    items = [(pp, i) for pp in range(n_pairs) for i in range(seq // blk)]
    scores(*items[0], 0)
    for n, item in enumerate(items):
        if n + 1 < len(items):
            scores(*items[n + 1], (n + 1) % 2)
        attend(*item, n % 2)


def _cast_row_block(rows, steps):
    block = 16 * pl.cdiv(pl.cdiv(rows, steps), 16)
    while rows % block:
        block += 16
    return block


def _fox_attn(q, k, cs, vt, side_weights):
    b, l, _ = q.shape
    width = FOX_PAIRS_PER_STEP * LANES
    n_pairs = FOX_WIDTH // width
    tok = pl.BlockSpec((1, l, width), lambda i, j: (i, 0, j))

    def side_spec(w):
        block = _cast_row_block(w.shape[0], b * n_pairs)
        last = w.shape[0] // block - 1
        return pl.BlockSpec((block, w.shape[1]), lambda i, j: (jnp.minimum(i * n_pairs + j, last), 0))

    side_specs = [side_spec(w) for w in side_weights]
    outs = pl.pallas_call(
        functools.partial(_fox_kernel, n_side=len(side_weights)), grid=(b, n_pairs),
        in_specs=[tok, tok, tok, pl.BlockSpec((1, width, l), lambda i, j: (i, j, 0))] + side_specs,
        out_specs=[tok] + side_specs,
        out_shape=[jax.ShapeDtypeStruct((b, l, FOX_WIDTH), BF16)]
        + [jax.ShapeDtypeStruct(w.shape, BF16) for w in side_weights],
        scratch_shapes=[pltpu.VMEM((2 * FOX_PAIRS_PER_STEP, l, LANES), BF16),
                        pltpu.VMEM((2, 2, l, ATT_BLOCK), F32)],
        compiler_params=_cparams(2), name="fox_attn")(q, k, cs, vt, *side_weights)
    return outs[0], outs[1:]


def _s5_chunk_kernel(xt_ref, tr_ref, o_ref, a16r_ref, a16i_ref, yg_ref):
    n_seq, n_grp, _, nck = xt_ref.shape
    n_levels = nck.bit_length() - 1
    kpos = lax.broadcasted_iota(jnp.int32, (S5_STATE, nck), 1)
    first = kpos == 0
    coef = []
    for gi in range(n_grp):
        a_r, a_i = a16r_ref[gi], a16i_ref[gi]
        levels = []
        for lvl in range(n_levels):
            keep = kpos >= (1 << lvl)
            levels.append((jnp.where(keep, a_r, 0.0), jnp.where(keep, a_i, 0.0)))
            a_r, a_i = a_r * a_r - a_i * a_i, 2.0 * a_r * a_i
        coef.append(levels)
    tiles = [(gi, n) for gi in range(n_grp) for n in range(n_seq)]
    stage1 = [_dot(tr_ref[gi], xt_ref[n, gi]) for gi, n in tiles]
    states = [(yr[CHUNK_W:CHUNK_W + S5_STATE], yr[CHUNK_W + S5_STATE:]) for yr in stage1]
    for lvl in range(n_levels):
        rolled = [(pltpu.roll(s_r, 1 << lvl, axis=1), pltpu.roll(s_i, 1 << lvl, axis=1)) for s_r, s_i in states]
        states = [(s_r + (coef[gi][lvl][0] * p_r - coef[gi][lvl][1] * p_i),
                   s_i + (coef[gi][lvl][0] * p_i + coef[gi][lvl][1] * p_r))
                  for (gi, _), (s_r, s_i), (p_r, p_i) in zip(tiles, states, rolled)]
    entering = [jnp.concatenate([jnp.where(first, 0.0, pltpu.roll(s_r, 1, axis=1)),
                                 jnp.where(first, 0.0, pltpu.roll(s_i, 1, axis=1))], axis=0).astype(BF16)
                for s_r, s_i in states]
    carried = [_dot(o_ref[gi], s_in) for (gi, _), s_in in zip(tiles, entering)]
    for (gi, n), yr, cy in zip(tiles, stage1, carried):
        yg = _gelu_tanh(yr[:CHUNK_W] + cy).astype(BF16)
        yg_ref[n, :, gi * S5_GROUP_CH:(gi + 1) * S5_GROUP_CH, :] = yg.reshape(CHUNK, S5_GROUP_CH, nck)


def _s5_chunk(xt, tr_op, o_op, a16r, a16i):
    b, g, _, nck = xt.shape
    gs = S5_GROUPS_PER_STEP
    per_g = lambda a: pl.BlockSpec((gs,) + a.shape[1:], lambda i: (i, 0, 0))
    return pl.pallas_call(
        _s5_chunk_kernel, grid=(g // gs,),
        in_specs=[pl.BlockSpec((b, gs, CHUNK_W, nck), lambda i: (0, i, 0, 0)),
                  per_g(tr_op), per_g(o_op), per_g(a16r), per_g(a16i)],
        out_specs=pl.BlockSpec((b, CHUNK, gs * S5_GROUP_CH, nck), lambda i: (0, 0, i, 0)),
        out_shape=jax.ShapeDtypeStruct((b, CHUNK, S5_WIDTH, nck), BF16),
        compiler_params=_cparams(1), name="s5_chunk")(xt, tr_op, o_op, a16r, a16i)


def _s5_post_kernel(yg_ref, wgt_ref, bg_ref, gn_ref, o_ref):
    nck = yg_ref.shape[3]
    bias = jnp.concatenate([bg_ref[...]] * 2, axis=1)
    gain = jnp.concatenate([gn_ref[...]] * 2, axis=1)
    for s in range(yg_ref.shape[0]):
        for t in range(0, CHUNK, 2):
            y = jnp.concatenate([yg_ref[s, t], yg_ref[s, t + 1]], axis=1)
            z = y.astype(F32) * _sigmoid(_dot(wgt_ref[...], y) + bias)
            zn = z * lax.rsqrt(jnp.mean(z * z, axis=0, keepdims=True) + EPS) * gain
            o_ref[s, :, t, :] = zn[:, :nck].T
            o_ref[s, :, t + 1, :] = zn[:, nck:].T


def _s5_post(yg, w_glu_t, b_glu_col, gain_col):
    batch, _, _, nck = yg.shape
    per_step = 2
    full = lambda a: pl.BlockSpec(a.shape, lambda i: (0,) * a.ndim)
    return pl.pallas_call(
        _s5_post_kernel, grid=(batch // per_step,),
        in_specs=[pl.BlockSpec((per_step, CHUNK, S5_WIDTH, nck), lambda i: (i, 0, 0, 0)),
                  full(w_glu_t), full(b_glu_col), full(gain_col)],
        out_specs=pl.BlockSpec((per_step, nck, CHUNK, S5_WIDTH), lambda i: (i, 0, 0, 0)),
        out_shape=jax.ShapeDtypeStruct((batch, nck, CHUNK, S5_WIDTH), F32),
        compiler_params=_cparams(1), name="s5_post")(yg, w_glu_t, b_glu_col, gain_col)


def _mix_cross_kernel(x_ref, fox_ref, s5_ref, gf_ref, wo_ref, gc_ref, wxq_ref, gq_ref, xk_ref, xv_ref, wxo_ref,
                      o_ref):
    halves = _row_halves(x_ref.shape[1])
    heads = [slice(hd * X_HEAD_DIM, (hd + 1) * X_HEAD_DIM) for hd in range(N_X_HEADS)]
    mixed = [jnp.concatenate([_rms(fox_ref[0, r, :].astype(F32), gf_ref[...]).astype(BF16),
                              s5_ref[0, r, :].astype(BF16)], axis=-1) for r in halves]
    proj = [_dot(m, wo_ref[...]) for m in mixed]
    h1 = [x_ref[0, r, :] + pr for r, pr in zip(halves, proj)]
    xq_raw = [_dot(_rms(h, gc_ref[...]).astype(BF16), wxq_ref[...]) for h in h1]
    xq = [(_head256_norm(v) * (gq_ref[...] * X_HEAD_DIM ** -0.5)).astype(BF16) for v in xq_raw]
    scores = [[_dot_nt(q[:, sl], xk_ref[0, :, sl]) for sl in heads] for q in xq]
    xo = []
    for sc in scores:
        outs = []
        for s, sl in zip(sc, heads):
            p = jnp.exp(s - jnp.max(s, axis=-1, keepdims=True))
            p = p / jnp.sum(p, axis=-1, keepdims=True)
            outs.append(_dot(p.astype(BF16), xv_ref[0, :, sl]).astype(BF16))
        xo.append(jnp.concatenate(outs, axis=-1))
    for r, h, v in zip(halves, h1, xo):
        o_ref[0, r, :] = h + _dot(v, wxo_ref[...])


def _mix_cross(x, fox, s5n, g_fox, w_out, g_cross, w_xq, g_xq, xk, xv, w_xo):
    b, l, d = x.shape
    tm = 2 * TOK_TILE
    tok = lambda w: pl.BlockSpec((1, tm, w), lambda i, j: (i, j, 0))
    full = lambda a: pl.BlockSpec(a.shape, lambda i, j: (0,) * a.ndim)
    mem = pl.BlockSpec((1, N_MEM, d), lambda i, j: (i, 0, 0))
    return pl.pallas_call(
        _mix_cross_kernel, grid=(b, l // tm),
        in_specs=[tok(d), tok(FOX_WIDTH), tok(S5_WIDTH), full(g_fox), full(w_out), full(g_cross),
                  full(w_xq), full(g_xq), mem, mem, full(w_xo)],
        out_specs=tok(d),
        out_shape=jax.ShapeDtypeStruct((b, l, d), F32),
        compiler_params=_cparams(2), name="mix_cross")(x, fox, s5n, g_fox, w_out, g_cross, w_xq, g_xq, xk, xv,
                                                        w_xo)


def _mem_kv_kernel(mem_ref, g_ref, wkv_ref, gk_ref, xk_ref, xv_ref):
    mn = _rms(mem_ref[...], g_ref[...]).astype(BF16)
    kv = _dot(mn, wkv_ref[...])
    xk_ref[...] = (_head256_norm(kv[:, :D_MODEL]) * gk_ref[...]).astype(BF16)
    xv_ref[...] = kv[:, D_MODEL:].astype(BF16)


def _mem_kv(mem, g_mem, w_xkv, g_xk):
    b, n, d = mem.shape
    rows = min(2 * TOK_TILE, b * n)
    assert (b * n) % rows == 0
    full = lambda a: pl.BlockSpec(a.shape, lambda i: (0,) * a.ndim)
    blk = pl.BlockSpec((rows, d), lambda i: (i, 0))
    xk, xv = pl.pallas_call(
        _mem_kv_kernel, grid=(b * n // rows,),
        in_specs=[blk, full(g_mem), full(w_xkv), full(g_xk)],
        out_specs=(blk, blk),
        out_shape=(jax.ShapeDtypeStruct((b * n, d), BF16),) * 2,
        compiler_params=_cparams(1), name="mem_kv")(mem.reshape(b * n, d), g_mem, w_xkv, g_xk)
    return xk.reshape(b, n, d), xv.reshape(b, n, d)


def _conv_ffn_kernel(h_ref, g_ref, wg_ref, wu_ref, cw_ref, cb_ref, wd_ref, o_ref, tail_ref, act_ref):
    @pl.when(pl.program_id(1) == 0)
    def _():
        tail_ref[...] = jnp.zeros_like(tail_ref)

    halves = _row_halves(h_ref.shape[1])
    rows_half = h_ref.shape[1] // 2
    hn = [_rms(h_ref[0, r, :], g_ref[...]).astype(BF16) for r in halves]
    row = lax.broadcasted_iota(jnp.int32, (rows_half, FF_CHUNK), 0)
    for c in range(D_FF // FF_CHUNK):
        sl = slice(c * FF_CHUNK, (c + 1) * FF_CHUNK)
        gates = [_dot(v, wg_ref[:, sl]) for v in hn]
        ups = [_dot(v, wu_ref[:, sl]) for v in hn]
        prev2, prev1 = tail_ref[0:1, sl], tail_ref[1:2, sl]
        for r, gate, up in zip(halves, gates, ups):
            g1 = jnp.where(row == 0, prev1, pltpu.roll(gate, 1, axis=0))
            g2 = jnp.where(row == 0, prev2, jnp.where(row == 1, prev1, pltpu.roll(gate, 2, axis=0)))
            conv = cb_ref[:, sl] + cw_ref[0:1, sl] * g2 + cw_ref[1:2, sl] * g1 + cw_ref[2:3, sl] * gate
            prev2, prev1 = gate[rows_half - 2:rows_half - 1, :], gate[rows_half - 1:rows_half, :]
            act_ref[r, sl] = (conv * _sigmoid(conv) * up).astype(BF16)
        tail_ref[0:1, sl] = prev2
        tail_ref[1:2, sl] = prev1
    for r in halves:
        o_ref[0, r, :] = h_ref[0, r, :] + _dot(act_ref[r, :], wd_ref[...])


def _conv_ffn(h, g_ffn, w_gate_up, conv_w, conv_b, w_down):
    b, l, d = h.shape
    assert conv_w.shape == (CONV_W, D_FF) and w_gate_up.shape == (d, 2 * D_FF)
    tm = 2 * TOK_TILE
    tok = pl.BlockSpec((1, tm, d), lambda i, j: (i, j, 0))
    full = lambda a: pl.BlockSpec(a.shape, lambda i, j: (0,) * a.ndim, pipeline_mode=pl.Buffered(1))
    half = lambda c: pl.BlockSpec((d, D_FF), lambda i, j: (0, c), pipeline_mode=pl.Buffered(1))
    return pl.pallas_call(
        _conv_ffn_kernel, grid=(b, l // tm),
        in_specs=[tok, full(g_ffn), half(0), half(1), full(conv_w), full(conv_b), full(w_down)],
        out_specs=tok,
        out_shape=jax.ShapeDtypeStruct((b, l, d), F32),
        scratch_shapes=[pltpu.VMEM((8, D_FF), F32), pltpu.VMEM((tm, D_FF), BF16)],
        compiler_params=_cparams(2), name="conv_ffn")(h, g_ffn, w_gate_up, w_gate_up, conv_w, conv_b, w_down)


def _gate_placement():
    place = np.zeros((LANES, FOX_WIDTH), np.float32)
    for h in range(N_FOX_HEADS):
        base = (h // 2) * LANES + (HEAD_DIM if h % 2 == 0 else 0)
        for term in range(3):
            place[term * N_FOX_HEADS + h, base + term] = 1.0
    return jnp.asarray(place, dtype=BF16)


def _layer(h, mem, p):
    b, l, d = h.shape
    nck = l // CHUNK
    assert nck == LANES and d == D_MODEL, "one sequence's chunks must fill exactly one lane tile"
    row = lambda a: a.reshape(1, -1)
    col = lambda a: a.reshape(-1, 1)
    w_in = p["w_in"]
    qk_end, v_end = 2 * FOX_WIDTH, 3 * FOX_WIDTH
    w_qk = w_in[:, :qk_end].astype(BF16)
    w_vt = w_in[:, qk_end:v_end].T.astype(BF16)
    w_uf = jnp.concatenate(
        [w_in[:, v_end + N_FOX_HEADS:],
         jnp.pad(w_in[:, v_end:v_end + N_FOX_HEADS], ((0, 0), (0, LANES - N_FOX_HEADS)))], axis=1).astype(BF16)
    f_bias = jnp.pad(p["fox_f_bias"], (0, LANES - N_FOX_HEADS)).reshape(1, LANES)

    tr_op, o_op, a16r, a16i = _s5_params(
        p["s5_log_dt"], p["s5_a_re"], p["s5_a_im"], p["s5_b_re"], p["s5_b_im"],
        p["s5_c_re"], p["s5_c_im"], p["s5_d"])

    q, k, vt, xt, cs = _in_proj(h, row(p["norm_mix"]), w_uf, w_qk, w_vt,
                                row(jnp.tile(p["fox_q_norm"], N_FOX_HEADS)),
                                row(jnp.tile(p["fox_k_norm"], N_FOX_HEADS)), f_bias, _gate_placement())
    fox, (w_ffn_up, w_ffn_down, w_out, w_xq, w_xo, w_xkv) = _fox_attn(
        q, k, cs, vt, [p["w_ffn_up"], p["w_ffn_down"], p["w_out"], p["w_xq"], p["w_xo"], p["w_xkv"]])

    yg = _s5_chunk(xt, tr_op, o_op, a16r, a16i)
    lane_bcast = lambda a: jnp.broadcast_to(col(a), (a.shape[0], LANES))
    s5n = _s5_post(yg, p["s5_w_glu"].T.astype(BF16), lane_bcast(p["s5_b_glu"]), lane_bcast(p["out_norm_s5"]))
    s5n = s5n.reshape(b, l, S5_WIDTH)

    xk, xv = _mem_kv(mem, row(p["norm_mem"]), w_xkv, row(jnp.tile(p["xk_norm"], N_X_HEADS)))
    h2 = _mix_cross(h, fox, s5n, row(p["out_norm_fox"]), w_out, row(p["norm_cross"]), w_xq,
                    row(jnp.tile(p["xq_norm"], N_X_HEADS)), xk, xv, w_xo)
    return _conv_ffn(h2, row(p["norm_ffn"]), w_ffn_up, p["ffn_conv_w"], row(p["ffn_conv_b"]), w_ffn_down)


_PARAM_NAMES = (
    "norm_mix", "w_in", "fox_q_norm", "fox_k_norm", "fox_f_bias", "s5_a_re", "s5_a_im", "s5_log_dt",
    "s5_b_re", "s5_b_im", "s5_c_re", "s5_c_im", "s5_d", "s5_w_glu", "s5_b_glu", "out_norm_fox",
    "out_norm_s5", "w_out", "norm_cross", "norm_mem", "w_xq", "w_xkv", "xq_norm", "xk_norm", "w_xo",
    "norm_ffn", "w_ffn_up", "ffn_conv_w", "ffn_conv_b", "w_ffn_down")


def kernel(x, mem, norm_mix, w_in, fox_q_norm, fox_k_norm, fox_f_bias, s5_a_re, s5_a_im, s5_log_dt, s5_b_re, s5_b_im, s5_c_re, s5_c_im, s5_d, s5_w_glu, s5_b_glu, out_norm_fox, out_norm_s5, w_out, norm_cross, norm_mem, w_xq, w_xkv, xq_norm, xk_norm, w_xo, norm_ffn, w_ffn_up, ffn_conv_w, ffn_conv_b, w_ffn_down):
    stacked = (norm_mix, w_in, fox_q_norm, fox_k_norm, fox_f_bias, s5_a_re, s5_a_im, s5_log_dt,
               s5_b_re, s5_b_im, s5_c_re, s5_c_im, s5_d, s5_w_glu, s5_b_glu, out_norm_fox,
               out_norm_s5, w_out, norm_cross, norm_mem, w_xq, w_xkv, xq_norm, xk_norm, w_xo,
               norm_ffn, w_ffn_up, ffn_conv_w, ffn_conv_b, w_ffn_down)
    h = x
    for layer in range(norm_mix.shape[0]):
        h = _layer(h, mem, {n: a[layer] for n, a in zip(_PARAM_NAMES, stacked)})
    return h
```

```python
import functools
import math

import jax
import jax.numpy as jnp
import numpy as np
from jax import lax
from jax.experimental import pallas as pl
from jax.experimental.pallas import tpu as pltpu

F32 = jnp.float32
BF16 = jnp.bfloat16

D_MODEL = 1024
N_MEM = 256
FOX_WIDTH = 512
HEAD_DIM = 64
N_FOX_HEADS = 8
S5_WIDTH = 512
S5_GROUP_CH = 16
S5_GROUPS = 32
S5_STATE = 64
N_X_HEADS = 4
X_HEAD_DIM = 256
D_FF = 2816
CONV_W = 3
EPS = 1e-6

LANES = 128
CHUNK = 16
CHUNK_W = CHUNK * S5_GROUP_CH
TOK_TILE = 512
ATT_BLOCK = 256
FOX_PAIRS_PER_STEP = 2
S5_GROUPS_PER_STEP = 4
FF_CHUNK = 256
NEG_BIG = -1e30
LOG2E = math.log2(math.e)
HI = lax.Precision.HIGHEST
V7X_VMEM_BYTES = 64 * 1024 * 1024
VMEM_LIMIT = V7X_VMEM_BYTES * 7 // 8
IN_PROJ_VMEM_LIMIT = V7X_VMEM_BYTES * 15 // 16

NT_DIMS = (((1,), (1,)), ((), ()))


def _cparams(n_axes):
    return pltpu.CompilerParams(dimension_semantics=("arbitrary",) * n_axes,
                                vmem_limit_bytes=VMEM_LIMIT)


def _rms(xf, g):
    return xf * lax.rsqrt(jnp.mean(xf * xf, axis=-1, keepdims=True) + EPS) * g


def _dot(a, b):
    return jnp.dot(a, b, preferred_element_type=F32)


def _dot_nt(a, b):
    return lax.dot_general(a, b, NT_DIMS, preferred_element_type=F32)


def _head64_norm(x):
    lo = lax.broadcasted_iota(jnp.int32, (x.shape[0], LANES), 1) < HEAD_DIM
    outs = []
    for j in range(x.shape[1] // LANES):
        blk = x[:, j * LANES:(j + 1) * LANES]
        sq = blk * blk
        s_lo = jnp.sum(jnp.where(lo, sq, 0.0), axis=-1, keepdims=True)
        s_hi = jnp.sum(jnp.where(lo, 0.0, sq), axis=-1, keepdims=True)
        inv = jnp.where(lo, lax.rsqrt(s_lo * (1.0 / HEAD_DIM) + EPS),
                        lax.rsqrt(s_hi * (1.0 / HEAD_DIM) + EPS))
        outs.append(blk * inv)
    return jnp.concatenate(outs, axis=-1)


def _head256_norm(x):
    outs = []
    for h in range(x.shape[1] // X_HEAD_DIM):
        blk = x[:, h * X_HEAD_DIM:(h + 1) * X_HEAD_DIM]
        outs.append(blk * lax.rsqrt(jnp.mean(blk * blk, axis=-1, keepdims=True) + EPS))
    return jnp.concatenate(outs, axis=-1)


def _row_halves(rows):
    return [slice(0, rows // 2), slice(rows // 2, rows)]


def _sigmoid(z):
    return 0.5 * jnp.tanh(0.5 * z) + 0.5


def _log_sigmoid(z):
    return jnp.minimum(z, 0.0) - jnp.log(1.0 + jnp.exp(-jnp.abs(z)))


def _gelu_tanh(x):
    k = math.sqrt(2.0 / math.pi)
    half = 0.5 * x
    return half + half * jnp.tanh(x * (k + (k * 0.044715) * (x * x)))


def _s5_params_kernel(*refs):
    for gi in range(refs[0].shape[0]):
        _s5_group_operators(gi, *refs)


def _s5_group_operators(gi, ldt_ref, arc_ref, aic_ref, arr_ref, air_ref, bre_ref, bim_ref, cre2_ref, cim2_ref,
                        d_ref, tr_ref, o_ref, a16r_ref, a16i_ref):
    dt = jnp.exp(ldt_ref[gi])
    ar, ai = arc_ref[gi], aic_ref[gi]
    mag = jnp.exp(ar * dt)
    lr, li = mag * jnp.cos(ai * dt), mag * jnp.sin(ai * dt)
    den = ar * ar + ai * ai
    nr = lr - 1.0
    coef_r = (nr * ar + li * ai) / den
    coef_i = (li * ar - nr * ai) / den
    bb_r = coef_r * bre_ref[gi] - coef_i * bim_ref[gi]
    bb_i = coef_r * bim_ref[gi] + coef_i * bre_ref[gi]
    arr, air = arr_ref[gi], air_ref[gi]
    magr = jnp.exp(arr * dt)
    lrr, lir = magr * jnp.cos(air * dt), magr * jnp.sin(air * dt)

    lane_t = lax.broadcasted_iota(jnp.int32, (S5_STATE, CHUNK_W), 1) // S5_GROUP_CH
    lane16 = lax.broadcasted_iota(jnp.int32, (S5_GROUP_CH, CHUNK_W), 1)
    row16 = lax.broadcasted_iota(jnp.int32, (S5_GROUP_CH, CHUNK_W), 0)
    lo128 = lax.broadcasted_iota(jnp.int32, (S5_GROUP_CH, 2 * S5_STATE), 1) < S5_STATE
    c_cat = jnp.where(lo128, cre2_ref[gi], -cim2_ref[gi])
    d_col = d_ref[gi]

    expo = (CHUNK - 1) - lane_t
    pw_r = jnp.ones((S5_STATE, CHUNK_W), F32)
    pw_i = jnp.zeros((S5_STATE, CHUNK_W), F32)
    sq_r, sq_i = lr, li
    for bit in range(CHUNK.bit_length() - 1):
        take = ((expo >> bit) & 1) == 1
        pw_r, pw_i = (jnp.where(take, pw_r * sq_r - pw_i * sq_i, pw_r),
                      jnp.where(take, pw_r * sq_i + pw_i * sq_r, pw_i))
        sq_r, sq_i = sq_r * sq_r - sq_i * sq_i, 2.0 * sq_r * sq_i
    a16r_ref[gi] = jnp.broadcast_to(sq_r, (S5_STATE, LANES))
    a16i_ref[gi] = jnp.broadcast_to(sq_i, (S5_STATE, LANES))
    m_r = pw_r * bb_r - pw_i * bb_i
    m_i = pw_r * bb_i + pw_i * bb_r
    tr_ref[gi, CHUNK_W:CHUNK_W + S5_STATE, :] = m_r.astype(BF16)
    tr_ref[gi, CHUNK_W + S5_STATE:CHUNK_W + 2 * S5_STATE, :] = m_i.astype(BF16)
    k_all = jnp.dot(c_cat, jnp.concatenate([m_r, m_i], axis=0), preferred_element_type=F32, precision=HI)
    w_r, w_i = cre2_ref[gi], cim2_ref[gi]
    for tq in range(CHUNK):
        shift = (CHUNK - 1 - tq) * S5_GROUP_CH
        t_rows = k_all if shift == 0 else pltpu.roll(k_all, CHUNK_W - shift, axis=1)
        t_rows = (jnp.where(lane16 < (tq + 1) * S5_GROUP_CH, t_rows, 0.0)
                  + jnp.where(lane16 == tq * S5_GROUP_CH + row16, d_col, 0.0))
        tr_ref[gi, tq * S5_GROUP_CH:(tq + 1) * S5_GROUP_CH, :] = t_rows.astype(BF16)
        w_r, w_i = w_r * lrr - w_i * lir, w_r * lir + w_i * lrr
        o_ref[gi, tq * S5_GROUP_CH:(tq + 1) * S5_GROUP_CH, :] = jnp.where(lo128, w_r, -w_i).astype(BF16)


def _s5_params(s5_log_dt, s5_a_re, s5_a_im, s5_b_re, s5_b_im, s5_c_re, s5_c_im, s5_d):
    g, p, c = S5_GROUPS, S5_STATE, S5_GROUP_CH
    args = (
        s5_log_dt.reshape(g, 1, 1),
        s5_a_re.reshape(g, p, 1), s5_a_im.reshape(g, p, 1),
        jnp.tile(s5_a_re.reshape(g, 1, p), (1, 1, 2)), jnp.tile(s5_a_im.reshape(g, 1, p), (1, 1, 2)),
        jnp.tile(s5_b_re, (1, 1, CHUNK)), jnp.tile(s5_b_im, (1, 1, CHUNK)),
        jnp.tile(s5_c_re, (1, 1, 2)), jnp.tile(s5_c_im, (1, 1, 2)),
        s5_d.reshape(g, c, 1),
    )
    per_step = 4
    in_specs = [pl.BlockSpec((per_step,) + a.shape[1:], lambda i: (i, 0, 0)) for a in args]
    out_shape = (
        jax.ShapeDtypeStruct((g, CHUNK_W + 2 * p, CHUNK_W), BF16),
        jax.ShapeDtypeStruct((g, CHUNK_W, 2 * p), BF16),
        jax.ShapeDtypeStruct((g, p, LANES), F32),
        jax.ShapeDtypeStruct((g, p, LANES), F32),
    )
    out_specs = tuple(pl.BlockSpec((per_step,) + s.shape[1:], lambda i: (i, 0, 0)) for s in out_shape)
    return pl.pallas_call(
        _s5_params_kernel, grid=(g // per_step,), in_specs=in_specs, out_specs=out_specs, out_shape=out_shape,
        compiler_params=_cparams(1), name="s5_params")(*args)


def _pack3(v):
    hi = v.astype(BF16).astype(F32)
    rest = v - hi
    mid = rest.astype(BF16).astype(F32)
    lo = rest - mid
    return (hi + pltpu.roll(mid, N_FOX_HEADS, axis=1) + pltpu.roll(lo, 2 * N_FOX_HEADS, axis=1)).astype(BF16)


def _in_proj_kernel(x_ref, g_ref, wuf_ref, wqk_ref, wvt_ref, qg_ref, kg_ref, bias_ref, place_ref,
                    q_ref, k_ref, vt_ref, xt_ref, cs_ref, z_ref, f_ref):
    seq = x_ref.shape[1]
    nck = seq // CHUNK
    n_slab = S5_WIDTH // LANES
    grp_per_slab = LANES // S5_GROUP_CH
    tiles = [slice(r * TOK_TILE, (r + 1) * TOK_TILE) for r in range(seq // TOK_TILE)]
    gate_blocks = seq // LANES
    head_lane = lax.broadcasted_iota(jnp.int32, (LANES, LANES), 1) < N_FOX_HEADS
    lower = (lax.broadcasted_iota(jnp.int32, (LANES, LANES), 0)
             >= lax.broadcasted_iota(jnp.int32, (LANES, LANES), 1)).astype(BF16)

    def normed(rows):
        return _rms(x_ref[0, rows, :], g_ref[...]).astype(BF16)

    for rows in tiles:
        z = _dot(normed(rows), wuf_ref[...])
        for j in range(n_slab):
            z_ref[j, rows, :] = z[:, j * LANES:(j + 1) * LANES]
        f_ref[rows, :] = z[:, S5_WIDTH:]

    local = []
    for n, rows in enumerate(tiles):
        hn = normed(rows)
        qk = _dot(hn, wqk_ref[...])
        vt = _dot_nt(wvt_ref[...], hn)
        for t in range(n * CHUNK // len(tiles), (n + 1) * CHUNK // len(tiles)):
            for j in range(n_slab):
                blk = z_ref[j, pl.ds(t, nck, stride=CHUNK), :]
                xt_ref[0, j * grp_per_slab:(j + 1) * grp_per_slab, t * S5_GROUP_CH:(t + 1) * S5_GROUP_CH, :] = (
                    blk.T.reshape(grp_per_slab, S5_GROUP_CH, nck).astype(BF16))
        if n == 0:
            for blk in range(gate_blocks):
                log2f = jnp.where(head_lane, _log_sigmoid(f_ref[blk * LANES:(blk + 1) * LANES, :] + bias_ref[...])
                                  * LOG2E, 0.0)
                part = _dot(lower, _pack3(log2f))
                c = (part + pltpu.roll(part, LANES - N_FOX_HEADS, axis=1)
                     + pltpu.roll(part, LANES - 2 * N_FOX_HEADS, axis=1))
                local.append(jnp.where(head_lane, c, 0.0))
        if n == min(1, len(tiles) - 1):
            carry = jnp.zeros((1, LANES), F32)
            for blk in range(gate_blocks):
                c = local[blk] + carry
                carry = c[LANES - 1:LANES, :]
                cs_ref[0, blk * LANES:(blk + 1) * LANES, :] = _dot(_pack3(c), place_ref[...]).astype(BF16)
        q = _head64_norm(qk[:, :FOX_WIDTH]) * (qg_ref[...] * (HEAD_DIM ** -0.5 * LOG2E))
        k = _head64_norm(qk[:, FOX_WIDTH:]) * kg_ref[...]
        q_ref[0, rows, :] = q.astype(BF16)
        k_ref[0, rows, :] = k.astype(BF16)
        vt_ref[0, :, rows] = vt.astype(BF16)


def _in_proj(x, norm_mix, w_uf, w_qk, w_vt, q_gain, k_gain, bias_row, place):
    b, l, d = x.shape
    nck = l // CHUNK
    full = lambda a: pl.BlockSpec(a.shape, lambda i: (0,) * a.ndim, pipeline_mode=pl.Buffered(1))
    tok = pl.BlockSpec((1, l, FOX_WIDTH), lambda i: (i, 0, 0))
    return pl.pallas_call(
        _in_proj_kernel, grid=(b,),
        in_specs=[pl.BlockSpec((1, l, d), lambda i: (i, 0, 0)), full(norm_mix), full(w_uf), full(w_qk),
                  full(w_vt), full(q_gain), full(k_gain), full(bias_row), full(place)],
        out_specs=(tok, tok, pl.BlockSpec((1, FOX_WIDTH, l), lambda i: (i, 0, 0)),
                   pl.BlockSpec((1, S5_GROUPS, CHUNK_W, nck), lambda i: (i, 0, 0, 0)), tok),
        out_shape=(jax.ShapeDtypeStruct((b, l, FOX_WIDTH), BF16),) * 2
        + (jax.ShapeDtypeStruct((b, FOX_WIDTH, l), BF16),
           jax.ShapeDtypeStruct((b, S5_GROUPS, CHUNK_W, nck), BF16),
           jax.ShapeDtypeStruct((b, l, FOX_WIDTH), BF16)),
        scratch_shapes=[pltpu.VMEM((S5_WIDTH // LANES, l, LANES), F32), pltpu.VMEM((l, LANES), F32)],
        compiler_params=pltpu.CompilerParams(dimension_semantics=("arbitrary",),
                                             vmem_limit_bytes=IN_PROJ_VMEM_LIMIT),
        name="in_proj")(x, norm_mix, w_uf, w_qk, w_vt, q_gain, k_gain, bias_row, place)


def _fox_kernel(q_ref, k_ref, cs_ref, vt_ref, *rest, n_side):
    side_in, o_ref, side_out = rest[:n_side], rest[n_side], rest[n_side + 1:2 * n_side + 1]
    kaug_ref, s_ref = rest[2 * n_side + 1:]
    for w_ref, wb_ref in zip(side_in, side_out):
        wb_ref[...] = w_ref[...].astype(BF16)

    blk = ATT_BLOCK
    seq = q_ref.shape[1]
    n_pairs = q_ref.shape[2] // LANES
    lane_k = lax.broadcasted_iota(jnp.int32, (seq, LANES), 1)
    for pp in range(n_pairs):
        lanes = slice(pp * LANES, (pp + 1) * LANES)
        k2 = k_ref[0, :, lanes]
        cs = cs_ref[0, :, lanes]
        kaug_ref[2 * pp] = jnp.where(lane_k < HEAD_DIM, k2, cs)
        kaug_ref[2 * pp + 1] = jnp.where(lane_k >= HEAD_DIM, k2, cs)

    lane_q = lax.broadcasted_iota(jnp.int32, (blk, LANES), 1)
    minus1 = [jnp.where((lane_q >= HEAD_DIM) & (lane_q < HEAD_DIM + 3), -1.0, 0.0).astype(BF16),
              jnp.where(lane_q < 3, -1.0, 0.0).astype(BF16)]
    own = [lane_q < HEAD_DIM, lane_q >= HEAD_DIM]
    key_pos = lax.broadcasted_iota(jnp.int32, (blk, blk), 0)
    qry_pos = lax.broadcasted_iota(jnp.int32, (blk, blk), 1)

    def scores(pp, i, slot):
        q2 = q_ref[0, i * blk:(i + 1) * blk, pp * LANES:(pp + 1) * LANES]
        for hh in range(2):
            qa = jnp.where(own[hh], q2, minus1[hh])
            st = _dot_nt(kaug_ref[2 * pp + hh, 0:(i + 1) * blk, :], qa)
            if i > 0:
                s_ref[slot, hh, 0:i * blk, :] = st[:i * blk]
            s_ref[slot, hh, i * blk:(i + 1) * blk, :] = jnp.where(key_pos > qry_pos, NEG_BIG, st[i * blk:])

    def attend(pp, i, slot):
        outs = []
        for hh in range(2):
            tile = lambda j: s_ref[slot, hh, j * blk:(j + 1) * blk, :]
            m = jnp.max(tile(0), axis=0, keepdims=True)
            for j in range(1, i + 1):
                m = jnp.maximum(m, jnp.max(tile(j), axis=0, keepdims=True))
            l = jnp.zeros((1, blk), F32)
            acc = jnp.zeros((HEAD_DIM, blk), F32)
            rows = slice((2 * pp + hh) * HEAD_DIM, (2 * pp + hh + 1) * HEAD_DIM)
            for j in range(i + 1):
                p = jnp.exp2(tile(j) - m)
                l = l + jnp.sum(p, axis=0, keepdims=True)
                acc = acc + _dot(vt_ref[0, rows, j * blk:(j + 1) * blk], p.astype(BF16))
            outs.append(acc / l)
        o_ref[0, i * blk:(i + 1) * blk, pp * LANES:(pp + 1) * LANES] = (
            jnp.concatenate(outs, axis=0).T.astype(BF16))

    items = [(pp, i) for pp in range(n_pairs) for i in range(seq // blk)]
    scores(*items[0], 0)
    for n, item in enumerate(items):
        if n + 1 < len(items):
            scores(*items[n + 1], (n + 1) % 2)
        attend(*item, n % 2)


def _cast_row_block(rows, steps):
    block = 16 * pl.cdiv(pl.cdiv(rows, steps), 16)
    while rows % block:
        block += 16
    return block


def _fox_attn(q, k, cs, vt, side_weights):
    b, l, _ = q.shape
    width = FOX_PAIRS_PER_STEP * LANES
    n_pairs = FOX_WIDTH // width
    tok = pl.BlockSpec((1, l, width), lambda i, j: (i, 0, j))

    def side_spec(w):
        block = _cast_row_block(w.shape[0], b * n_pairs)
        last = w.shape[0] // block - 1
        return pl.BlockSpec((block, w.shape[1]), lambda i, j: (jnp.minimum(i * n_pairs + j, last), 0))

    side_specs = [side_spec(w) for w in side_weights]
    outs = pl.pallas_call(
        functools.partial(_fox_kernel, n_side=len(side_weights)), grid=(b, n_pairs),
        in_specs=[tok, tok, tok, pl.BlockSpec((1, width, l), lambda i, j: (i, j, 0))] + side_specs,
        out_specs=[tok] + side_specs,
        out_shape=[jax.ShapeDtypeStruct((b, l, FOX_WIDTH), BF16)]
        + [jax.ShapeDtypeStruct(w.shape, BF16) for w in side_weights],
        scratch_shapes=[pltpu.VMEM((2 * FOX_PAIRS_PER_STEP, l, LANES), BF16),
                        pltpu.VMEM((2, 2, l, ATT_BLOCK), F32)],
        compiler_params=_cparams(2), name="fox_attn")(q, k, cs, vt, *side_weights)
    return outs[0], outs[1:]


def _s5_chunk_kernel(xt_ref, tr_ref, o_ref, a16r_ref, a16i_ref, yg_ref):
    n_seq, n_grp, _, nck = xt_ref.shape
    n_levels = nck.bit_length() - 1
    kpos = lax.broadcasted_iota(jnp.int32, (S5_STATE, nck), 1)
    first = kpos == 0
    coef = []
    for gi in range(n_grp):
        a_r, a_i = a16r_ref[gi], a16i_ref[gi]
        levels = []
        for lvl in range(n_levels):
            keep = kpos >= (1 << lvl)
            levels.append((jnp.where(keep, a_r, 0.0), jnp.where(keep, a_i, 0.0)))
            a_r, a_i = a_r * a_r - a_i * a_i, 2.0 * a_r * a_i
        coef.append(levels)
    tiles = [(gi, n) for gi in range(n_grp) for n in range(n_seq)]
    stage1 = [_dot(tr_ref[gi], xt_ref[n, gi]) for gi, n in tiles]
    states = [(yr[CHUNK_W:CHUNK_W + S5_STATE], yr[CHUNK_W + S5_STATE:]) for yr in stage1]
    for lvl in range(n_levels):
        rolled = [(pltpu.roll(s_r, 1 << lvl, axis=1), pltpu.roll(s_i, 1 << lvl, axis=1)) for s_r, s_i in states]
        states = [(s_r + (coef[gi][lvl][0] * p_r - coef[gi][lvl][1] * p_i),
                   s_i + (coef[gi][lvl][0] * p_i + coef[gi][lvl][1] * p_r))
                  for (gi, _), (s_r, s_i), (p_r, p_i) in zip(tiles, states, rolled)]
    entering = [jnp.concatenate([jnp.where(first, 0.0, pltpu.roll(s_r, 1, axis=1)),
                                 jnp.where(first, 0.0, pltpu.roll(s_i, 1, axis=1))], axis=0).astype(BF16)
                for s_r, s_i in states]
    carried = [_dot(o_ref[gi], s_in) for (gi, _), s_in in zip(tiles, entering)]
    for (gi, n), yr, cy in zip(tiles, stage1, carried):
        yg = _gelu_tanh(yr[:CHUNK_W] + cy).astype(BF16)
        yg_ref[n, :, gi * S5_GROUP_CH:(gi + 1) * S5_GROUP_CH, :] = yg.reshape(CHUNK, S5_GROUP_CH, nck)


def _s5_chunk(xt, tr_op, o_op, a16r, a16i):
    b, g, _, nck = xt.shape
    gs = S5_GROUPS_PER_STEP
    per_g = lambda a: pl.BlockSpec((gs,) + a.shape[1:], lambda i: (i, 0, 0))
    return pl.pallas_call(
        _s5_chunk_kernel, grid=(g // gs,),
        in_specs=[pl.BlockSpec((b, gs, CHUNK_W, nck), lambda i: (0, i, 0, 0)),
                  per_g(tr_op), per_g(o_op), per_g(a16r), per_g(a16i)],
        out_specs=pl.BlockSpec((b, CHUNK, gs * S5_GROUP_CH, nck), lambda i: (0, 0, i, 0)),
        out_shape=jax.ShapeDtypeStruct((b, CHUNK, S5_WIDTH, nck), BF16),
        compiler_params=_cparams(1), name="s5_chunk")(xt, tr_op, o_op, a16r, a16i)


def _s5_post_kernel(yg_ref, wgt_ref, bg_ref, gn_ref, o_ref):
    nck = yg_ref.shape[3]
    bias = jnp.concatenate([bg_ref[...]] * 2, axis=1)
    gain = jnp.concatenate([gn_ref[...]] * 2, axis=1)
    for s in range(yg_ref.shape[0]):
        for t in range(0, CHUNK, 2):
            y = jnp.concatenate([yg_ref[s, t], yg_ref[s, t + 1]], axis=1)
            z = y.astype(F32) * _sigmoid(_dot(wgt_ref[...], y) + bias)
            zn = z * lax.rsqrt(jnp.mean(z * z, axis=0, keepdims=True) + EPS) * gain
            o_ref[s, :, t, :] = zn[:, :nck].T
            o_ref[s, :, t + 1, :] = zn[:, nck:].T


def _s5_post(yg, w_glu_t, b_glu_col, gain_col):
    batch, _, _, nck = yg.shape
    per_step = 2
    full = lambda a: pl.BlockSpec(a.shape, lambda i: (0,) * a.ndim)
    return pl.pallas_call(
        _s5_post_kernel, grid=(batch // per_step,),
        in_specs=[pl.BlockSpec((per_step, CHUNK, S5_WIDTH, nck), lambda i: (i, 0, 0, 0)),
                  full(w_glu_t), full(b_glu_col), full(gain_col)],
        out_specs=pl.BlockSpec((per_step, nck, CHUNK, S5_WIDTH), lambda i: (i, 0, 0, 0)),
        out_shape=jax.ShapeDtypeStruct((batch, nck, CHUNK, S5_WIDTH), F32),
        compiler_params=_cparams(1), name="s5_post")(yg, w_glu_t, b_glu_col, gain_col)


def _mix_cross_kernel(x_ref, fox_ref, s5_ref, gf_ref, wo_ref, gc_ref, wxq_ref, gq_ref, xk_ref, xv_ref, wxo_ref,
                      o_ref):
    halves = _row_halves(x_ref.shape[1])
    heads = [slice(hd * X_HEAD_DIM, (hd + 1) * X_HEAD_DIM) for hd in range(N_X_HEADS)]
    mixed = [jnp.concatenate([_rms(fox_ref[0, r, :].astype(F32), gf_ref[...]).astype(BF16),
                              s5_ref[0, r, :].astype(BF16)], axis=-1) for r in halves]
    proj = [_dot(m, wo_ref[...]) for m in mixed]
    h1 = [x_ref[0, r, :] + pr for r, pr in zip(halves, proj)]
    xq_raw = [_dot(_rms(h, gc_ref[...]).astype(BF16), wxq_ref[...]) for h in h1]
    xq = [(_head256_norm(v) * (gq_ref[...] * X_HEAD_DIM ** -0.5)).astype(BF16) for v in xq_raw]
    scores = [[_dot_nt(q[:, sl], xk_ref[0, :, sl]) for sl in heads] for q in xq]
    xo = []
    for sc in scores:
        outs = []
        for s, sl in zip(sc, heads):
            p = jnp.exp(s - jnp.max(s, axis=-1, keepdims=True))
            p = p / jnp.sum(p, axis=-1, keepdims=True)
            outs.append(_dot(p.astype(BF16), xv_ref[0, :, sl]).astype(BF16))
        xo.append(jnp.concatenate(outs, axis=-1))
    for r, h, v in zip(halves, h1, xo):
        o_ref[0, r, :] = h + _dot(v, wxo_ref[...])


def _mix_cross(x, fox, s5n, g_fox, w_out, g_cross, w_xq, g_xq, xk, xv, w_xo):
    b, l, d = x.shape
    tm = 2 * TOK_TILE
    tok = lambda w: pl.BlockSpec((1, tm, w), lambda i, j: (i, j, 0))
    full = lambda a: pl.BlockSpec(a.shape, lambda i, j: (0,) * a.ndim)
    mem = pl.BlockSpec((1, N_MEM, d), lambda i, j: (i, 0, 0))
    return pl.pallas_call(
        _mix_cross_kernel, grid=(b, l // tm),
        in_specs=[tok(d), tok(FOX_WIDTH), tok(S5_WIDTH), full(g_fox), full(w_out), full(g_cross),
                  full(w_xq), full(g_xq), mem, mem, full(w_xo)],
        out_specs=tok(d),
        out_shape=jax.ShapeDtypeStruct((b, l, d), F32),
        compiler_params=_cparams(2), name="mix_cross")(x, fox, s5n, g_fox, w_out, g_cross, w_xq, g_xq, xk, xv,
                                                        w_xo)


def _mem_kv_kernel(mem_ref, g_ref, wkv_ref, gk_ref, xk_ref, xv_ref):
    mn = _rms(mem_ref[...], g_ref[...]).astype(BF16)
    kv = _dot(mn, wkv_ref[...])
    xk_ref[...] = (_head256_norm(kv[:, :D_MODEL]) * gk_ref[...]).astype(BF16)
    xv_ref[...] = kv[:, D_MODEL:].astype(BF16)


def _mem_kv(mem, g_mem, w_xkv, g_xk):
    b, n, d = mem.shape
    rows = min(2 * TOK_TILE, b * n)
    assert (b * n) % rows == 0
    full = lambda a: pl.BlockSpec(a.shape, lambda i: (0,) * a.ndim)
    blk = pl.BlockSpec((rows, d), lambda i: (i, 0))
    xk, xv = pl.pallas_call(
        _mem_kv_kernel, grid=(b * n // rows,),
        in_specs=[blk, full(g_mem), full(w_xkv), full(g_xk)],
        out_specs=(blk, blk),
        out_shape=(jax.ShapeDtypeStruct((b * n, d), BF16),) * 2,
        compiler_params=_cparams(1), name="mem_kv")(mem.reshape(b * n, d), g_mem, w_xkv, g_xk)
    return xk.reshape(b, n, d), xv.reshape(b, n, d)


def _conv_ffn_kernel(h_ref, g_ref, wg_ref, wu_ref, cw_ref, cb_ref, wd_ref, o_ref, tail_ref, act_ref):
    @pl.when(pl.program_id(1) == 0)
    def _():
        tail_ref[...] = jnp.zeros_like(tail_ref)

    n_rows = h_ref.shape[1] // 4
    quarters = [slice(i * n_rows, (i + 1) * n_rows) for i in range(4)]
    hn = [_rms(h_ref[0, r, :], g_ref[...]).astype(BF16) for r in quarters]
    row = lax.broadcasted_iota(jnp.int32, (n_rows, FF_CHUNK), 0)
    for c in range(D_FF // FF_CHUNK):
        sl = slice(c * FF_CHUNK, (c + 1) * FF_CHUNK)
        gates = [_dot(v, wg_ref[:, sl]) for v in hn]
        ups = [_dot(v, wu_ref[:, sl]) for v in hn]
        prev2, prev1 = tail_ref[0:1, sl], tail_ref[1:2, sl]
        for r, gate, up in zip(quarters, gates, ups):
            g1 = jnp.where(row == 0, prev1, pltpu.roll(gate, 1, axis=0))
            g2 = jnp.where(row == 0, prev2, jnp.where(row == 1, prev1, pltpu.roll(gate, 2, axis=0)))
            conv = cb_ref[:, sl] + cw_ref[0:1, sl] * g2 + cw_ref[1:2, sl] * g1 + cw_ref[2:3, sl] * gate
            prev2, prev1 = gate[n_rows - 2:n_rows - 1, :], gate[n_rows - 1:n_rows, :]
            act_ref[r, sl] = (conv * _sigmoid(conv) * up).astype(BF16)
        tail_ref[0:1, sl] = prev2
        tail_ref[1:2, sl] = prev1
    for r in quarters:
        o_ref[0, r, :] = h_ref[0, r, :] + _dot(act_ref[r, :], wd_ref[...])


def _conv_ffn(h, g_ffn, w_gate_up, conv_w, conv_b, w_down):
    b, l, d = h.shape
    assert conv_w.shape == (CONV_W, D_FF) and w_gate_up.shape == (d, 2 * D_FF)
    tm = 2 * TOK_TILE
    tok = pl.BlockSpec((1, tm, d), lambda i, j: (i, j, 0))
    full = lambda a: pl.BlockSpec(a.shape, lambda i, j: (0,) * a.ndim, pipeline_mode=pl.Buffered(1))
    half = lambda c: pl.BlockSpec((d, D_FF), lambda i, j: (0, c), pipeline_mode=pl.Buffered(1))
    return pl.pallas_call(
        _conv_ffn_kernel, grid=(b, l // tm),
        in_specs=[tok, full(g_ffn), half(0), half(1), full(conv_w), full(conv_b), full(w_down)],
        out_specs=tok,
        out_shape=jax.ShapeDtypeStruct((b, l, d), F32),
        scratch_shapes=[pltpu.VMEM((8, D_FF), F32), pltpu.VMEM((tm, D_FF), BF16)],
        compiler_params=_cparams(2), name="conv_ffn")(h, g_ffn, w_gate_up, w_gate_up, conv_w, conv_b, w_down)


def _gate_placement():
    place = np.zeros((LANES, FOX_WIDTH), np.float32)
    for h in range(N_FOX_HEADS):
        base = (h // 2) * LANES + (HEAD_DIM if h % 2 == 0 else 0)
        for term in range(3):
            place[term * N_FOX_HEADS + h, base + term] = 1.0
    return jnp.asarray(place, dtype=BF16)


def _layer(h, mem, p):
    b, l, d = h.shape
    nck = l // CHUNK
    assert nck == LANES and d == D_MODEL, "one sequence's chunks must fill exactly one lane tile"
    row = lambda a: a.reshape(1, -1)
    col = lambda a: a.reshape(-1, 1)
    w_in = p["w_in"]
    qk_end, v_end = 2 * FOX_WIDTH, 3 * FOX_WIDTH
    w_qk = w_in[:, :qk_end].astype(BF16)
    w_vt = w_in[:, qk_end:v_end].T.astype(BF16)
    w_uf = jnp.concatenate(
        [w_in[:, v_end + N_FOX_HEADS:],
         jnp.pad(w_in[:, v_end:v_end + N_FOX_HEADS], ((0, 0), (0, LANES - N_FOX_HEADS)))], axis=1).astype(BF16)
    f_bias = jnp.pad(p["fox_f_bias"], (0, LANES - N_FOX_HEADS)).reshape(1, LANES)

    tr_op, o_op, a16r, a16i = _s5_params(
        p["s5_log_dt"], p["s5_a_re"], p["s5_a_im"], p["s5_b_re"], p["s5_b_im"],
        p["s5_c_re"], p["s5_c_im"], p["s5_d"])

    q, k, vt, xt, cs = _in_proj(h, row(p["norm_mix"]), w_uf, w_qk, w_vt,
                                row(jnp.tile(p["fox_q_norm"], N_FOX_HEADS)),
                                row(jnp.tile(p["fox_k_norm"], N_FOX_HEADS)), f_bias, _gate_placement())
    fox, (w_ffn_up, w_ffn_down, w_out, w_xq, w_xo, w_xkv) = _fox_attn(
        q, k, cs, vt, [p["w_ffn_up"], p["w_ffn_down"], p["w_out"], p["w_xq"], p["w_xo"], p["w_xkv"]])

    yg = _s5_chunk(xt, tr_op, o_op, a16r, a16i)
    lane_bcast = lambda a: jnp.broadcast_to(col(a), (a.shape[0], LANES))
    s5n = _s5_post(yg, p["s5_w_glu"].T.astype(BF16), lane_bcast(p["s5_b_glu"]), lane_bcast(p["out_norm_s5"]))
    s5n = s5n.reshape(b, l, S5_WIDTH)

    xk, xv = _mem_kv(mem, row(p["norm_mem"]), w_xkv, row(jnp.tile(p["xk_norm"], N_X_HEADS)))
    h2 = _mix_cross(h, fox, s5n, row(p["out_norm_fox"]), w_out, row(p["norm_cross"]), w_xq,
                    row(jnp.tile(p["xq_norm"], N_X_HEADS)), xk, xv, w_xo)
    return _conv_ffn(h2, row(p["norm_ffn"]), w_ffn_up, p["ffn_conv_w"], row(p["ffn_conv_b"]), w_ffn_down)


_PARAM_NAMES = (
    "norm_mix", "w_in", "fox_q_norm", "fox_k_norm", "fox_f_bias", "s5_a_re", "s5_a_im", "s5_log_dt",
    "s5_b_re", "s5_b_im", "s5_c_re", "s5_c_im", "s5_d", "s5_w_glu", "s5_b_glu", "out_norm_fox",
    "out_norm_s5", "w_out", "norm_cross", "norm_mem", "w_xq", "w_xkv", "xq_norm", "xk_norm", "w_xo",
    "norm_ffn", "w_ffn_up", "ffn_conv_w", "ffn_conv_b", "w_ffn_down")


def kernel(x, mem, norm_mix, w_in, fox_q_norm, fox_k_norm, fox_f_bias, s5_a_re, s5_a_im, s5_log_dt, s5_b_re, s5_b_im, s5_c_re, s5_c_im, s5_d, s5_w_glu, s5_b_glu, out_norm_fox, out_norm_s5, w_out, norm_cross, norm_mem, w_xq, w_xkv, xq_norm, xk_norm, w_xo, norm_ffn, w_ffn_up, ffn_conv_w, ffn_conv_b, w_ffn_down):
    stacked = (norm_mix, w_in, fox_q_norm, fox_k_norm, fox_f_bias, s5_a_re, s5_a_im, s5_log_dt,
               s5_b_re, s5_b_im, s5_c_re, s5_c_im, s5_d, s5_w_glu, s5_b_glu, out_norm_fox,
               out_norm_s5, w_out, norm_cross, norm_mem, w_xq, w_xkv, xq_norm, xk_norm, w_xo,
               norm_ffn, w_ffn_up, ffn_conv_w, ffn_conv_b, w_ffn_down)
    h = x
    for layer in range(norm_mix.shape[0]):
        h = _layer(h, mem, {n: a[layer] for n, a in zip(_PARAM_NAMES, stacked)})
    return h
```
